```python
import math
import jax
import jax.numpy as jnp
from jax import lax
import numpy as np

D_MODEL = 1024
BATCH = 8
SEQ = 4096
DEPTH = 2

GRID_W = 64
CTX_LEN = 256
ATTN_WIDTH = D_MODEL // 2
SSD_WIDTH = D_MODEL // 4
RG_WIDTH = D_MODEL // 4
MIX_WIDTH = ATTN_WIDTH + SSD_WIDTH + RG_WIDTH
HEAD_DIM = 64
ATTN_HEADS = ATTN_WIDTH // HEAD_DIM
KV_HEADS = 2
Q_PER_KV = ATTN_HEADS // KV_HEADS
KV_WIDTH = KV_HEADS * HEAD_DIM
AXIS_DIM = HEAD_DIM // 2
ROPE_THETA = 10000.0
Q_BLOCK = 128
SSD_HEAD_DIM = 64
SSD_HEADS = SSD_WIDTH // SSD_HEAD_DIM
SSD_GROUPS = 2
SSD_STATE = 64
SSD_BC_WIDTH = SSD_GROUPS * SSD_STATE
SSD_XBC_WIDTH = SSD_WIDTH + 2 * SSD_BC_WIDTH
SSD_CHUNK = 128
RG_BLOCKS = 4
RG_BLOCK_DIM = RG_WIDTH // RG_BLOCKS
RG_C = 8.0
CONV_W = 4
CONV_PAD_LO = (CONV_W - 1) // 2
CONV_PAD_HI = CONV_W // 2
N_EXPERTS = 16
EXPERT_FF = 2 * D_MODEL
EC_CAPACITY = 2
EPS = 1e-6
DEEPNORM_ALPHA = (2 * DEPTH) ** 0.25
DEEPNORM_BETA = (8 * DEPTH) ** -0.25
IN_SPLITS = (ATTN_WIDTH, KV_WIDTH, KV_WIDTH, SSD_XBC_WIDTH, SSD_WIDTH, 2 * SSD_HEADS, RG_WIDTH, RG_WIDTH)
IN_WIDTH = ATTN_WIDTH + 2 * KV_WIDTH + SSD_XBC_WIDTH + SSD_WIDTH + 2 * SSD_HEADS + 2 * RG_WIDTH

kernel_name = 'hybrid_attn_ssd_rglru_ec_moe_dit'

F32 = jnp.float32


def split_columns(t, sizes):
    out, start = [], 0
    for s in sizes:
        out.append(t[..., start:start + s])
        start += s
    return out


def layer_norm(x, w, b):
    xf = x.astype(F32)
    mu = jnp.mean(xf, axis=-1, keepdims=True)
    var = jnp.mean(jnp.square(xf - mu), axis=-1, keepdims=True)
    return ((xf - mu) * lax.rsqrt(var + EPS) * w + b).astype(x.dtype)


def rms_norm(x, w):
    xf = x.astype(F32)
    return (xf * lax.rsqrt(jnp.mean(xf * xf, axis=-1, keepdims=True) + EPS) * w).astype(x.dtype)


def dw_conv(x, w, b):
    y = lax.conv_general_dilated(x, w[:, None, :].astype(x.dtype), window_strides=(1,),
                                 padding=[(CONV_PAD_LO, CONV_PAD_HI)],
                                 dimension_numbers=('NWC', 'WIO', 'NWC'),
                                 feature_group_count=x.shape[-1])
    return y + b


def to_heads(t, n_heads):
    return t.reshape(t.shape[0], t.shape[1], n_heads, HEAD_DIM)


def axial_rope_tables(n):
    rows = n // GRID_W
    row = jnp.repeat(jnp.arange(rows, dtype=F32), GRID_W)
    col = jnp.tile(jnp.arange(GRID_W, dtype=F32), rows)
    inv_freq = ROPE_THETA ** (-jnp.arange(0, AXIS_DIM, 2, dtype=F32) / AXIS_DIM)
    ang_r = row[:, None] * inv_freq
    ang_c = col[:, None] * inv_freq
    return (jnp.cos(ang_r), jnp.sin(ang_r), jnp.cos(ang_c), jnp.sin(ang_c))


def _rotate_half(t, cos, sin):
    t1, t2 = jnp.split(t, 2, axis=-1)
    cos = cos[:, None, :]
    sin = sin[:, None, :]
    return jnp.concatenate([t1 * cos - t2 * sin, t2 * cos + t1 * sin], axis=-1)


def apply_axial_rope(t, rope):
    cos_r, sin_r, cos_c, sin_c = rope
    out = jnp.concatenate([_rotate_half(t[..., :AXIS_DIM], cos_r, sin_r),
                           _rotate_half(t[..., AXIS_DIM:], cos_c, sin_c)], axis=-1)
    return out.astype(t.dtype)


def attend(q, k, v):
    s = jnp.einsum('bqkgd,blkd->bkgql', q, k).astype(F32) * (HEAD_DIM ** -0.5)
    p = jax.nn.softmax(s, axis=-1).astype(v.dtype)
    return jnp.einsum('bkgql,blkd->bqkgd', p, v)


def blocked_attention(q, k, v):
    b, n = q.shape[0], q.shape[1]
    nblk = n // Q_BLOCK
    qb = q.reshape(b, nblk, Q_BLOCK, KV_HEADS, Q_PER_KV, HEAD_DIM).transpose(1, 0, 2, 3, 4, 5)
    o = lax.map(lambda blk: attend(blk, k, v), qb)
    return o.transpose(1, 0, 2, 3, 4, 5).reshape(b, n, ATTN_WIDTH)


def segsum_from_cumsum(cs):
    t = cs.shape[-1]
    diff = cs[..., :, None] - cs[..., None, :]
    return jnp.where(jnp.tril(jnp.ones((t, t), dtype=bool)), diff, -jnp.inf)


def ssd_scan(xh, dt, a, bh, ch, h0):
    b, l, h, p = xh.shape
    n = bh.shape[-1]
    nc = l // SSD_CHUNK
    xdt = (xh * dt[..., None]).reshape(b, nc, SSD_CHUNK, h, p)
    bc = bh.reshape(b, nc, SSD_CHUNK, h, n)
    cc = ch.reshape(b, nc, SSD_CHUNK, h, n)
    da = (dt * a).reshape(b, nc, SSD_CHUNK, h).transpose(0, 3, 1, 2)
    da_cs = jnp.cumsum(da, axis=-1)
    l_mat = jnp.exp(segsum_from_cumsum(da_cs))
    y_diag = jnp.einsum('bclhn,bcshn,bhcls,bcshp->bclhp', cc, bc, l_mat, xdt)
    decay_states = jnp.exp(da_cs[..., -1:] - da_cs)
    states = jnp.einsum('bclhn,bhcl,bclhp->bchpn', bc, decay_states, xdt)
    states = jnp.concatenate([h0[:, None].astype(states.dtype), states], axis=1)
    chunk_tot = jnp.pad(da_cs[..., -1], ((0, 0), (0, 0), (1, 0)))
    decay_chunk = jnp.exp(segsum_from_cumsum(jnp.cumsum(chunk_tot, axis=-1)))
    new_states = jnp.einsum('bhzc,bchpn->bzhpn', decay_chunk, states)
    states_in, h_final = new_states[:, :-1], new_states[:, -1]
    y_off = jnp.einsum('bclhn,bchpn,bhcl->bclhp', cc, states_in, jnp.exp(da_cs))
    return (y_diag + y_off).reshape(b, l, h, p), h_final


def ssd_stream(xbc, dt_raw, lp, h0_f, h0_b):
    b, n, _ = xbc.shape
    xbc = jax.nn.silu(dw_conv(xbc, lp['ssd_conv_w'], lp['ssd_conv_b']))
    xs, bs, cs = split_columns(xbc, (SSD_WIDTH, SSD_BC_WIDTH, SSD_BC_WIDTH))
    xh = xs.reshape(b, n, SSD_HEADS, SSD_HEAD_DIM)
    rep = SSD_HEADS // SSD_GROUPS
    bh = jnp.repeat(bs.reshape(b, n, SSD_GROUPS, SSD_STATE), rep, axis=2)
    ch = jnp.repeat(cs.reshape(b, n, SSD_GROUPS, SSD_STATE), rep, axis=2)
    dt = jax.nn.softplus(dt_raw.astype(F32).reshape(b, n, 2, SSD_HEADS) + lp['ssd_dt_bias'].astype(F32))
    a = -jnp.exp(lp['ssd_a_log'].astype(F32))
    flip = lambda t: jnp.flip(t, axis=1)
    y_f, s_f = ssd_scan(xh, dt[:, :, 0], a[0], bh, ch, h0_f)
    y_b, s_b = ssd_scan(flip(xh), flip(dt[:, :, 1]), a[1], flip(bh), flip(ch), h0_b)
    y = y_f + flip(y_b) + lp['ssd_d'][:, None] * xh
    return y.reshape(b, n, SSD_WIDTH).astype(xs.dtype), s_f, s_b


def linear_scan(a, u, h0):
    u = u.at[:, 0].add(a[:, 0] * h0)

    def combine(left, right):
        a_l, u_l = left
        a_r, u_r = right
        return a_l * a_r, a_r * u_l + u_r

    _, h = lax.associative_scan(combine, (a, u), axis=1)
    return h


def rglru_stream(u_raw, lp, h0_f, h0_b):
    b, n, _ = u_raw.shape
    u = dw_conv(u_raw, lp['rg_conv_w'], lp['rg_conv_b'])
    ub = u.reshape(b, n, RG_BLOCKS, RG_BLOCK_DIM)

    def block_gate(w, bias):
        pre = jnp.einsum('bnkd,jkde->jbnke', ub, w).reshape(2, b, n, RG_WIDTH) + bias[:, None, None, :]
        return jax.nn.sigmoid(pre.astype(F32))

    r = block_gate(lp['rg_wa'], lp['rg_ba'])
    i = block_gate(lp['rg_wx'], lp['rg_bx'])
    log_a = -RG_C * r * jax.nn.softplus(-lp['rg_lambda'].astype(F32))[:, None, None, :]
    a = jnp.exp(log_a)
    inp = jnp.sqrt(-jnp.expm1(2.0 * log_a)) * i * u.astype(F32)[None]
    h_f = linear_scan(a[0], inp[0], h0_f)
    h_b = jnp.flip(linear_scan(jnp.flip(a[1], axis=1), jnp.flip(inp[1], axis=1), h0_b), axis=1)
    return (h_f + h_b).astype(u.dtype), h_f[:, -1], h_b[:, 0]


def expert_choice_ffn(h, lp):
    b, n, d = h.shape
    cap = EC_CAPACITY * n // N_EXPERTS
    aff = jax.nn.softmax(jnp.einsum('bnd,de->ben', h, lp['w_router']).astype(F32), axis=1)
    gate, idx = lax.top_k(aff, cap)
    xs = jax.vmap(lambda hb, ib: hb[ib])(h, idx)
    hid = jax.nn.silu(jnp.einsum('becd,edf->becf', xs, lp['w_gate'])) * jnp.einsum('becd,edf->becf', xs, lp['w_up'])
    ys = jnp.einsum('becf,efd->becd', hid, lp['w_down']) * gate[..., None].astype(h.dtype)
    return jax.vmap(lambda ib, yb: jnp.zeros((n, d), yb.dtype).at[ib.reshape(-1)].add(yb.reshape(-1, d)))(idx, ys)


def hybrid_layer(x, ctx, mod_x, mod_c, rope, lp, last):
    b = x.shape[0]
    m = ctx.shape[1]
    sh1, sc1, g1, sh2, sc2, g2 = jnp.split(mod_x[:, None, :], 6, axis=-1)
    csh1, csc1, cg1, csh2, csc2, cg2 = jnp.split(mod_c, 6, axis=-1)

    hx = x * (1 + sc1) + sh1
    hc = ctx * (1 + csc1) + csh1
    qx, kx, vx, xbc_x, zx, dtx, ux, gx = split_columns(hx @ lp['w_in'], IN_SPLITS)
    qc, kc, vc, xbc_c, zc, dtc, uc, gc = split_columns(hc @ lp['w_in'], IN_SPLITS)

    k_ctx = rms_norm(to_heads(kc, KV_HEADS), lp['k_norm'])
    v_ctx = to_heads(vc, KV_HEADS)
    q_lat = apply_axial_rope(rms_norm(to_heads(qx, ATTN_HEADS), lp['q_norm']), rope)
    k_lat = apply_axial_rope(rms_norm(to_heads(kx, KV_HEADS), lp['k_norm']), rope)
    k_all = jnp.concatenate([k_ctx, k_lat], axis=1)
    v_all = jnp.concatenate([v_ctx, to_heads(vx, KV_HEADS)], axis=1)
    attn_x = blocked_attention(q_lat, k_all, v_all)

    s0 = jnp.zeros((b, SSD_HEADS, SSD_HEAD_DIM, SSD_STATE), F32)
    ssd_c, s_f, s_b = ssd_stream(xbc_c, dtc, lp, s0, s0)
    ssd_x, _, _ = ssd_stream(xbc_x, dtx, lp, s_f, s_b)

    r0 = jnp.zeros((b, RG_WIDTH), F32)
    rg_c, r_f, r_b = rglru_stream(uc, lp, r0, r0)
    rg_x, _, _ = rglru_stream(ux, lp, r_f, r_b)

    def merge(attn, ssd, z, rg, gate):
        groups = [rms_norm(attn, lp['attn_out_norm']),
                  rms_norm(ssd * jax.nn.silu(z), lp['ssd_norm']),
                  rms_norm(rg * jax.nn.gelu(gate), lp['rg_out_norm'])]
        return jnp.concatenate(groups, axis=-1) @ lp['w_out']

    x = layer_norm(DEEPNORM_ALPHA * x + g1 * merge(attn_x, ssd_x, zx, rg_x, gx), lp['ln1_w'], lp['ln1_b'])
    if not last:
        q_ctx = rms_norm(to_heads(qc, ATTN_HEADS), lp['q_norm']).reshape(b, m, KV_HEADS, Q_PER_KV, HEAD_DIM)
        attn_c = attend(q_ctx, k_ctx, v_ctx).reshape(b, m, ATTN_WIDTH)
        ctx = layer_norm(DEEPNORM_ALPHA * ctx + cg1 * merge(attn_c, ssd_c, zc, rg_c, gc), lp['ln1_w'], lp['ln1_b'])

    x = layer_norm(DEEPNORM_ALPHA * x + g2 * expert_choice_ffn(x * (1 + sc2) + sh2, lp), lp['ln2_w'], lp['ln2_b'])
    if not last:
        ctx = layer_norm(DEEPNORM_ALPHA * ctx + cg2 * expert_choice_ffn(ctx * (1 + csc2) + csh2, lp), lp['ln2_w'], lp['ln2_b'])
    return x, ctx


def setup_inputs(seed: int = 0) -> dict:
    key = jax.random.key(seed)
    keys = iter(jax.random.split(key, 48))
    nrm = lambda shape, scale: jax.random.normal(next(keys), shape, F32) * scale
    L, D = DEPTH, D_MODEL
    dt0 = jnp.exp(jax.random.uniform(next(keys), (L, 2, SSD_HEADS), F32, math.log(1e-3), math.log(1e-1)))
    a0 = jax.random.uniform(next(keys), (L, 2, RG_WIDTH), F32, 0.9, 0.999)
    return {
        'x': nrm((BATCH, SEQ, D), 1.0),
        'c': nrm((BATCH, D), 1.0),
        'ctx': nrm((BATCH, CTX_LEN, D), 1.0),
        'c_ctx': nrm((D,), 1.0),
        'w_mod': nrm((L, D, 6 * D), 0.5 * D ** -0.5),
        'b_mod': nrm((L, 6 * D), 0.02),
        'w_in': nrm((L, D, IN_WIDTH), D ** -0.5),
        'q_norm': 1.0 + nrm((L, HEAD_DIM), 0.02),
        'k_norm': 1.0 + nrm((L, HEAD_DIM), 0.02),
        'attn_out_norm': 1.0 + nrm((L, ATTN_WIDTH), 0.02),
        'ssd_conv_w': nrm((L, CONV_W, SSD_XBC_WIDTH), CONV_W ** -0.5),
        'ssd_conv_b': nrm((L, SSD_XBC_WIDTH), 0.02),
        'ssd_dt_bias': dt0 + jnp.log(-jnp.expm1(-dt0)),
        'ssd_a_log': jnp.log(jax.random.uniform(next(keys), (L, 2, SSD_HEADS), F32, 1.0, 16.0)),
        'ssd_d': 1.0 + nrm((L, SSD_HEADS), 0.02),
        'ssd_norm': 1.0 + nrm((L, SSD_WIDTH), 0.02),
        'rg_conv_w': nrm((L, CONV_W, RG_WIDTH), CONV_W ** -0.5),
        'rg_conv_b': nrm((L, RG_WIDTH), 0.02),
        'rg_wa': nrm((L, 2, RG_BLOCKS, RG_BLOCK_DIM, RG_BLOCK_DIM), RG_BLOCK_DIM ** -0.5),
        'rg_ba': nrm((L, 2, RG_WIDTH), 0.02),
        'rg_wx': nrm((L, 2, RG_BLOCKS, RG_BLOCK_DIM, RG_BLOCK_DIM), RG_BLOCK_DIM ** -0.5),
        'rg_bx': nrm((L, 2, RG_WIDTH), 0.02),
        'rg_lambda': jnp.log(a0) - jnp.log1p(-a0),
        'rg_out_norm': 1.0 + nrm((L, RG_WIDTH), 0.02),
        'w_out': nrm((L, MIX_WIDTH, D), DEEPNORM_BETA * MIX_WIDTH ** -0.5),
        'ln1_w': 1.0 + nrm((L, D), 0.02),
        'ln1_b': nrm((L, D), 0.02),
        'w_router': nrm((L, D, N_EXPERTS), D ** -0.5),
        'w_gate': nrm((L, N_EXPERTS, D, EXPERT_FF), D ** -0.5),
        'w_up': nrm((L, N_EXPERTS, D, EXPERT_FF), D ** -0.5),
        'w_down': nrm((L, N_EXPERTS, EXPERT_FF, D), DEEPNORM_BETA * EXPERT_FF ** -0.5),
        'ln2_w': 1.0 + nrm((L, D), 0.02),
        'ln2_b': nrm((L, D), 0.02),
    }


def reference(x, c, ctx, c_ctx, w_mod, b_mod, w_in, q_norm, k_norm, attn_out_norm,
              ssd_conv_w, ssd_conv_b, ssd_dt_bias, ssd_a_log, ssd_d, ssd_norm,
              rg_conv_w, rg_conv_b, rg_wa, rg_ba, rg_wx, rg_bx, rg_lambda, rg_out_norm,
              w_out, ln1_w, ln1_b, w_router, w_gate, w_up, w_down, ln2_w, ln2_b):
    rope = axial_rope_tables(x.shape[1])
    for l in range(DEPTH):
        lp = dict(w_in=w_in[l], q_norm=q_norm[l], k_norm=k_norm[l], attn_out_norm=attn_out_norm[l],
                  ssd_conv_w=ssd_conv_w[l], ssd_conv_b=ssd_conv_b[l], ssd_dt_bias=ssd_dt_bias[l],
                  ssd_a_log=ssd_a_log[l], ssd_d=ssd_d[l], ssd_norm=ssd_norm[l],
                  rg_conv_w=rg_conv_w[l], rg_conv_b=rg_conv_b[l], rg_wa=rg_wa[l], rg_ba=rg_ba[l],
                  rg_wx=rg_wx[l], rg_bx=rg_bx[l], rg_lambda=rg_lambda[l], rg_out_norm=rg_out_norm[l],
                  w_out=w_out[l], ln1_w=ln1_w[l], ln1_b=ln1_b[l], w_router=w_router[l],
                  w_gate=w_gate[l], w_up=w_up[l], w_down=w_down[l], ln2_w=ln2_w[l], ln2_b=ln2_b[l])
        mod_x = jax.nn.silu(c) @ w_mod[l] + b_mod[l]
        mod_c = jax.nn.silu(c_ctx) @ w_mod[l] + b_mod[l]
        x, ctx = hybrid_layer(x, ctx, mod_x, mod_c, rope, lp, l == DEPTH - 1)
    return x
```

```python
import functools
import math

import jax
import jax.numpy as jnp
from jax import lax
from jax.experimental import pallas as pl
from jax.experimental.pallas import tpu as pltpu

F32 = jnp.float32
BF16 = jnp.bfloat16
I32 = jnp.int32

DEPTH = 2
GRID_W = 64
HEAD_DIM = 64
KV_HEADS = 2
Q_PER_KV = 4
AXIS_DIM = HEAD_DIM // 2
ROPE_THETA = 10000.0
SSD_HEADS = 4
SSD_HEAD_DIM = 64
SSD_STATE = 64
RG_BLOCKS = 4
RG_C = 8.0
CONV_W = 4
N_EXPERTS = 16
EC_CAPACITY = 2
EPS = 1e-6
DEEPNORM_ALPHA = (2 * DEPTH) ** 0.25

LANES = 128
SUBLANES = 8
SEQ_TILE = 256
VMEM_LIMIT = 48 * 1024 * 1024
VMEM_LIMIT_MOE = 60 * 1024 * 1024


def _cparams(limit=VMEM_LIMIT):
    return pltpu.CompilerParams(vmem_limit_bytes=limit)


def _sigmoid(x):
    return 1.0 / (1.0 + jnp.exp(-x))


def _silu(x):
    return x * _sigmoid(x)


def _softplus(x):
    return jnp.maximum(x, 0.0) + jnp.log1p(jnp.exp(-jnp.abs(x)))


def _bdot(a, b):
    return jnp.dot(a.astype(BF16), b.astype(BF16), preferred_element_type=F32)


def _mod_body(c_ref, w_ref, b_ref, o_ref):
    c = c_ref[...]
    o_ref[0] = _bdot(_silu(c), w_ref[0]) + b_ref[0]


def _modulation(c, c_ctx, w_mod, b_mod):
    depth, d, n6 = w_mod.shape
    b = c.shape[0]
    rows = 2 * SUBLANES
    cc = jnp.zeros((rows, d), F32).at[:b].set(c).at[b].set(c_ctx)
    tn = 1536
    return pl.pallas_call(
        _mod_body,
        grid=(depth, n6 // tn),
        in_specs=[
            pl.BlockSpec((rows, d), lambda l, j: (0, 0)),
            pl.BlockSpec((1, d, tn), lambda l, j: (l, 0, j)),
            pl.BlockSpec((1, 1, tn), lambda l, j: (l, 0, j)),
        ],
        out_specs=pl.BlockSpec((1, rows, tn), lambda l, j: (l, 0, j)),
        out_shape=jax.ShapeDtypeStruct((depth, rows, n6), F32),
        compiler_params=_cparams(),
        name="adaln_mod",
    )(cc, w_mod, b_mod.reshape(depth, 1, n6))


def _mod_spec(j, d, ctx_row, off):
    if off == 0:
        return pl.BlockSpec((1, 1, d), lambda b, i: (jnp.where(i == 0, ctx_row, b), 0, j))
    return pl.BlockSpec((1, 1, d), lambda b, i: (b, 0, j))


W_Q, W_K, W_V, W_XU, W_Z, W_G, W_DT = 512, 128, 128, 768, 256, 256, 256
IN_PAD = W_Q + W_K + W_V + W_XU + W_Z + W_G + W_DT


def _rope(t, cos, sin_signed):
    rows = t.shape[0]
    lane = lax.broadcasted_iota(I32, (rows, LANES), 1)
    first = (lane % AXIS_DIM) < (AXIS_DIM // 2)
    outs = []
    for c in range(t.shape[1] // LANES):
        tc = t[:, c * LANES:(c + 1) * LANES]
        partner = jnp.where(first, pltpu.roll(tc, LANES - AXIS_DIM // 2, 1), pltpu.roll(tc, AXIS_DIM // 2, 1))
        outs.append(tc * cos + partner * sin_signed)
    return outs[0] if len(outs) == 1 else jnp.concatenate(outs, axis=1)


def _inproj_body(x_ref, sc_ref, sh_ref, w_ref, cos_ref, sin_ref, qw_ref, kw_ref, oq_ref, ok_ref,
                 q_ref, k_ref, v_ref, xu_ref, z_ref, g_ref, dt_ref):
    h = x_ref[0] * (1.0 + sc_ref[0]) + sh_ref[0]
    y = jnp.dot(h.astype(BF16), w_ref[...], preferred_element_type=F32)
    o = 0
    q = y[:, o:o + W_Q]; o += W_Q
    k = y[:, o:o + W_K]; o += W_K
    v = y[:, o:o + W_V]; o += W_V
    xu_ref[0] = y[:, o:o + W_XU]; o += W_XU
    z_ref[0] = y[:, o:o + W_Z]; o += W_Z
    g_ref[0] = y[:, o:o + W_G]; o += W_G
    dt_ref[0] = y[:, o:o + W_DT]
    cos = cos_ref[...]
    sin = sin_ref[...]
    ssq = jnp.dot((q * q).astype(BF16), oq_ref[...], preferred_element_type=F32)
    qn = q * lax.rsqrt(ssq * (1.0 / HEAD_DIM) + EPS) * qw_ref[...]
    q_ref[0] = (_rope(qn, cos, sin) * (HEAD_DIM ** -0.5)).astype(BF16)
    ssk = jnp.dot((k * k).astype(BF16), ok_ref[...], preferred_element_type=F32)
    kn = k * lax.rsqrt(ssk * (1.0 / HEAD_DIM) + EPS) * kw_ref[...]
    k_ref[0] = _rope(kn, cos, sin).astype(BF16)
    v_ref[0] = v.astype(BF16)


def _inproj(xt, modl, wcat, cos_t, sin_t, qw, kw, ones_q, ones_k, ctx_row):
    b, t, d = xt.shape
    tl = SEQ_TILE
    grid = (b, t // tl)
    full = lambda shape: pl.BlockSpec(shape, lambda bb, i: (0,) * len(shape))
    seq = lambda w: pl.BlockSpec((1, tl, w), lambda bb, i: (bb, i, 0))
    outs = [(W_Q, BF16), (W_K, BF16), (W_V, BF16), (W_XU, F32), (W_Z, F32), (W_G, F32), (W_DT, F32)]
    return pl.pallas_call(
        _inproj_body,
        grid=grid,
        in_specs=[
            seq(d),
            _mod_spec(1, d, ctx_row, 0),
            _mod_spec(0, d, ctx_row, 0),
            full((d, IN_PAD)),
            pl.BlockSpec((tl, LANES), lambda bb, i: (i, 0)),
            pl.BlockSpec((tl, LANES), lambda bb, i: (i, 0)),
            full((1, W_Q)), full((1, W_K)), full((W_Q, W_Q)), full((W_K, W_K)),
        ],
        out_specs=[seq(w) for w, _ in outs],
        out_shape=[jax.ShapeDtypeStruct((b, t, w), dt) for w, dt in outs],
        compiler_params=_cparams(),
        name="in_proj",
    )(xt, modl, modl, wcat, cos_t, sin_t, qw, kw, ones_q, ones_k)


def _conv_body(x_ref, prev_ref, next_ref, w_ref, b_ref, xbc_ref, u_ref, *, n_xbc):
    i = pl.program_id(1)
    n = pl.num_programs(1)
    x = x_ref[0]
    tl = x.shape[0]
    has_prev = i > 1
    has_next = (i > 0) & (i < n - 1)
    pm = jnp.where(has_prev, prev_ref[0, SUBLANES - 1:SUBLANES, :], 0.0)
    n0 = jnp.where(has_next, next_ref[0, 0:1, :], 0.0)
    n1 = jnp.where(has_next, next_ref[0, 1:2, :], 0.0)
    row = lax.broadcasted_iota(I32, x.shape, 0)
    xm1 = jnp.where(row == 0, pm, pltpu.roll(x, 1, 0))
    xp1 = jnp.where(row == tl - 1, n0, pltpu.roll(x, tl - 1, 0))
    xp2 = jnp.where(row == tl - 1, n1, jnp.where(row == tl - 2, n0, pltpu.roll(x, tl - 2, 0)))
    w = w_ref[...]
    y = xm1 * w[0:1] + x * w[1:2] + xp1 * w[2:3] + xp2 * w[3:4] + b_ref[...]
    xbc = y[:, :n_xbc]
    xbc_ref[0] = _silu(xbc)
    u_ref[0] = y[:, n_xbc:]


def _conv(xu, w, bias, n_xbc):
    b, t, c = xu.shape
    tl = SEQ_TILE
    nt = t // tl
    r = tl // SUBLANES
    nb = t // SUBLANES
    return pl.pallas_call(
        functools.partial(_conv_body, n_xbc=n_xbc),
        grid=(b, nt),
        in_specs=[
            pl.BlockSpec((1, tl, c), lambda bb, i: (bb, i, 0)),
            pl.BlockSpec((1, SUBLANES, c), lambda bb, i: (bb, jnp.maximum(i * r - 1, 0), 0)),
            pl.BlockSpec((1, SUBLANES, c), lambda bb, i: (bb, jnp.minimum((i + 1) * r, nb - 1), 0)),
            pl.BlockSpec((CONV_W, c), lambda bb, i: (0, 0)),
            pl.BlockSpec((1, c), lambda bb, i: (0, 0)),
        ],
        out_specs=[
            pl.BlockSpec((1, tl, n_xbc), lambda bb, i: (bb, i, 0)),
            pl.BlockSpec((1, tl, c - n_xbc), lambda bb, i: (bb, i, 0)),
        ],
        out_shape=[jax.ShapeDtypeStruct((b, t, n_xbc), F32), jax.ShapeDtypeStruct((b, t, c - n_xbc), F32)],
        compiler_params=_cparams(),
        name="dwconv",
    )(xu, xu, xu, w, bias)


def _attn_body(q_ref, kt_ref, v_ref, o_ref, *, chunks):
    q = q_ref[0]
    tq = q.shape[0]
    qs = jnp.concatenate([q[:, h * HEAD_DIM:(h + 1) * HEAD_DIM] for h in range(Q_PER_KV)], axis=0)
    rows = Q_PER_KV * tq
    m = jnp.full((rows, 1), -jnp.inf, F32)
    l = jnp.zeros((rows, 1), F32)
    acc = jnp.zeros((rows, HEAD_DIM), F32)
    for s0, sz in chunks:
        s = jnp.dot(qs, kt_ref[0, 0, :, s0:s0 + sz], preferred_element_type=F32)
        m_new = jnp.maximum(m, jnp.max(s, axis=1, keepdims=True))
        alpha = jnp.exp(m - m_new)
        p = jnp.exp(s - m_new)
        l = alpha * l + jnp.sum(p, axis=1, keepdims=True)
        acc = alpha * acc + jnp.dot(p.astype(BF16), v_ref[0, 0, s0:s0 + sz, :], preferred_element_type=F32)
        m = m_new
    o = acc / l
    o_ref[0] = jnp.concatenate([o[h * tq:(h + 1) * tq] for h in range(Q_PER_KV)], axis=1)


def _key_chunks(n_keys, size=512):
    rem = n_keys % size
    chunks = [(0, rem)] if rem else []
    chunks += [(rem + size * i, size) for i in range(n_keys // size)]
    return tuple(chunks)


def _attention(q, kt, v, n_q, q_off, n_keys, tq):
    b = q.shape[0]
    gw = Q_PER_KV * HEAD_DIM
    off = q_off // tq
    return pl.pallas_call(
        functools.partial(_attn_body, chunks=_key_chunks(n_keys)),
        grid=(b, KV_HEADS, n_q // tq),
        in_specs=[
            pl.BlockSpec((1, tq, gw), lambda bb, g, i: (bb, i + off, g)),
            pl.BlockSpec((1, 1, HEAD_DIM, n_keys), lambda bb, g, i: (bb, g, 0, 0)),
            pl.BlockSpec((1, 1, n_keys, HEAD_DIM), lambda bb, g, i: (bb, g, 0, 0)),
        ],
        out_specs=pl.BlockSpec((1, tq, gw), lambda bb, g, i: (bb, i, g)),
        out_shape=jax.ShapeDtypeStruct((b, n_q, KV_HEADS * gw), F32),
        compiler_params=_cparams(),
        name="gqa_attention",
    )(q, kt, v)


def _seq_order(d, c, nc):
    return jnp.where(c == 0, 0, jnp.where(d == 0, c, nc - c))


def _ssd_body(xbc_ref, dt_ref, bias_ref, a_ref, dvec_ref, y_ref, s_scr):
    d = pl.program_id(1)
    c = pl.program_id(2)

    @pl.when(c == 0)
    def _():
        s_scr[...] = jnp.zeros_like(s_scr)

    xbc = xbc_ref[0]
    q = xbc.shape[0]
    xw = SSD_HEADS * SSD_HEAD_DIM
    gw = 2 * SSD_STATE
    x = xbc[:, :xw]
    bm = xbc[:, xw:xw + gw]
    cm = xbc[:, xw + gw:xw + 2 * gw]
    dt = _softplus(dt_ref[0] + bias_ref[0])
    da = dt * a_ref[0]
    ii = lax.broadcasted_iota(I32, (q, q), 0)
    jj = lax.broadcasted_iota(I32, (q, q), 1)
    mask = ((ii - jj) * (1 - 2 * d)) >= 0
    tm = jnp.where(mask, 1.0, 0.0).astype(BF16)
    da_hi = da.astype(BF16)
    da_lo = (da - da_hi.astype(F32)).astype(BF16)
    cs = jnp.dot(tm, da_hi, preferred_element_type=F32) + jnp.dot(tm, da_lo, preferred_element_type=F32)
    tot = jnp.where(d == 0, cs[q - 1:q, :], cs[0:1, :])
    cst = cs.T
    dec = jnp.exp(tot - cs)
    ecs = jnp.exp(cs)
    etot = jnp.exp(tot)
    ys = []
    nt = (((1,), (1,)), ((), ()))
    tn = (((0,), (0,)), ((), ()))
    for g in range(2):
        bg = bm[:, g * SSD_STATE:(g + 1) * SSD_STATE].astype(BF16)
        cg = cm[:, g * SSD_STATE:(g + 1) * SSD_STATE].astype(BF16)
        gmat = lax.dot_general(cg, bg, nt, preferred_element_type=F32)
        for hh in range(SSD_HEADS // 2):
            h = 2 * g + hh
            lmat = jnp.exp(jnp.where(mask, cs[:, h:h + 1] - cst[h:h + 1, :], -jnp.inf))
            xh = x[:, h * SSD_HEAD_DIM:(h + 1) * SSD_HEAD_DIM]
            xdt = xh * dt[:, h:h + 1]
            y_diag = _bdot(gmat * lmat, xdt)
            s_in = s_scr[h]
            y_off = lax.dot_general(cg, s_in.astype(BF16), nt, preferred_element_type=F32) * ecs[:, h:h + 1]
            ys.append(y_diag + y_off)
            xd = (xdt * dec[:, h:h + 1]).astype(BF16)
            s_scr[h] = etot[:, h:h + 1] * s_in + lax.dot_general(xd, bg, tn, preferred_element_type=F32)
    y = jnp.concatenate(ys, axis=1)
    y_ref[0, 0] = y + x * (dvec_ref[...] * jnp.where(d == 0, 1.0, 0.0))


def _ssd(xbc, dt, bias, a_neg, dvec):
    b, t, cw = xbc.shape
    q = SEQ_TILE
    nc = t // q
    xw = SSD_HEADS * SSD_HEAD_DIM
    return pl.pallas_call(
        _ssd_body,
        grid=(b, 2, nc),
        in_specs=[
            pl.BlockSpec((1, q, cw), lambda bb, d, c: (bb, _seq_order(d, c, nc), 0)),
            pl.BlockSpec((1, q, LANES), lambda bb, d, c: (bb, _seq_order(d, c, nc), d)),
            pl.BlockSpec((1, 1, LANES), lambda bb, d, c: (d, 0, 0)),
            pl.BlockSpec((1, 1, LANES), lambda bb, d, c: (d, 0, 0)),
            pl.BlockSpec((1, xw), lambda bb, d, c: (0, 0)),
        ],
        out_specs=pl.BlockSpec((1, 1, q, xw), lambda bb, d, c: (d, bb, _seq_order(d, c, nc), 0)),
        out_shape=jax.ShapeDtypeStruct((2, b, t, xw), F32),
        scratch_shapes=[pltpu.VMEM((SSD_HEADS, SSD_HEAD_DIM, SSD_STATE), F32)],
        compiler_params=_cparams(),
        name="ssd_scan",
    )(xbc, dt, bias, a_neg, dvec)


def _rg_body(u_ref, w_ref, bias_ref, lam_ref, y_ref, a_s, v_s, o_s, h_s, *, pitch):
    d = pl.program_id(0)
    c = pl.program_id(1)
    nb, tl, width = u_ref.shape
    ng = width // LANES

    @pl.when(c == 0)
    def _():
        h_s[...] = jnp.zeros_like(h_s)

    sp = _softplus(-lam_ref[0])
    w = w_ref[0]
    bias = bias_ref[0]
    for b in range(nb):
        ub = u_ref[b]
        pre = jnp.dot(ub.astype(BF16), w, preferred_element_type=F32) + bias
        r = _sigmoid(pre[:, :width])
        ig = _sigmoid(pre[:, width:])
        a = jnp.exp((-RG_C) * r * sp)
        v = jnp.sqrt(1.0 - a * a) * ig * ub
        for j in range(ng):
            a_s[j, pl.ds(b * pitch, tl), :] = a[:, j * LANES:(j + 1) * LANES]
            v_s[j, pl.ds(b * pitch, tl), :] = v[:, j * LANES:(j + 1) * LANES]

    def step(t, hs):
        te = jnp.where(d == 0, t, tl - 1 - t)
        out = []
        for j in range(ng):
            at = a_s[j, pl.ds(te, nb, stride=pitch), :]
            vt = v_s[j, pl.ds(te, nb, stride=pitch), :]
            hj = at * hs[j] + vt
            o_s[j, pl.ds(te, nb, stride=pitch), :] = hj
            out.append(hj)
        return tuple(out)

    h0 = tuple(h_s[:, j * LANES:(j + 1) * LANES] for j in range(ng))
    hf = lax.fori_loop(0, tl, step, h0, unroll=8)
    for j in range(ng):
        h_s[:, j * LANES:(j + 1) * LANES] = hf[j]
    for b in range(nb):
        y_ref[0, b] = jnp.concatenate([o_s[j, pl.ds(b * pitch, tl), :] for j in range(ng)], axis=1)


def _rglru(u, wg, bias, lam):
    b, t, width = u.shape
    assert b == SUBLANES, "the recurrence keeps one sample per sublane"
    tl = SEQ_TILE
    nc = t // tl
    pitch = tl + SUBLANES
    ng = width // LANES
    slab = pltpu.VMEM((ng, b * pitch, LANES), F32)
    return pl.pallas_call(
        functools.partial(_rg_body, pitch=pitch),
        grid=(2, nc),
        in_specs=[
            pl.BlockSpec((b, tl, width), lambda d, c: (0, _seq_order(d, c, nc), 0)),
            pl.BlockSpec((1, width, 2 * width), lambda d, c: (d, 0, 0)),
            pl.BlockSpec((1, 1, 2 * width), lambda d, c: (d, 0, 0)),
            pl.BlockSpec((1, 1, width), lambda d, c: (d, 0, 0)),
        ],
        out_specs=pl.BlockSpec((1, b, tl, width), lambda d, c: (d, 0, _seq_order(d, c, nc), 0)),
        out_shape=jax.ShapeDtypeStruct((2, b, t, width), F32),
        scratch_shapes=[slab, slab, slab, pltpu.VMEM((b, width), F32)],
        compiler_params=_cparams(),
        name="rglru_scan",
    )(u, wg, bias, lam)


def _rms(x, w):
    return x * lax.rsqrt(jnp.mean(x * x, axis=-1, keepdims=True) + EPS) * w


def _layer_norm(t, w, b):
    mu = jnp.mean(t, axis=-1, keepdims=True)
    tc = t - mu
    var = jnp.mean(tc * tc, axis=-1, keepdims=True)
    return tc * lax.rsqrt(var + EPS) * w + b


def _gelu_tanh(x):
    return 0.5 * x * (1.0 + jnp.tanh(math.sqrt(2.0 / math.pi) * (x + 0.044715 * (x * x * x))))


def _pack_bf16_pair(lo, hi):
    lb = pltpu.bitcast(lo.astype(BF16).astype(F32), jnp.uint32)
    hb = pltpu.bitcast(hi.astype(BF16).astype(F32), jnp.uint32)
    return (lb >> 16) | (hb & jnp.uint32(0xFFFF0000))


def _merge_body(*refs, has_ctx):
    if has_ctx:
        actx_ref, refs = refs[0], refs[1:]
    (alat_ref, ys_ref, z_ref, hr_ref, g_ref, x_ref, g1_ref, sc2_ref, sh2_ref, aw_ref, sw_ref, rw_ref,
     wo_ref, lnw_ref, lnb_ref, wr_ref, x1_ref, hp_ref, aff_ref, afft_ref) = refs
    a = alat_ref[0]
    if has_ctx:
        a = jnp.where(pl.program_id(1) == 0, actx_ref[0], a)
    an = _rms(a, aw_ref[...])
    sn = _rms((ys_ref[0, 0] + ys_ref[1, 0]) * _silu(z_ref[0]), sw_ref[...])
    rn = _rms((hr_ref[0, 0] + hr_ref[1, 0]) * _gelu_tanh(g_ref[0]), rw_ref[...])
    cat = jnp.concatenate([an, sn, rn], axis=1).astype(BF16)
    proj = jnp.dot(cat, wo_ref[...], preferred_element_type=F32)
    x1 = _layer_norm(DEEPNORM_ALPHA * x_ref[0] + g1_ref[0] * proj, lnw_ref[...], lnb_ref[...])
    x1_ref[0] = x1
    h2 = x1 * (1.0 + sc2_ref[0]) + sh2_ref[0]
    half = h2.shape[1] // 2
    hp_ref[0] = _pack_bf16_pair(h2[:, :half], h2[:, half:])
    logits = jnp.dot(h2.astype(BF16), wr_ref[...], preferred_element_type=F32)
    lane = lax.broadcasted_iota(I32, logits.shape, 1)
    logits = jnp.where(lane < N_EXPERTS, logits, -jnp.inf)
    e = jnp.exp(logits - jnp.max(logits, axis=-1, keepdims=True))
    aff = e / jnp.sum(e, axis=-1, keepdims=True)
    aff_ref[0] = aff
    afft_ref[0] = aff.T[:N_EXPERTS, :]


def _merge(attn_ctx, attn_lat, ys, z, hr, g, xt, modl, aw, sw, rw, wo, lnw, lnb, wr, ctx_row, off):
    b, t, d = xt.shape
    tl = SEQ_TILE
    nt = t // tl - off
    has_ctx = off == 0
    aw_ = attn_lat.shape[-1]
    seq = lambda w: pl.BlockSpec((1, tl, w), lambda bb, i: (bb, i + off, 0))
    pair = lambda w: pl.BlockSpec((2, 1, tl, w), lambda bb, i: (0, bb, i + off, 0))
    full = lambda shape: pl.BlockSpec(shape, lambda bb, i: (0,) * len(shape))
    out = lambda w: pl.BlockSpec((1, tl, w), lambda bb, i: (bb, i, 0))
    lat_off = 1 - off
    in_specs = [
        pl.BlockSpec((1, tl, aw_), lambda bb, i: (bb, jnp.maximum(i - lat_off, 0), 0)),
        pair(ys.shape[-1]), seq(z.shape[-1]), pair(hr.shape[-1]), seq(g.shape[-1]), seq(d),
        _mod_spec(2, d, ctx_row, off), _mod_spec(4, d, ctx_row, off), _mod_spec(3, d, ctx_row, off),
        full(aw.shape), full(sw.shape), full(rw.shape), full(wo.shape), full(lnw.shape), full(lnb.shape),
        full(wr.shape),
    ]
    args = [attn_lat, ys, z, hr, g, xt, modl, modl, modl, aw, sw, rw, wo, lnw, lnb, wr]
    if has_ctx:
        in_specs = [pl.BlockSpec((1, tl, aw_), lambda bb, i: (bb, 0, 0))] + in_specs
        args = [attn_ctx] + args
    rows = nt * tl
    return pl.pallas_call(
        functools.partial(_merge_body, has_ctx=has_ctx),
        grid=(b, nt),
        in_specs=in_specs,
        out_specs=[out(d), out(d // 2), out(LANES), pl.BlockSpec((1, N_EXPERTS, tl), lambda bb, i: (bb, 0, i))],
        out_shape=[
            jax.ShapeDtypeStruct((b, rows, d), F32),
            jax.ShapeDtypeStruct((b, rows, d // 2), jnp.uint32),
            jax.ShapeDtypeStruct((b, rows, LANES), F32),
            jax.ShapeDtypeStruct((b, N_EXPERTS, rows), F32),
        ],
        compiler_params=_cparams(),
        name="merge_outproj_ln1_router",
    )(*args)


def _topk_body(aff_ref, idx_ref, *, cap):
    aff = aff_ref[0, 0]
    n = aff.shape[1]
    bits = pltpu.bitcast(aff, I32)

    def search(i, thr):
        cand = thr | lax.shift_left(jnp.int32(1), 30 - i)
        cnt = jnp.sum(jnp.where(bits >= cand, 1.0, 0.0), axis=1, keepdims=True)
        return jnp.where(cnt >= cap, cand, thr)

    thr = lax.fori_loop(0, 31, search, jnp.zeros((1, 1), I32))
    gt = jnp.where(bits > thr, 1.0, 0.0)
    eq = jnp.where(bits == thr, 1.0, 0.0)
    need = cap - jnp.sum(gt, axis=1, keepdims=True)
    nblk = n // LANES
    r_ = lax.broadcasted_iota(I32, (LANES, LANES), 0)
    c_ = lax.broadcasted_iota(I32, (LANES, LANES), 1)
    upper = jnp.where(r_ <= c_, 1.0, 0.0).astype(BF16)

    def prefix(blocks):
        outs = []
        off = jnp.zeros((1, 1), F32)
        for mk in blocks:
            w = jnp.dot(mk.astype(BF16), upper, preferred_element_type=F32) + off
            outs.append(w)
            off = w[:, LANES - 1:LANES]
        return outs

    blk = lambda a, k: a[:, k * LANES:(k + 1) * LANES]
    tie_rank = prefix([blk(eq, k) for k in range(nblk)])
    cum = prefix([jnp.maximum(blk(gt, k), blk(eq, k) * jnp.where(tie_rank[k] <= need, 1.0, 0.0))
                  for k in range(nblk)])
    rows = -(-cap // LANES) * LANES
    slot = lax.broadcasted_iota(I32, (rows, LANES), 0).astype(F32)
    acc = jnp.zeros((rows, LANES), F32)
    for k in range(nblk):
        acc = acc + jnp.where(cum[k] <= slot, 1.0, 0.0)
    idx_ref[0, 0] = jnp.sum(acc.T, axis=0, keepdims=True)[:, :cap].astype(I32)


def _topk(aff_t, cap):
    b, e, n = aff_t.shape
    out = pl.pallas_call(
        functools.partial(_topk_body, cap=cap),
        grid=(b, e),
        in_specs=[pl.BlockSpec((1, 1, 1, n), lambda bb, ee: (bb, ee, 0, 0))],
        out_specs=pl.BlockSpec((1, 1, 1, cap), lambda bb, ee: (bb, ee, 0, 0)),
        out_shape=jax.ShapeDtypeStruct((b, e, 1, cap), I32),
        compiler_params=_cparams(),
        name="expert_choice_topk",
    )(aff_t.reshape(b, e, 1, n))
    return out.reshape(b, e, cap)


def _moe_body(idx_ref, hp_ref, aff_ref, wg_ref, wu_ref, wd_ref, out_ref, xs_scr, ag_scr, xlo_scr, xhi_scr, y_scr,
              *, cap):
    e = pl.program_id(1)
    f = pl.program_id(2)
    nf = pl.num_programs(2)

    @pl.when((e == 0) & (f == 0))
    def _():
        out_ref[...] = jnp.zeros_like(out_ref)

    @pl.when(f == 0)
    def _():
        def gather(s, carry):
            t = idx_ref[0, 0, 0, s]
            xs_scr[pl.ds(s, 1), :] = hp_ref[0, pl.ds(t, 1), :]
            ag_scr[pl.ds(s, 1), :] = aff_ref[0, pl.ds(t, 1), :]
            return carry

        lax.fori_loop(0, cap, gather, 0, unroll=8)
        w = xs_scr[...]
        xlo_scr[...] = pltpu.bitcast(w << 16, F32).astype(BF16)
        xhi_scr[...] = pltpu.bitcast(w & jnp.uint32(0xFFFF0000), F32).astype(BF16)

    half = xlo_scr.shape[1]
    xlo = xlo_scr[...]
    xhi = xhi_scr[...]
    hg = (jnp.dot(xlo, wg_ref[0, :half, :], preferred_element_type=F32)
          + jnp.dot(xhi, wg_ref[0, half:, :], preferred_element_type=F32))
    hu = (jnp.dot(xlo, wu_ref[0, :half, :], preferred_element_type=F32)
          + jnp.dot(xhi, wu_ref[0, half:, :], preferred_element_type=F32))
    hid = (_silu(hg) * hu).astype(BF16)
    yp = jnp.dot(hid, wd_ref[0], preferred_element_type=F32)

    @pl.when(f == 0)
    def _():
        y_scr[...] = yp

    @pl.when((f > 0) & (f < nf - 1))
    def _():
        y_scr[...] += yp

    @pl.when(f == nf - 1)
    def _():
        lane = lax.broadcasted_iota(I32, ag_scr.shape, 1)
        gate = jnp.sum(jnp.where(lane == e, ag_scr[...], 0.0), axis=1, keepdims=True)
        y_scr[...] = (y_scr[...] + yp) * gate

        def scatter(s, carry):
            t = idx_ref[0, 0, 0, s]
            out_ref[0, pl.ds(t, 1), :] = out_ref[0, pl.ds(t, 1), :] + y_scr[pl.ds(s, 1), :]
            return carry

        lax.fori_loop(0, cap, scatter, 0, unroll=8)


def _moe(idx, hp, aff, wg, wu, wd, tf):
    b, t, half = hp.shape
    d = 2 * half
    ne, _, ff = wg.shape
    cap = idx.shape[-1]
    nf = ff // tf
    assert nf >= 2
    one = pl.Buffered(1)
    return pl.pallas_call(
        functools.partial(_moe_body, cap=cap),
        grid=(b, ne, nf),
        in_specs=[
            pl.BlockSpec((1, 1, 1, cap), lambda bb, e, f: (bb, e, 0, 0), memory_space=pltpu.SMEM),
            pl.BlockSpec((1, t, half), lambda bb, e, f: (bb, 0, 0), pipeline_mode=one),
            pl.BlockSpec((1, t, LANES), lambda bb, e, f: (bb, 0, 0), pipeline_mode=one),
            pl.BlockSpec((1, d, tf), lambda bb, e, f: (e, 0, f)),
            pl.BlockSpec((1, d, tf), lambda bb, e, f: (e, 0, f)),
            pl.BlockSpec((1, tf, d), lambda bb, e, f: (e, f, 0)),
        ],
        out_specs=pl.BlockSpec((1, t, d), lambda bb, e, f: (bb, 0, 0), pipeline_mode=one),
        out_shape=jax.ShapeDtypeStruct((b, t, d), F32),
        scratch_shapes=[
            pltpu.VMEM((cap, half), jnp.uint32),
            pltpu.VMEM((cap, LANES), F32),
            pltpu.VMEM((cap, half), BF16),
            pltpu.VMEM((cap, half), BF16),
            pltpu.VMEM((cap, d), F32),
        ],
        compiler_params=_cparams(VMEM_LIMIT_MOE),
        name="expert_ffn",
    )(idx.reshape(b, ne, 1, cap), hp, aff, wg, wu, wd)


def _ln2_body(x1_ref, moe_ref, g2_ref, w_ref, b_ref, o_ref):
    o_ref[0] = _layer_norm(DEEPNORM_ALPHA * x1_ref[0] + g2_ref[0] * moe_ref[0], w_ref[...], b_ref[...])


def _ln2(x1, moe, modl, w, bias, ctx_row, off):
    b, t, d = x1.shape
    tl = SEQ_TILE
    seq = pl.BlockSpec((1, tl, d), lambda bb, i: (bb, i, 0))
    full = pl.BlockSpec((1, d), lambda bb, i: (0, 0))
    return pl.pallas_call(
        _ln2_body,
        grid=(b, t // tl),
        in_specs=[seq, seq, _mod_spec(5, d, ctx_row, off), full, full],
        out_specs=seq,
        out_shape=jax.ShapeDtypeStruct((b, t, d), F32),
        compiler_params=_cparams(),
        name="ln2",
    )(x1, moe, modl, w, bias)


def _block_diag_ones(width, block):
    r = jnp.arange(width)[:, None] // block
    c = jnp.arange(width)[None, :] // block
    return (r == c).astype(BF16)


def _block_diag(w):
    k, d, e = w.shape
    eye = jnp.eye(k, dtype=w.dtype)
    return (eye[:, None, :, None] * w[:, :, None, :]).reshape(k * d, k * e)


def _rope_tables(n_ctx, n_lat):
    pos = jnp.arange(n_lat)
    row = (pos // GRID_W).astype(F32)
    col = (pos % GRID_W).astype(F32)
    inv_freq = ROPE_THETA ** (-jnp.arange(0, AXIS_DIM, 2, dtype=F32) / AXIS_DIM)
    ang_r = row[:, None] * inv_freq
    ang_c = col[:, None] * inv_freq
    cos_h = jnp.concatenate([jnp.cos(ang_r)] * 2 + [jnp.cos(ang_c)] * 2, axis=1)
    sin_h = jnp.concatenate([-jnp.sin(ang_r), jnp.sin(ang_r), -jnp.sin(ang_c), jnp.sin(ang_c)], axis=1)
    reps = LANES // HEAD_DIM
    cos_t = jnp.concatenate([jnp.ones((n_ctx, LANES), F32), jnp.tile(cos_h, (1, reps))], axis=0)
    sin_t = jnp.concatenate([jnp.zeros((n_ctx, LANES), F32), jnp.tile(sin_h, (1, reps))], axis=0)
    return cos_t, sin_t


def _layer(xt, modl, rope, p, last, n_ctx, ctx_row):
    b, t, d = xt.shape
    n_lat = t - n_ctx
    cos_t, sin_t = rope
    w = p["w_in"]
    pad = jnp.zeros((d, LANES - SSD_HEADS), F32)
    wcat = jnp.concatenate(
        [w[:, 0:768], w[:, 768:1280], w[:, 1544:1800], w[:, 1280:1536], w[:, 1800:2056],
         w[:, 1536:1540], pad, w[:, 1540:1544], pad], axis=1).astype(BF16)
    qw = jnp.tile(p["q_norm"], W_Q // HEAD_DIM)[None]
    kw = jnp.tile(p["k_norm"], W_K // HEAD_DIM)[None]
    q, k, v, xu, z, g, dt = _inproj(xt, modl, wcat, cos_t, sin_t, qw, kw,
                                    _block_diag_ones(W_Q, HEAD_DIM), _block_diag_ones(W_K, HEAD_DIM), ctx_row)

    kt = k.reshape(b, t, KV_HEADS, HEAD_DIM).transpose(0, 2, 3, 1)
    vh = v.reshape(b, t, KV_HEADS, HEAD_DIM).transpose(0, 2, 1, 3)
    attn_lat = _attention(q, kt, vh, n_lat, n_ctx, t, 128)
    attn_ctx = None if last else _attention(q, kt, vh, n_ctx, 0, n_ctx, n_ctx)

    conv_w = jnp.concatenate([p["ssd_conv_w"], p["rg_conv_w"]], axis=1)
    conv_b = jnp.concatenate([p["ssd_conv_b"], p["rg_conv_b"]])[None]
    xbc, u = _conv(xu, conv_w, conv_b, p["ssd_conv_w"].shape[1])
    lane_pad = lambda a: jnp.pad(a, ((0, 0), (0, LANES - a.shape[1])))[:, None, :]
    ys = _ssd(xbc, dt, lane_pad(p["ssd_dt_bias"]), lane_pad(-jnp.exp(p["ssd_a_log"])),
              jnp.repeat(p["ssd_d"], SSD_HEAD_DIM)[None])
    wgate = jnp.stack([jnp.concatenate([_block_diag(p["rg_wa"][j]), _block_diag(p["rg_wx"][j])], axis=1)
                       for j in range(2)]).astype(BF16)
    bgate = jnp.concatenate([p["rg_ba"], p["rg_bx"]], axis=1)[:, None, :]
    hr = _rglru(u, wgate, bgate, p["rg_lambda"][:, None, :])

    off = 1 if last else 0
    wr = jnp.pad(p["w_router"], ((0, 0), (0, LANES - N_EXPERTS))).astype(BF16)
    x1, hp, aff, aff_t = _merge(attn_ctx, attn_lat, ys, z, hr, g, xt, modl,
                                p["attn_out_norm"][None], p["ssd_norm"][None], p["rg_out_norm"][None],
                                p["w_out"].astype(BF16), p["ln1_w"][None], p["ln1_b"][None], wr, ctx_row, off)

    if last:
        idx = _topk(aff_t, EC_CAPACITY * n_lat // N_EXPERTS)
    else:
        idx_lat = _topk(aff_t[:, :, n_ctx:], EC_CAPACITY * n_lat // N_EXPERTS) + n_ctx
        idx_ctx = _topk(aff_t[:, :, :n_ctx], EC_CAPACITY * n_ctx // N_EXPERTS)
        idx = jnp.concatenate([idx_lat, idx_ctx], axis=-1)
    moe = _moe(idx, hp, aff, p["w_gate"].astype(BF16), p["w_up"].astype(BF16), p["w_down"].astype(BF16), 1024)
    return _ln2(x1, moe, modl, p["ln2_w"][None], p["ln2_b"][None], ctx_row, off)


def kernel(x, c, ctx, c_ctx, w_mod, b_mod, w_in, q_norm, k_norm, attn_out_norm, ssd_conv_w, ssd_conv_b, ssd_dt_bias, ssd_a_log, ssd_d, ssd_norm, rg_conv_w, rg_conv_b, rg_wa, rg_ba, rg_wx, rg_bx, rg_lambda, rg_out_norm, w_out, ln1_w, ln1_b, w_router, w_gate, w_up, w_down, ln2_w, ln2_b):
    b, n_lat, d = x.shape
    n_ctx = ctx.shape[1]
    assert n_ctx == SEQ_TILE and n_lat % 512 == 0 and b == SUBLANES
    params = dict(w_in=w_in, q_norm=q_norm, k_norm=k_norm, attn_out_norm=attn_out_norm, ssd_conv_w=ssd_conv_w,
                  ssd_conv_b=ssd_conv_b, ssd_dt_bias=ssd_dt_bias, ssd_a_log=ssd_a_log, ssd_d=ssd_d,
                  ssd_norm=ssd_norm, rg_conv_w=rg_conv_w, rg_conv_b=rg_conv_b, rg_wa=rg_wa, rg_ba=rg_ba,
                  rg_wx=rg_wx, rg_bx=rg_bx, rg_lambda=rg_lambda, rg_out_norm=rg_out_norm, w_out=w_out,
                  ln1_w=ln1_w, ln1_b=ln1_b, w_router=w_router, w_gate=w_gate, w_up=w_up, w_down=w_down,
                  ln2_w=ln2_w, ln2_b=ln2_b)
    mod = _modulation(c, c_ctx, w_mod, b_mod)
    rope = _rope_tables(n_ctx, n_lat)
    xt = jnp.concatenate([ctx, x], axis=1)
    depth = w_mod.shape[0]
    for l in range(depth):
        p = {name: val[l] for name, val in params.items()}
        modl = mod[l].reshape(mod.shape[1], 1, mod.shape[2])
        xt = _layer(xt, modl, rope, p, l == depth - 1, n_ctx, b)
    return xt
```

```python
import functools
import math

import jax
import jax.numpy as jnp
from jax import lax
from jax.experimental import pallas as pl
from jax.experimental.pallas import tpu as pltpu

F32 = jnp.float32
BF16 = jnp.bfloat16
I32 = jnp.int32

DEPTH = 2
GRID_W = 64
HEAD_DIM = 64
KV_HEADS = 2
Q_PER_KV = 4
AXIS_DIM = HEAD_DIM // 2
ROPE_THETA = 10000.0
SSD_HEADS = 4
SSD_HEAD_DIM = 64
SSD_STATE = 64
RG_BLOCKS = 4
RG_C = 8.0
CONV_W = 4
N_EXPERTS = 16
EC_CAPACITY = 2
EPS = 1e-6
DEEPNORM_ALPHA = (2 * DEPTH) ** 0.25

LANES = 128
SUBLANES = 8
SEQ_TILE = 256
ATTN_Q_TILE = 128
ATTN_KEY_CHUNK = 512
LOG2E = math.log2(math.e)
VMEM_LIMIT = 48 * 1024 * 1024
VMEM_LIMIT_MOE = 60 * 1024 * 1024


def _cparams(limit=VMEM_LIMIT):
    return pltpu.CompilerParams(vmem_limit_bytes=limit)


def _sigmoid(x):
    return 1.0 / (1.0 + jnp.exp(-x))


def _silu(x):
    return x * _sigmoid(x)


def _softplus(x):
    return jnp.maximum(x, 0.0) + jnp.log1p(jnp.exp(-jnp.abs(x)))


def _bdot(a, b):
    return jnp.dot(a.astype(BF16), b.astype(BF16), preferred_element_type=F32)


def _mod_body(c_ref, w_ref, b_ref, o_ref):
    c = c_ref[...]
    o_ref[0] = _bdot(_silu(c), w_ref[0]) + b_ref[0]


def _modulation(c, c_ctx, w_mod, b_mod):
    depth, d, n6 = w_mod.shape
    b = c.shape[0]
    rows = 2 * SUBLANES
    cc = jnp.zeros((rows, d), F32).at[:b].set(c).at[b].set(c_ctx)
    tn = 1536
    return pl.pallas_call(
        _mod_body,
        grid=(depth, n6 // tn),
        in_specs=[
            pl.BlockSpec((rows, d), lambda l, j: (0, 0)),
            pl.BlockSpec((1, d, tn), lambda l, j: (l, 0, j)),
            pl.BlockSpec((1, 1, tn), lambda l, j: (l, 0, j)),
        ],
        out_specs=pl.BlockSpec((1, rows, tn), lambda l, j: (l, 0, j)),
        out_shape=jax.ShapeDtypeStruct((depth, rows, n6), F32),
        compiler_params=_cparams(),
        name="adaln_mod",
    )(cc, w_mod, b_mod.reshape(depth, 1, n6))


def _mod_spec(j, d, ctx_row, off):
    if off == 0:
        return pl.BlockSpec((1, 1, d), lambda b, i: (jnp.where(i == 0, ctx_row, b), 0, j))
    return pl.BlockSpec((1, 1, d), lambda b, i: (b, 0, j))


W_Q, W_K, W_V, W_XU, W_Z, W_G, W_DT = 512, 128, 128, 768, 256, 256, 256
IN_PAD = W_Q + W_K + W_V + W_XU + W_Z + W_G + W_DT


def _rope(t, cos, sin_signed):
    rows = t.shape[0]
    lane = lax.broadcasted_iota(I32, (rows, LANES), 1)
    first = (lane % AXIS_DIM) < (AXIS_DIM // 2)
    outs = []
    for c in range(t.shape[1] // LANES):
        tc = t[:, c * LANES:(c + 1) * LANES]
        partner = jnp.where(first, pltpu.roll(tc, LANES - AXIS_DIM // 2, 1), pltpu.roll(tc, AXIS_DIM // 2, 1))
        outs.append(tc * cos + partner * sin_signed)
    return outs[0] if len(outs) == 1 else jnp.concatenate(outs, axis=1)


def _inproj_body(x_ref, sc_ref, sh_ref, w_ref, cos_ref, sin_ref, qw_ref, kw_ref, oq_ref, ok_ref,
                 q_ref, k_ref, v_ref, xu_ref, z_ref, g_ref, dt_ref):
    h = x_ref[0] * (1.0 + sc_ref[0]) + sh_ref[0]
    y = jnp.dot(h.astype(BF16), w_ref[...], preferred_element_type=F32)
    o = 0
    q = y[:, o:o + W_Q]; o += W_Q
    k = y[:, o:o + W_K]; o += W_K
    v = y[:, o:o + W_V]; o += W_V
    xu_ref[0] = y[:, o:o + W_XU]; o += W_XU
    z_ref[0] = y[:, o:o + W_Z]; o += W_Z
    g_ref[0] = y[:, o:o + W_G]; o += W_G
    dt_ref[0] = y[:, o:o + W_DT]
    cos = cos_ref[...]
    sin = sin_ref[...]
    ssq = jnp.dot((q * q).astype(BF16), oq_ref[...], preferred_element_type=F32)
    qn = q * lax.rsqrt(ssq * (1.0 / HEAD_DIM) + EPS) * qw_ref[...]
    q_ref[0] = (_rope(qn, cos, sin) * (HEAD_DIM ** -0.5 * LOG2E)).T.astype(BF16)
    ssk = jnp.dot((k * k).astype(BF16), ok_ref[...], preferred_element_type=F32)
    kn = k * lax.rsqrt(ssk * (1.0 / HEAD_DIM) + EPS) * kw_ref[...]
    k_ref[0] = _rope(kn, cos, sin).astype(BF16)
    v_ref[0] = v.astype(BF16)


def _inproj(xt, modl, wcat, cos_t, sin_t, qw, kw, ones_q, ones_k, ctx_row):
    b, t, d = xt.shape
    tl = SEQ_TILE
    grid = (b, t // tl)
    full = lambda shape: pl.BlockSpec(shape, lambda bb, i: (0,) * len(shape))
    seq = lambda w: pl.BlockSpec((1, tl, w), lambda bb, i: (bb, i, 0))
    outs = [(W_K, BF16), (W_V, BF16), (W_XU, F32), (W_Z, F32), (W_G, F32), (W_DT, F32)]
    return pl.pallas_call(
        _inproj_body,
        grid=grid,
        in_specs=[
            seq(d),
            _mod_spec(1, d, ctx_row, 0),
            _mod_spec(0, d, ctx_row, 0),
            full((d, IN_PAD)),
            pl.BlockSpec((tl, LANES), lambda bb, i: (i, 0)),
            pl.BlockSpec((tl, LANES), lambda bb, i: (i, 0)),
            full((1, W_Q)), full((1, W_K)), full((W_Q, W_Q)), full((W_K, W_K)),
        ],
        out_specs=[pl.BlockSpec((1, W_Q, tl), lambda bb, i: (bb, 0, i))] + [seq(w) for w, _ in outs],
        out_shape=[jax.ShapeDtypeStruct((b, W_Q, t), BF16)]
        + [jax.ShapeDtypeStruct((b, t, w), dt) for w, dt in outs],
        compiler_params=_cparams(),
        name="in_proj",
    )(xt, modl, modl, wcat, cos_t, sin_t, qw, kw, ones_q, ones_k)


def _conv_body(x_ref, prev_ref, next_ref, w_ref, b_ref, xbc_ref, u_ref, *, n_xbc):
    i = pl.program_id(1)
    n = pl.num_programs(1)
    x = x_ref[0]
    tl = x.shape[0]
    has_prev = i > 1
    has_next = (i > 0) & (i < n - 1)
    pm = jnp.where(has_prev, prev_ref[0, SUBLANES - 1:SUBLANES, :], 0.0)
    n0 = jnp.where(has_next, next_ref[0, 0:1, :], 0.0)
    n1 = jnp.where(has_next, next_ref[0, 1:2, :], 0.0)
    row = lax.broadcasted_iota(I32, x.shape, 0)
    xm1 = jnp.where(row == 0, pm, pltpu.roll(x, 1, 0))
    xp1 = jnp.where(row == tl - 1, n0, pltpu.roll(x, tl - 1, 0))
    xp2 = jnp.where(row == tl - 1, n1, jnp.where(row == tl - 2, n0, pltpu.roll(x, tl - 2, 0)))
    w = w_ref[...]
    y = xm1 * w[0:1] + x * w[1:2] + xp1 * w[2:3] + xp2 * w[3:4] + b_ref[...]
    xbc = y[:, :n_xbc]
    xbc_ref[0] = _silu(xbc)
    u_ref[0] = y[:, n_xbc:]


def _conv(xu, w, bias, n_xbc):
    b, t, c = xu.shape
    tl = SEQ_TILE
    nt = t // tl
    r = tl // SUBLANES
    nb = t // SUBLANES
    return pl.pallas_call(
        functools.partial(_conv_body, n_xbc=n_xbc),
        grid=(b, nt),
        in_specs=[
            pl.BlockSpec((1, tl, c), lambda bb, i: (bb, i, 0)),
            pl.BlockSpec((1, SUBLANES, c), lambda bb, i: (bb, jnp.maximum(i * r - 1, 0), 0)),
            pl.BlockSpec((1, SUBLANES, c), lambda bb, i: (bb, jnp.minimum((i + 1) * r, nb - 1), 0)),
            pl.BlockSpec((CONV_W, c), lambda bb, i: (0, 0)),
            pl.BlockSpec((1, c), lambda bb, i: (0, 0)),
        ],
        out_specs=[
            pl.BlockSpec((1, tl, n_xbc), lambda bb, i: (bb, i, 0)),
            pl.BlockSpec((1, tl, c - n_xbc), lambda bb, i: (bb, i, 0)),
        ],
        out_shape=[jax.ShapeDtypeStruct((b, t, n_xbc), F32), jax.ShapeDtypeStruct((b, t, c - n_xbc), F32)],
        compiler_params=_cparams(),
        name="dwconv",
    )(xu, xu, xu, w, bias)


def _attn_body(qt_ref, k_ref, vt_ref, o_ref, *, n_keys, kc):
    nh, hd, tq = qt_ref.shape[1:]
    ngrp = 2
    hpg = nh // ngrp
    qts = [jnp.concatenate([qt_ref[0, g * hpg + h] for h in range(hpg)], axis=1) for g in range(ngrp)]
    cols = hpg * tq
    m = [jnp.full((1, cols), -jnp.inf, F32) for _ in range(ngrp)]
    l = [jnp.zeros((1, cols), F32) for _ in range(ngrp)]
    acc = [jnp.zeros((hd, cols), F32) for _ in range(ngrp)]
    rem = n_keys % kc
    chunks = ([(0, rem)] if rem else []) + [(rem + kc * i, kc) for i in range(n_keys // kc)]
    tiles = [(s0, sz, g) for s0, sz in chunks for g in range(ngrp)]
    score = lambda s0, sz, g: jnp.dot(k_ref[0, 0, s0:s0 + sz, :], qts[g], preferred_element_type=F32)
    s_next = score(*tiles[0])
    for i, (s0, kc, g) in enumerate(tiles):
        s = s_next
        if i + 1 < len(tiles):
            s_next = score(*tiles[i + 1])
        m_new = jnp.maximum(m[g], jnp.max(s, axis=0, keepdims=True))
        alpha = jnp.exp2(m[g] - m_new)
        p = jnp.exp2(s - m_new)
        l[g] = alpha * l[g] + jnp.sum(p, axis=0, keepdims=True)
        acc[g] = alpha * acc[g] + jnp.dot(vt_ref[0, 0, :, s0:s0 + kc], p.astype(BF16),
                                          preferred_element_type=F32)
        m[g] = m_new
    for g in range(ngrp):
        o = acc[g] * (1.0 / l[g])
        for h in range(hpg):
            hh = g * hpg + h
            o_ref[0, hh * hd:(hh + 1) * hd, :] = o[:, h * tq:(h + 1) * tq]


def _attention(qt, k, vt, n_q, q_off, n_keys, tq):
    b = qt.shape[0]
    off = q_off // tq
    return pl.pallas_call(
        functools.partial(_attn_body, n_keys=n_keys, kc=ATTN_KEY_CHUNK),
        grid=(b, KV_HEADS, n_q // tq),
        in_specs=[
            pl.BlockSpec((1, Q_PER_KV, HEAD_DIM, tq), lambda bb, g, i: (bb, g, 0, i + off)),
            pl.BlockSpec((1, 1, n_keys, HEAD_DIM), lambda bb, g, i: (bb, g, 0, 0)),
            pl.BlockSpec((1, 1, HEAD_DIM, n_keys), lambda bb, g, i: (bb, g, 0, 0)),
        ],
        out_specs=pl.BlockSpec((1, Q_PER_KV * HEAD_DIM, tq), lambda bb, g, i: (bb, g, i)),
        out_shape=jax.ShapeDtypeStruct((b, KV_HEADS * Q_PER_KV * HEAD_DIM, n_q), F32),
        compiler_params=_cparams(),
        name="gqa_attention",
    )(qt, k, vt)


def _seq_order(d, c, nc):
    return jnp.where(c == 0, 0, jnp.where(d == 0, c, nc - c))


def _ssd_body(xbc_ref, dt_ref, bias_ref, a_ref, dvec_ref, y_ref, s_scr):
    d = pl.program_id(1)
    c = pl.program_id(2)

    @pl.when(c == 0)
    def _():
        s_scr[...] = jnp.zeros_like(s_scr)

    xbc = xbc_ref[0]
    q = xbc.shape[0]
    xw = SSD_HEADS * SSD_HEAD_DIM
    gw = 2 * SSD_STATE
    x = xbc[:, :xw]
    bm = xbc[:, xw:xw + gw]
    cm = xbc[:, xw + gw:xw + 2 * gw]
    dt = _softplus(dt_ref[0] + bias_ref[0])
    da = dt * a_ref[0]
    ii = lax.broadcasted_iota(I32, (q, q), 0)
    jj = lax.broadcasted_iota(I32, (q, q), 1)
    mask = ((ii - jj) * (1 - 2 * d)) >= 0
    tm = jnp.where(mask, 1.0, 0.0).astype(BF16)
    da_hi = da.astype(BF16)
    da_lo = (da - da_hi.astype(F32)).astype(BF16)
    cs = jnp.dot(tm, da_hi, preferred_element_type=F32) + jnp.dot(tm, da_lo, preferred_element_type=F32)
    tot = jnp.where(d == 0, cs[q - 1:q, :], cs[0:1, :])
    cst = cs.T
    dec = jnp.exp(tot - cs)
    ecs = jnp.exp(cs)
    etot = jnp.exp(tot)
    ys = []
    nt = (((1,), (1,)), ((), ()))
    tn = (((0,), (0,)), ((), ()))
    for g in range(2):
        bg = bm[:, g * SSD_STATE:(g + 1) * SSD_STATE].astype(BF16)
        cg = cm[:, g * SSD_STATE:(g + 1) * SSD_STATE].astype(BF16)
        gmat = lax.dot_general(cg, bg, nt, preferred_element_type=F32)
        for hh in range(SSD_HEADS // 2):
            h = 2 * g + hh
            lmat = jnp.exp(jnp.where(mask, cs[:, h:h + 1] - cst[h:h + 1, :], -jnp.inf))
            xh = x[:, h * SSD_HEAD_DIM:(h + 1) * SSD_HEAD_DIM]
            xdt = xh * dt[:, h:h + 1]
            y_diag = _bdot(gmat * lmat, xdt)
            s_in = s_scr[h]
            y_off = lax.dot_general(cg, s_in.astype(BF16), nt, preferred_element_type=F32) * ecs[:, h:h + 1]
            ys.append(y_diag + y_off)
            xd = (xdt * dec[:, h:h + 1]).astype(BF16)
            s_scr[h] = etot[:, h:h + 1] * s_in + lax.dot_general(xd, bg, tn, preferred_element_type=F32)
    y = jnp.concatenate(ys, axis=1)
    y_ref[0, 0] = y + x * (dvec_ref[...] * jnp.where(d == 0, 1.0, 0.0))


def _ssd(xbc, dt, bias, a_neg, dvec):
    b, t, cw = xbc.shape
    q = SEQ_TILE
    nc = t // q
    xw = SSD_HEADS * SSD_HEAD_DIM
    return pl.pallas_call(
        _ssd_body,
        grid=(b, 2, nc),
        in_specs=[
            pl.BlockSpec((1, q, cw), lambda bb, d, c: (bb, _seq_order(d, c, nc), 0)),
            pl.BlockSpec((1, q, LANES), lambda bb, d, c: (bb, _seq_order(d, c, nc), d)),
            pl.BlockSpec((1, 1, LANES), lambda bb, d, c: (d, 0, 0)),
            pl.BlockSpec((1, 1, LANES), lambda bb, d, c: (d, 0, 0)),
            pl.BlockSpec((1, xw), lambda bb, d, c: (0, 0)),
        ],
        out_specs=pl.BlockSpec((1, 1, q, xw), lambda bb, d, c: (d, bb, _seq_order(d, c, nc), 0)),
        out_shape=jax.ShapeDtypeStruct((2, b, t, xw), F32),
        scratch_shapes=[pltpu.VMEM((SSD_HEADS, SSD_HEAD_DIM, SSD_STATE), F32)],
        compiler_params=_cparams(),
        name="ssd_scan",
    )(xbc, dt, bias, a_neg, dvec)


def _rg_body(u_ref, w_ref, bias_ref, lam_ref, y_ref, a_s, v_s, o_s, h_s, *, pitch):
    d = pl.program_id(0)
    c = pl.program_id(1)
    nb, tl, width = u_ref.shape
    ng = width // LANES

    @pl.when(c == 0)
    def _():
        h_s[...] = jnp.zeros_like(h_s)

    sp = _softplus(-lam_ref[0])
    w = w_ref[0]
    bias = bias_ref[0]
    for b in range(nb):
        ub = u_ref[b]
        pre = jnp.dot(ub.astype(BF16), w, preferred_element_type=F32) + bias
        r = _sigmoid(pre[:, :width])
        ig = _sigmoid(pre[:, width:])
        a = jnp.exp((-RG_C) * r * sp)
        v = jnp.sqrt(1.0 - a * a) * ig * ub
        for j in range(ng):
            a_s[j, pl.ds(b * pitch, tl), :] = a[:, j * LANES:(j + 1) * LANES]
            v_s[j, pl.ds(b * pitch, tl), :] = v[:, j * LANES:(j + 1) * LANES]

    def step(t, hs):
        te = jnp.where(d == 0, t, tl - 1 - t)
        out = []
        for j in range(ng):
            at = a_s[j, pl.ds(te, nb, stride=pitch), :]
            vt = v_s[j, pl.ds(te, nb, stride=pitch), :]
            hj = at * hs[j] + vt
            o_s[j, pl.ds(te, nb, stride=pitch), :] = hj
            out.append(hj)
        return tuple(out)

    h0 = tuple(h_s[:, j * LANES:(j + 1) * LANES] for j in range(ng))
    hf = lax.fori_loop(0, tl, step, h0, unroll=8)
    for j in range(ng):
        h_s[:, j * LANES:(j + 1) * LANES] = hf[j]
    for b in range(nb):
        y_ref[0, b] = jnp.concatenate([o_s[j, pl.ds(b * pitch, tl), :] for j in range(ng)], axis=1)


def _rglru(u, wg, bias, lam):
    b, t, width = u.shape
    assert b == SUBLANES, "the recurrence keeps one sample per sublane"
    tl = SEQ_TILE
    nc = t // tl
    pitch = tl + SUBLANES
    ng = width // LANES
    slab = pltpu.VMEM((ng, b * pitch, LANES), F32)
    return pl.pallas_call(
        functools.partial(_rg_body, pitch=pitch),
        grid=(2, nc),
        in_specs=[
            pl.BlockSpec((b, tl, width), lambda d, c: (0, _seq_order(d, c, nc), 0)),
            pl.BlockSpec((1, width, 2 * width), lambda d, c: (d, 0, 0)),
            pl.BlockSpec((1, 1, 2 * width), lambda d, c: (d, 0, 0)),
            pl.BlockSpec((1, 1, width), lambda d, c: (d, 0, 0)),
        ],
        out_specs=pl.BlockSpec((1, b, tl, width), lambda d, c: (d, 0, _seq_order(d, c, nc), 0)),
        out_shape=jax.ShapeDtypeStruct((2, b, t, width), F32),
        scratch_shapes=[slab, slab, slab, pltpu.VMEM((b, width), F32)],
        compiler_params=_cparams(),
        name="rglru_scan",
    )(u, wg, bias, lam)


def _rms(x, w):
    return x * lax.rsqrt(jnp.mean(x * x, axis=-1, keepdims=True) + EPS) * w


def _layer_norm(t, w, b):
    mu = jnp.mean(t, axis=-1, keepdims=True)
    tc = t - mu
    var = jnp.mean(tc * tc, axis=-1, keepdims=True)
    return tc * lax.rsqrt(var + EPS) * w + b


def _gelu_tanh(x):
    return 0.5 * x * (1.0 + jnp.tanh(math.sqrt(2.0 / math.pi) * (x + 0.044715 * (x * x * x))))


def _pack_bf16_pair(lo, hi):
    lb = pltpu.bitcast(lo.astype(BF16).astype(F32), jnp.uint32)
    hb = pltpu.bitcast(hi.astype(BF16).astype(F32), jnp.uint32)
    return (lb >> 16) | (hb & jnp.uint32(0xFFFF0000))


def _merge_body(*refs, has_ctx):
    if has_ctx:
        actx_ref, refs = refs[0], refs[1:]
    (alat_ref, ys_ref, z_ref, hr_ref, g_ref, x_ref, g1_ref, sc2_ref, sh2_ref, aw_ref, sw_ref, rw_ref,
     wo_ref, lnw_ref, lnb_ref, wr_ref, x1_ref, hp_ref, aff_ref, afft_ref) = refs
    a = alat_ref[0]
    if has_ctx:
        a = jnp.where(pl.program_id(1) == 0, actx_ref[0], a)
    an = _rms(a.T, aw_ref[...])
    sn = _rms((ys_ref[0, 0] + ys_ref[1, 0]) * _silu(z_ref[0]), sw_ref[...])
    rn = _rms((hr_ref[0, 0] + hr_ref[1, 0]) * _gelu_tanh(g_ref[0]), rw_ref[...])
    cat = jnp.concatenate([an, sn, rn], axis=1).astype(BF16)
    proj = jnp.dot(cat, wo_ref[...], preferred_element_type=F32)
    x1 = _layer_norm(DEEPNORM_ALPHA * x_ref[0] + g1_ref[0] * proj, lnw_ref[...], lnb_ref[...])
    x1_ref[0] = x1
    h2 = x1 * (1.0 + sc2_ref[0]) + sh2_ref[0]
    half = h2.shape[1] // 2
    hp_ref[0] = _pack_bf16_pair(h2[:, :half], h2[:, half:])
    logits = jnp.dot(h2.astype(BF16), wr_ref[...], preferred_element_type=F32)
    lane = lax.broadcasted_iota(I32, logits.shape, 1)
    logits = jnp.where(lane < N_EXPERTS, logits, -jnp.inf)
    e = jnp.exp(logits - jnp.max(logits, axis=-1, keepdims=True))
    aff = e / jnp.sum(e, axis=-1, keepdims=True)
    aff_ref[0] = aff
    afft_ref[0] = aff.T[:N_EXPERTS, :]


def _merge(attn_ctx, attn_lat, ys, z, hr, g, xt, modl, aw, sw, rw, wo, lnw, lnb, wr, ctx_row, off):
    b, t, d = xt.shape
    tl = SEQ_TILE
    nt = t // tl - off
    has_ctx = off == 0
    aw_ = attn_lat.shape[1]
    seq = lambda w: pl.BlockSpec((1, tl, w), lambda bb, i: (bb, i + off, 0))
    pair = lambda w: pl.BlockSpec((2, 1, tl, w), lambda bb, i: (0, bb, i + off, 0))
    full = lambda shape: pl.BlockSpec(shape, lambda bb, i: (0,) * len(shape))
    out = lambda w: pl.BlockSpec((1, tl, w), lambda bb, i: (bb, i, 0))
    lat_off = 1 - off
    in_specs = [
        pl.BlockSpec((1, aw_, tl), lambda bb, i: (bb, 0, jnp.maximum(i - lat_off, 0))),
        pair(ys.shape[-1]), seq(z.shape[-1]), pair(hr.shape[-1]), seq(g.shape[-1]), seq(d),
        _mod_spec(2, d, ctx_row, off), _mod_spec(4, d, ctx_row, off), _mod_spec(3, d, ctx_row, off),
        full(aw.shape), full(sw.shape), full(rw.shape), full(wo.shape), full(lnw.shape), full(lnb.shape),
        full(wr.shape),
    ]
    args = [attn_lat, ys, z, hr, g, xt, modl, modl, modl, aw, sw, rw, wo, lnw, lnb, wr]
    if has_ctx:
        in_specs = [pl.BlockSpec((1, aw_, tl), lambda bb, i: (bb, 0, 0))] + in_specs
        args = [attn_ctx] + args
    rows = nt * tl
    return pl.pallas_call(
        functools.partial(_merge_body, has_ctx=has_ctx),
        grid=(b, nt),
        in_specs=in_specs,
        out_specs=[out(d), out(d // 2), out(LANES), pl.BlockSpec((1, N_EXPERTS, tl), lambda bb, i: (bb, 0, i))],
        out_shape=[
            jax.ShapeDtypeStruct((b, rows, d), F32),
            jax.ShapeDtypeStruct((b, rows, d // 2), jnp.uint32),
            jax.ShapeDtypeStruct((b, rows, LANES), F32),
            jax.ShapeDtypeStruct((b, N_EXPERTS, rows), F32),
        ],
        compiler_params=_cparams(),
        name="merge_outproj_ln1_router",
    )(*args)


def _topk_body(aff_ref, idx_ref, cum_scr, *, cap):
    aff = aff_ref[0]
    ne, n = aff.shape
    bits = pltpu.bitcast(aff, I32)

    def search(i, thr):
        cand = thr | lax.shift_left(jnp.int32(1), 30 - i)
        cnt = jnp.sum(jnp.where(bits >= cand, 1.0, 0.0), axis=1, keepdims=True)
        return jnp.where(cnt >= cap, cand, thr)

    thr = lax.fori_loop(0, 31, search, jnp.zeros((ne, 1), I32))
    gt = jnp.where(bits > thr, 1.0, 0.0)
    eq = jnp.where(bits == thr, 1.0, 0.0)
    need = cap - jnp.sum(gt, axis=1, keepdims=True)
    nblk = n // LANES
    r_ = lax.broadcasted_iota(I32, (LANES, LANES), 0)
    c_ = lax.broadcasted_iota(I32, (LANES, LANES), 1)
    upper = jnp.where(r_ <= c_, 1.0, 0.0).astype(BF16)

    def prefix(blocks):
        outs = []
        off = jnp.zeros((ne, 1), F32)
        for mk in blocks:
            w = jnp.dot(mk.astype(BF16), upper, preferred_element_type=F32) + off
            outs.append(w)
            off = w[:, LANES - 1:LANES]
        return outs

    blk = lambda a, k: a[:, k * LANES:(k + 1) * LANES]
    tie_rank = prefix([blk(eq, k) for k in range(nblk)])
    cum = prefix([jnp.maximum(blk(gt, k), blk(eq, k) * jnp.where(tie_rank[k] <= need, 1.0, 0.0))
                  for k in range(nblk)])
    for k in range(nblk):
        cum_scr[k] = cum[k]
    rows = -(-cap // LANES) * LANES
    slot = lax.broadcasted_iota(I32, (rows, LANES), 0).astype(F32)

    def per_expert(e, carry):
        acc = jnp.zeros((rows, LANES), F32)
        for k in range(nblk):
            acc = acc + jnp.where(cum_scr[k, pl.ds(e, 1), :] <= slot, 1.0, 0.0)
        idx_ref[0, pl.ds(e, 1), :] = jnp.sum(acc.T, axis=0, keepdims=True)[:, :cap].astype(I32)
        return carry

    lax.fori_loop(0, ne, per_expert, 0)


def _topk(aff_t, cap):
    b, e, n = aff_t.shape
    return pl.pallas_call(
        functools.partial(_topk_body, cap=cap),
        grid=(b,),
        in_specs=[pl.BlockSpec((1, e, n), lambda bb: (bb, 0, 0))],
        out_specs=pl.BlockSpec((1, e, cap), lambda bb: (bb, 0, 0)),
        out_shape=jax.ShapeDtypeStruct((b, e, cap), I32),
        scratch_shapes=[pltpu.VMEM((n // LANES, e, LANES), F32)],
        compiler_params=_cparams(),
        name="expert_choice_topk",
    )(aff_t)


def _moe_body(idx_ref, hp_ref, aff_ref, wg_ref, wu_ref, wd_ref, out_ref, xs_scr, ag_scr, xlo_scr, xhi_scr, y_scr,
              *, cap):
    e = pl.program_id(1)
    f = pl.program_id(2)
    nf = pl.num_programs(2)

    @pl.when((e == 0) & (f == 0))
    def _():
        out_ref[...] = jnp.zeros_like(out_ref)

    @pl.when(f == 0)
    def _():
        def gather(s, carry):
            t = idx_ref[0, 0, 0, s]
            xs_scr[pl.ds(s, 1), :] = hp_ref[0, pl.ds(t, 1), :]
            ag_scr[pl.ds(s, 1), :] = aff_ref[0, pl.ds(t, 1), :]
            return carry

        lax.fori_loop(0, cap, gather, 0, unroll=8)
        w = xs_scr[...]
        xlo_scr[...] = pltpu.bitcast(w << 16, F32).astype(BF16)
        xhi_scr[...] = pltpu.bitcast(w & jnp.uint32(0xFFFF0000), F32).astype(BF16)

    half = xlo_scr.shape[1]
    xlo = xlo_scr[...]
    xhi = xhi_scr[...]
    hg = (jnp.dot(xlo, wg_ref[0, :half, :], preferred_element_type=F32)
          + jnp.dot(xhi, wg_ref[0, half:, :], preferred_element_type=F32))
    hu = (jnp.dot(xlo, wu_ref[0, :half, :], preferred_element_type=F32)
          + jnp.dot(xhi, wu_ref[0, half:, :], preferred_element_type=F32))
    hid = (_silu(hg) * hu).astype(BF16)
    yp = jnp.dot(hid, wd_ref[0], preferred_element_type=F32)

    @pl.when(f == 0)
    def _():
        y_scr[...] = yp

    @pl.when((f > 0) & (f < nf - 1))
    def _():
        y_scr[...] += yp

    @pl.when(f == nf - 1)
    def _():
        lane = lax.broadcasted_iota(I32, ag_scr.shape, 1)
        gate = jnp.sum(jnp.where(lane == e, ag_scr[...], 0.0), axis=1, keepdims=True)
        y_scr[...] = (y_scr[...] + yp) * gate

        def scatter(s, carry):
            t = idx_ref[0, 0, 0, s]
            out_ref[0, pl.ds(t, 1), :] = out_ref[0, pl.ds(t, 1), :] + y_scr[pl.ds(s, 1), :]
            return carry

        lax.fori_loop(0, cap, scatter, 0, unroll=8)


def _moe(idx, hp, aff, wg, wu, wd, tf):
    b, t, half = hp.shape
    d = 2 * half
    ne, _, ff = wg.shape
    cap = idx.shape[-1]
    nf = ff // tf
    assert nf >= 2
    one = pl.Buffered(1)
    return pl.pallas_call(
        functools.partial(_moe_body, cap=cap),
        grid=(b, ne, nf),
        in_specs=[
            pl.BlockSpec((1, 1, 1, cap), lambda bb, e, f: (bb, e, 0, 0), memory_space=pltpu.SMEM),
            pl.BlockSpec((1, t, half), lambda bb, e, f: (bb, 0, 0), pipeline_mode=one),
            pl.BlockSpec((1, t, LANES), lambda bb, e, f: (bb, 0, 0), pipeline_mode=one),
            pl.BlockSpec((1, d, tf), lambda bb, e, f: (e, 0, f)),
            pl.BlockSpec((1, d, tf), lambda bb, e, f: (e, 0, f)),
            pl.BlockSpec((1, tf, d), lambda bb, e, f: (e, f, 0)),
        ],
        out_specs=pl.BlockSpec((1, t, d), lambda bb, e, f: (bb, 0, 0), pipeline_mode=one),
        out_shape=jax.ShapeDtypeStruct((b, t, d), F32),
        scratch_shapes=[
            pltpu.VMEM((cap, half), jnp.uint32),
            pltpu.VMEM((cap, LANES), F32),
            pltpu.VMEM((cap, half), BF16),
            pltpu.VMEM((cap, half), BF16),
            pltpu.VMEM((cap, d), F32),
        ],
        compiler_params=_cparams(VMEM_LIMIT_MOE),
        name="expert_ffn",
    )(idx.reshape(b, ne, 1, cap), hp, aff, wg, wu, wd)


def _ln2_body(x1_ref, moe_ref, g2_ref, w_ref, b_ref, o_ref):
    o_ref[0] = _layer_norm(DEEPNORM_ALPHA * x1_ref[0] + g2_ref[0] * moe_ref[0], w_ref[...], b_ref[...])


def _ln2(x1, moe, modl, w, bias, ctx_row, off):
    b, t, d = x1.shape
    tl = SEQ_TILE
    seq = pl.BlockSpec((1, tl, d), lambda bb, i: (bb, i, 0))
    full = pl.BlockSpec((1, d), lambda bb, i: (0, 0))
    return pl.pallas_call(
        _ln2_body,
        grid=(b, t // tl),
        in_specs=[seq, seq, _mod_spec(5, d, ctx_row, off), full, full],
        out_specs=seq,
        out_shape=jax.ShapeDtypeStruct((b, t, d), F32),
        compiler_params=_cparams(),
        name="ln2",
    )(x1, moe, modl, w, bias)


def _block_diag_ones(width, block):
    r = jnp.arange(width)[:, None] // block
    c = jnp.arange(width)[None, :] // block
    return (r == c).astype(BF16)


def _block_diag(w):
    k, d, e = w.shape
    eye = jnp.eye(k, dtype=w.dtype)
    return (eye[:, None, :, None] * w[:, :, None, :]).reshape(k * d, k * e)


def _rope_tables(n_ctx, n_lat):
    pos = jnp.arange(n_lat)
    row = (pos // GRID_W).astype(F32)
    col = (pos % GRID_W).astype(F32)
    inv_freq = ROPE_THETA ** (-jnp.arange(0, AXIS_DIM, 2, dtype=F32) / AXIS_DIM)
    ang_r = row[:, None] * inv_freq
    ang_c = col[:, None] * inv_freq
    cos_h = jnp.concatenate([jnp.cos(ang_r)] * 2 + [jnp.cos(ang_c)] * 2, axis=1)
    sin_h = jnp.concatenate([-jnp.sin(ang_r), jnp.sin(ang_r), -jnp.sin(ang_c), jnp.sin(ang_c)], axis=1)
    reps = LANES // HEAD_DIM
    cos_t = jnp.concatenate([jnp.ones((n_ctx, LANES), F32), jnp.tile(cos_h, (1, reps))], axis=0)
    sin_t = jnp.concatenate([jnp.zeros((n_ctx, LANES), F32), jnp.tile(sin_h, (1, reps))], axis=0)
    return cos_t, sin_t


def _layer(xt, modl, rope, p, last, n_ctx, ctx_row):
    b, t, d = xt.shape
    n_lat = t - n_ctx
    cos_t, sin_t = rope
    w = p["w_in"]
    pad = jnp.zeros((d, LANES - SSD_HEADS), F32)
    wcat = jnp.concatenate(
        [w[:, 0:768], w[:, 768:1280], w[:, 1544:1800], w[:, 1280:1536], w[:, 1800:2056],
         w[:, 1536:1540], pad, w[:, 1540:1544], pad], axis=1).astype(BF16)
    qw = jnp.tile(p["q_norm"], W_Q // HEAD_DIM)[None]
    kw = jnp.tile(p["k_norm"], W_K // HEAD_DIM)[None]
    qt, k, v, xu, z, g, dt = _inproj(xt, modl, wcat, cos_t, sin_t, qw, kw,
                                     _block_diag_ones(W_Q, HEAD_DIM), _block_diag_ones(W_K, HEAD_DIM), ctx_row)

    qt = qt.reshape(b, W_Q // HEAD_DIM, HEAD_DIM, t)
    kh = k.reshape(b, t, KV_HEADS, HEAD_DIM).transpose(0, 2, 1, 3)
    vt = v.reshape(b, t, KV_HEADS, HEAD_DIM).transpose(0, 2, 3, 1)
    attn_lat = _attention(qt, kh, vt, n_lat, n_ctx, t, ATTN_Q_TILE)
    attn_ctx = None if last else _attention(qt, kh, vt, n_ctx, 0, n_ctx, ATTN_Q_TILE)

    conv_w = jnp.concatenate([p["ssd_conv_w"], p["rg_conv_w"]], axis=1)
    conv_b = jnp.concatenate([p["ssd_conv_b"], p["rg_conv_b"]])[None]
    xbc, u = _conv(xu, conv_w, conv_b, p["ssd_conv_w"].shape[1])
    lane_pad = lambda a: jnp.pad(a, ((0, 0), (0, LANES - a.shape[1])))[:, None, :]
    ys = _ssd(xbc, dt, lane_pad(p["ssd_dt_bias"]), lane_pad(-jnp.exp(p["ssd_a_log"])),
              jnp.repeat(p["ssd_d"], SSD_HEAD_DIM)[None])
    wgate = jnp.stack([jnp.concatenate([_block_diag(p["rg_wa"][j]), _block_diag(p["rg_wx"][j])], axis=1)
                       for j in range(2)]).astype(BF16)
    bgate = jnp.concatenate([p["rg_ba"], p["rg_bx"]], axis=1)[:, None, :]
    hr = _rglru(u, wgate, bgate, p["rg_lambda"][:, None, :])

    off = 1 if last else 0
    wr = jnp.pad(p["w_router"], ((0, 0), (0, LANES - N_EXPERTS))).astype(BF16)
    x1, hp, aff, aff_t = _merge(attn_ctx, attn_lat, ys, z, hr, g, xt, modl,
                                p["attn_out_norm"][None], p["ssd_norm"][None], p["rg_out_norm"][None],
                                p["w_out"].astype(BF16), p["ln1_w"][None], p["ln1_b"][None], wr, ctx_row, off)

    if last:
        idx = _topk(aff_t, EC_CAPACITY * n_lat // N_EXPERTS)
    else:
        idx_lat = _topk(aff_t[:, :, n_ctx:], EC_CAPACITY * n_lat // N_EXPERTS) + n_ctx
        idx_ctx = _topk(aff_t[:, :, :n_ctx], EC_CAPACITY * n_ctx // N_EXPERTS)
        idx = jnp.concatenate([idx_lat, idx_ctx], axis=-1)
    moe = _moe(idx, hp, aff, p["w_gate"].astype(BF16), p["w_up"].astype(BF16), p["w_down"].astype(BF16), 1024)
    return _ln2(x1, moe, modl, p["ln2_w"][None], p["ln2_b"][None], ctx_row, off)


def kernel(x, c, ctx, c_ctx, w_mod, b_mod, w_in, q_norm, k_norm, attn_out_norm, ssd_conv_w, ssd_conv_b, ssd_dt_bias, ssd_a_log, ssd_d, ssd_norm, rg_conv_w, rg_conv_b, rg_wa, rg_ba, rg_wx, rg_bx, rg_lambda, rg_out_norm, w_out, ln1_w, ln1_b, w_router, w_gate, w_up, w_down, ln2_w, ln2_b):
    b, n_lat, d = x.shape
    n_ctx = ctx.shape[1]
    assert n_ctx == SEQ_TILE and n_lat % 512 == 0 and b == SUBLANES
    params = dict(w_in=w_in, q_norm=q_norm, k_norm=k_norm, attn_out_norm=attn_out_norm, ssd_conv_w=ssd_conv_w,
                  ssd_conv_b=ssd_conv_b, ssd_dt_bias=ssd_dt_bias, ssd_a_log=ssd_a_log, ssd_d=ssd_d,
                  ssd_norm=ssd_norm, rg_conv_w=rg_conv_w, rg_conv_b=rg_conv_b, rg_wa=rg_wa, rg_ba=rg_ba,
                  rg_wx=rg_wx, rg_bx=rg_bx, rg_lambda=rg_lambda, rg_out_norm=rg_out_norm, w_out=w_out,
                  ln1_w=ln1_w, ln1_b=ln1_b, w_router=w_router, w_gate=w_gate, w_up=w_up, w_down=w_down,
                  ln2_w=ln2_w, ln2_b=ln2_b)
    mod = _modulation(c, c_ctx, w_mod, b_mod)
    rope = _rope_tables(n_ctx, n_lat)
    xt = jnp.concatenate([ctx, x], axis=1)
    depth = w_mod.shape[0]
    for l in range(depth):
        p = {name: val[l] for name, val in params.items()}
        modl = mod[l].reshape(mod.shape[1], 1, mod.shape[2])
        xt = _layer(xt, modl, rope, p, l == depth - 1, n_ctx, b)
    return xt
```

```python
import functools
import math

import jax
import jax.numpy as jnp
from jax import lax
from jax.experimental import pallas as pl
from jax.experimental.pallas import tpu as pltpu

F32 = jnp.float32
BF16 = jnp.bfloat16
I32 = jnp.int32

DEPTH = 2
GRID_W = 64
HEAD_DIM = 64
KV_HEADS = 2
Q_PER_KV = 4
AXIS_DIM = HEAD_DIM // 2
ROPE_THETA = 10000.0
SSD_HEADS = 4
SSD_HEAD_DIM = 64
SSD_STATE = 64
RG_BLOCKS = 4
RG_C = 8.0
CONV_W = 4
N_EXPERTS = 16
EC_CAPACITY = 2
EPS = 1e-6
DEEPNORM_ALPHA = (2 * DEPTH) ** 0.25

LANES = 128
SUBLANES = 8
SEQ_TILE = 256
ATTN_Q_TILE = 256
ATTN_KEY_CHUNK = 256
MOE_FF_TILE = 1024
MOE_FF_SLICE = 256
MOE_ROW_GROUP = 4
LOG2E = math.log2(math.e)
VMEM_LIMIT = 48 * 1024 * 1024
VMEM_LIMIT_MOE = 60 * 1024 * 1024


def _cparams(limit=VMEM_LIMIT):
    return pltpu.CompilerParams(vmem_limit_bytes=limit)


def _sigmoid(x):
    return 1.0 / (1.0 + jnp.exp(-x))


def _silu(x):
    return x * _sigmoid(x)


def _softplus(x):
    return jnp.maximum(x, 0.0) + jnp.log1p(jnp.exp(-jnp.abs(x)))


def _bdot(a, b):
    return jnp.dot(a.astype(BF16), b.astype(BF16), preferred_element_type=F32)


def _mod_body(c_ref, w_ref, b_ref, o_ref):
    c = c_ref[...]
    o_ref[0] = _bdot(_silu(c), w_ref[0]) + b_ref[0]


def _modulation(c, c_ctx, w_mod, b_mod):
    depth, d, n6 = w_mod.shape
    b = c.shape[0]
    rows = 2 * SUBLANES
    cc = jnp.zeros((rows, d), F32).at[:b].set(c).at[b].set(c_ctx)
    tn = 1536
    return pl.pallas_call(
        _mod_body,
        grid=(depth, n6 // tn),
        in_specs=[
            pl.BlockSpec((rows, d), lambda l, j: (0, 0)),
            pl.BlockSpec((1, d, tn), lambda l, j: (l, 0, j)),
            pl.BlockSpec((1, 1, tn), lambda l, j: (l, 0, j)),
        ],
        out_specs=pl.BlockSpec((1, rows, tn), lambda l, j: (l, 0, j)),
        out_shape=jax.ShapeDtypeStruct((depth, rows, n6), F32),
        compiler_params=_cparams(),
        name="adaln_mod",
    )(cc, w_mod, b_mod.reshape(depth, 1, n6))


def _mod_spec(j, d, ctx_row, off):
    if off == 0:
        return pl.BlockSpec((1, 1, d), lambda b, i: (jnp.where(i == 0, ctx_row, b), 0, j))
    return pl.BlockSpec((1, 1, d), lambda b, i: (b, 0, j))


W_Q, W_K, W_V, W_XU, W_Z, W_G, W_DT = 512, 128, 128, 768, 256, 256, 256
IN_PAD = W_Q + W_K + W_V + W_XU + W_Z + W_G + W_DT


def _rope(t, cos, sin_signed):
    rows = t.shape[0]
    lane = lax.broadcasted_iota(I32, (rows, LANES), 1)
    first = (lane % AXIS_DIM) < (AXIS_DIM // 2)
    outs = []
    for c in range(t.shape[1] // LANES):
        tc = t[:, c * LANES:(c + 1) * LANES]
        partner = jnp.where(first, pltpu.roll(tc, LANES - AXIS_DIM // 2, 1), pltpu.roll(tc, AXIS_DIM // 2, 1))
        outs.append(tc * cos + partner * sin_signed)
    return outs[0] if len(outs) == 1 else jnp.concatenate(outs, axis=1)


def _inproj_body(x_ref, sc_ref, sh_ref, w_ref, cos_ref, sin_ref, qw_ref, kw_ref, oq_ref, ok_ref,
                 q_ref, k_ref, v_ref, xu_ref, z_ref, g_ref, dt_ref):
    h = x_ref[0] * (1.0 + sc_ref[0]) + sh_ref[0]
    y = jnp.dot(h.astype(BF16), w_ref[...], preferred_element_type=F32)
    o = 0
    q = y[:, o:o + W_Q]; o += W_Q
    k = y[:, o:o + W_K]; o += W_K
    v = y[:, o:o + W_V]; o += W_V
    xu_ref[0] = y[:, o:o + W_XU]; o += W_XU
    z_ref[0] = y[:, o:o + W_Z]; o += W_Z
    g_ref[0] = y[:, o:o + W_G]; o += W_G
    dt_ref[0] = y[:, o:o + W_DT]
    cos = cos_ref[...]
    sin = sin_ref[...]
    ssq = jnp.dot((q * q).astype(BF16), oq_ref[...], preferred_element_type=F32)
    qn = q * lax.rsqrt(ssq * (1.0 / HEAD_DIM) + EPS) * qw_ref[...]
    q_ref[0] = (_rope(qn, cos, sin) * (HEAD_DIM ** -0.5 * LOG2E)).T.astype(BF16)
    ssk = jnp.dot((k * k).astype(BF16), ok_ref[...], preferred_element_type=F32)
    kn = k * lax.rsqrt(ssk * (1.0 / HEAD_DIM) + EPS) * kw_ref[...]
    k_ref[0] = _rope(kn, cos, sin).astype(BF16)
    v_ref[0] = v.astype(BF16)


def _inproj(xt, modl, wcat, cos_t, sin_t, qw, kw, ones_q, ones_k, ctx_row):
    b, t, d = xt.shape
    tl = SEQ_TILE
    grid = (b, t // tl)
    full = lambda shape: pl.BlockSpec(shape, lambda bb, i: (0,) * len(shape))
    seq = lambda w: pl.BlockSpec((1, tl, w), lambda bb, i: (bb, i, 0))
    outs = [(W_K, BF16), (W_V, BF16), (W_XU, F32), (W_Z, F32), (W_G, F32), (W_DT, F32)]
    return pl.pallas_call(
        _inproj_body,
        grid=grid,
        in_specs=[
            seq(d),
            _mod_spec(1, d, ctx_row, 0),
            _mod_spec(0, d, ctx_row, 0),
            full((d, IN_PAD)),
            pl.BlockSpec((tl, LANES), lambda bb, i: (i, 0)),
            pl.BlockSpec((tl, LANES), lambda bb, i: (i, 0)),
            full((1, W_Q)), full((1, W_K)), full((W_Q, W_Q)), full((W_K, W_K)),
        ],
        out_specs=[pl.BlockSpec((1, W_Q, tl), lambda bb, i: (bb, 0, i))] + [seq(w) for w, _ in outs],
        out_shape=[jax.ShapeDtypeStruct((b, W_Q, t), BF16)]
        + [jax.ShapeDtypeStruct((b, t, w), dt) for w, dt in outs],
        compiler_params=_cparams(),
        name="in_proj",
    )(xt, modl, modl, wcat, cos_t, sin_t, qw, kw, ones_q, ones_k)


def _conv_body(x_ref, prev_ref, next_ref, w_ref, b_ref, xbc_ref, u_ref, *, n_xbc):
    i = pl.program_id(1)
    n = pl.num_programs(1)
    x = x_ref[0]
    tl = x.shape[0]
    has_prev = i > 1
    has_next = (i > 0) & (i < n - 1)
    pm = jnp.where(has_prev, prev_ref[0, SUBLANES - 1:SUBLANES, :], 0.0)
    n0 = jnp.where(has_next, next_ref[0, 0:1, :], 0.0)
    n1 = jnp.where(has_next, next_ref[0, 1:2, :], 0.0)
    row = lax.broadcasted_iota(I32, x.shape, 0)
    xm1 = jnp.where(row == 0, pm, pltpu.roll(x, 1, 0))
    xp1 = jnp.where(row == tl - 1, n0, pltpu.roll(x, tl - 1, 0))
    xp2 = jnp.where(row == tl - 1, n1, jnp.where(row == tl - 2, n0, pltpu.roll(x, tl - 2, 0)))
    w = w_ref[...]
    y = xm1 * w[0:1] + x * w[1:2] + xp1 * w[2:3] + xp2 * w[3:4] + b_ref[...]
    xbc = y[:, :n_xbc]
    xbc_ref[0] = _silu(xbc)
    u_ref[0] = y[:, n_xbc:]


def _conv(xu, w, bias, n_xbc):
    b, t, c = xu.shape
    tl = SEQ_TILE
    nt = t // tl
    r = tl // SUBLANES
    nb = t // SUBLANES
    return pl.pallas_call(
        functools.partial(_conv_body, n_xbc=n_xbc),
        grid=(b, nt),
        in_specs=[
            pl.BlockSpec((1, tl, c), lambda bb, i: (bb, i, 0)),
            pl.BlockSpec((1, SUBLANES, c), lambda bb, i: (bb, jnp.maximum(i * r - 1, 0), 0)),
            pl.BlockSpec((1, SUBLANES, c), lambda bb, i: (bb, jnp.minimum((i + 1) * r, nb - 1), 0)),
            pl.BlockSpec((CONV_W, c), lambda bb, i: (0, 0)),
            pl.BlockSpec((1, c), lambda bb, i: (0, 0)),
        ],
        out_specs=[
            pl.BlockSpec((1, tl, n_xbc), lambda bb, i: (bb, i, 0)),
            pl.BlockSpec((1, tl, c - n_xbc), lambda bb, i: (bb, i, 0)),
        ],
        out_shape=[jax.ShapeDtypeStruct((b, t, n_xbc), F32), jax.ShapeDtypeStruct((b, t, c - n_xbc), F32)],
        compiler_params=_cparams(),
        name="dwconv",
    )(xu, xu, xu, w, bias)


def _attn_body(qt_ref, k_ref, vt_ref, o_ref, s_buf, p_buf, acc_ref, *, n_keys, kc):
    nh, hd, tq = qt_ref.shape[1:]
    qt = jnp.concatenate([qt_ref[0, h] for h in range(nh)], axis=1)
    cols = nh * tq
    rem = n_keys % kc
    chunks = ([(0, rem)] if rem else []) + [(rem + kc * i, kc) for i in range(n_keys // kc)]
    n = len(chunks)

    def scores(c):
        s0, sz = chunks[c]
        s_buf[c % 2, :sz, :] = jnp.dot(k_ref[0, 0, s0:s0 + sz, :], qt, preferred_element_type=F32)

    def softmax(c, m, l):
        sz = chunks[c][1]
        s = s_buf[c % 2, :sz, :]
        m_new = jnp.maximum(m, jnp.max(s, axis=0, keepdims=True))
        alpha = jnp.exp2(m - m_new)
        p = jnp.exp2(s - m_new)
        p_buf[c % 2, :sz, :] = p.astype(BF16)
        return m_new, alpha * l + jnp.sum(p, axis=0, keepdims=True), alpha

    def weighted_values(c, alpha):
        s0, sz = chunks[c]
        acc_ref[...] = alpha * acc_ref[...] + jnp.dot(vt_ref[0, 0, :, s0:s0 + sz], p_buf[c % 2, :sz, :],
                                                      preferred_element_type=F32)

    acc_ref[...] = jnp.zeros_like(acc_ref)
    m = jnp.full((1, cols), -jnp.inf, F32)
    l = jnp.zeros((1, cols), F32)
    alpha = None
    scores(0)
    for j in range(n + 1):
        if j + 1 < n:
            scores(j + 1)
        prev_alpha = alpha
        if j < n:
            m, l, alpha = softmax(j, m, l)
        if j >= 1:
            weighted_values(j - 1, prev_alpha)
    o = acc_ref[...] * (1.0 / l)
    for h in range(nh):
        o_ref[0, h * hd:(h + 1) * hd, :] = o[:, h * tq:(h + 1) * tq]


def _attention(qt, k, vt, n_q, q_off, n_keys, tq):
    b = qt.shape[0]
    off = q_off // tq
    kc = min(ATTN_KEY_CHUNK, n_keys)
    cols = Q_PER_KV * tq
    return pl.pallas_call(
        functools.partial(_attn_body, n_keys=n_keys, kc=kc),
        grid=(b, KV_HEADS, n_q // tq),
        scratch_shapes=[pltpu.VMEM((2, kc, cols), F32), pltpu.VMEM((2, kc, cols), BF16),
                        pltpu.VMEM((HEAD_DIM, cols), F32)],
        in_specs=[
            pl.BlockSpec((1, Q_PER_KV, HEAD_DIM, tq), lambda bb, g, i: (bb, g, 0, i + off)),
            pl.BlockSpec((1, 1, n_keys, HEAD_DIM), lambda bb, g, i: (bb, g, 0, 0)),
            pl.BlockSpec((1, 1, HEAD_DIM, n_keys), lambda bb, g, i: (bb, g, 0, 0)),
        ],
        out_specs=pl.BlockSpec((1, Q_PER_KV * HEAD_DIM, tq), lambda bb, g, i: (bb, g, i)),
        out_shape=jax.ShapeDtypeStruct((b, KV_HEADS * Q_PER_KV * HEAD_DIM, n_q), F32),
        compiler_params=_cparams(),
        name="gqa_attention",
    )(qt, k, vt)


def _seq_order(d, c, nc):
    return jnp.where(c == 0, 0, jnp.where(d == 0, c, nc - c))


def _ssd_body(xbc_ref, dt_ref, bias_ref, a_ref, dvec_ref, y_ref, s_scr):
    d = pl.program_id(1)
    c = pl.program_id(2)

    @pl.when(c == 0)
    def _():
        s_scr[...] = jnp.zeros_like(s_scr)

    xbc = xbc_ref[0]
    q = xbc.shape[0]
    xw = SSD_HEADS * SSD_HEAD_DIM
    gw = 2 * SSD_STATE
    x = xbc[:, :xw]
    bm = xbc[:, xw:xw + gw]
    cm = xbc[:, xw + gw:xw + 2 * gw]
    dt = _softplus(dt_ref[0] + bias_ref[0])
    da = dt * a_ref[0]
    ii = lax.broadcasted_iota(I32, (q, q), 0)
    jj = lax.broadcasted_iota(I32, (q, q), 1)
    mask = ((ii - jj) * (1 - 2 * d)) >= 0
    tm = jnp.where(mask, 1.0, 0.0).astype(BF16)
    da_hi = da.astype(BF16)
    da_lo = (da - da_hi.astype(F32)).astype(BF16)
    cs = jnp.dot(tm, da_hi, preferred_element_type=F32) + jnp.dot(tm, da_lo, preferred_element_type=F32)
    tot = jnp.where(d == 0, cs[q - 1:q, :], cs[0:1, :])
    cst = cs.T
    dec = jnp.exp(tot - cs)
    ecs = jnp.exp(cs)
    etot = jnp.exp(tot)
    ys = []
    nt = (((1,), (1,)), ((), ()))
    tn = (((0,), (0,)), ((), ()))
    for g in range(2):
        bg = bm[:, g * SSD_STATE:(g + 1) * SSD_STATE].astype(BF16)
        cg = cm[:, g * SSD_STATE:(g + 1) * SSD_STATE].astype(BF16)
        gmat = lax.dot_general(cg, bg, nt, preferred_element_type=F32)
        for hh in range(SSD_HEADS // 2):
            h = 2 * g + hh
            lmat = jnp.exp(jnp.where(mask, cs[:, h:h + 1] - cst[h:h + 1, :], -jnp.inf))
            xh = x[:, h * SSD_HEAD_DIM:(h + 1) * SSD_HEAD_DIM]
            xdt = xh * dt[:, h:h + 1]
            y_diag = _bdot(gmat * lmat, xdt)
            s_in = s_scr[h]
            y_off = lax.dot_general(cg, s_in.astype(BF16), nt, preferred_element_type=F32) * ecs[:, h:h + 1]
            ys.append(y_diag + y_off)
            xd = (xdt * dec[:, h:h + 1]).astype(BF16)
            s_scr[h] = etot[:, h:h + 1] * s_in + lax.dot_general(xd, bg, tn, preferred_element_type=F32)
    y = jnp.concatenate(ys, axis=1)
    y_ref[0, 0] = y + x * (dvec_ref[...] * jnp.where(d == 0, 1.0, 0.0))


def _ssd(xbc, dt, bias, a_neg, dvec):
    b, t, cw = xbc.shape
    q = SEQ_TILE
    nc = t // q
    xw = SSD_HEADS * SSD_HEAD_DIM
    return pl.pallas_call(
        _ssd_body,
        grid=(b, 2, nc),
        in_specs=[
            pl.BlockSpec((1, q, cw), lambda bb, d, c: (bb, _seq_order(d, c, nc), 0)),
            pl.BlockSpec((1, q, LANES), lambda bb, d, c: (bb, _seq_order(d, c, nc), d)),
            pl.BlockSpec((1, 1, LANES), lambda bb, d, c: (d, 0, 0)),
            pl.BlockSpec((1, 1, LANES), lambda bb, d, c: (d, 0, 0)),
            pl.BlockSpec((1, xw), lambda bb, d, c: (0, 0)),
        ],
        out_specs=pl.BlockSpec((1, 1, q, xw), lambda bb, d, c: (d, bb, _seq_order(d, c, nc), 0)),
        out_shape=jax.ShapeDtypeStruct((2, b, t, xw), F32),
        scratch_shapes=[pltpu.VMEM((SSD_HEADS, SSD_HEAD_DIM, SSD_STATE), F32)],
        compiler_params=_cparams(),
        name="ssd_scan",
    )(xbc, dt, bias, a_neg, dvec)


def _rg_body(u_ref, w_ref, bias_ref, lam_ref, y_ref, a_s, v_s, o_s, h_s, *, pitch):
    d = pl.program_id(0)
    c = pl.program_id(1)
    nb, tl, width = u_ref.shape
    ng = width // LANES

    @pl.when(c == 0)
    def _():
        h_s[...] = jnp.zeros_like(h_s)

    sp = _softplus(-lam_ref[0])
    w = w_ref[0]
    bias = bias_ref[0]
    for b in range(nb):
        ub = u_ref[b]
        pre = jnp.dot(ub.astype(BF16), w, preferred_element_type=F32) + bias
        r = _sigmoid(pre[:, :width])
        ig = _sigmoid(pre[:, width:])
        a = jnp.exp((-RG_C) * r * sp)
        v = jnp.sqrt(1.0 - a * a) * ig * ub
        for j in range(ng):
            a_s[j, pl.ds(b * pitch, tl), :] = a[:, j * LANES:(j + 1) * LANES]
            v_s[j, pl.ds(b * pitch, tl), :] = v[:, j * LANES:(j + 1) * LANES]

    def step(t, hs):
        te = jnp.where(d == 0, t, tl - 1 - t)
        out = []
        for j in range(ng):
            at = a_s[j, pl.ds(te, nb, stride=pitch), :]
            vt = v_s[j, pl.ds(te, nb, stride=pitch), :]
            hj = at * hs[j] + vt
            o_s[j, pl.ds(te, nb, stride=pitch), :] = hj
            out.append(hj)
        return tuple(out)

    h0 = tuple(h_s[:, j * LANES:(j + 1) * LANES] for j in range(ng))
    hf = lax.fori_loop(0, tl, step, h0, unroll=8)
    for j in range(ng):
        h_s[:, j * LANES:(j + 1) * LANES] = hf[j]
    for b in range(nb):
        y_ref[0, b] = jnp.concatenate([o_s[j, pl.ds(b * pitch, tl), :] for j in range(ng)], axis=1)


def _rglru(u, wg, bias, lam):
    b, t, width = u.shape
    assert b == SUBLANES, "the recurrence keeps one sample per sublane"
    tl = SEQ_TILE
    nc = t // tl
    pitch = tl + SUBLANES
    ng = width // LANES
    slab = pltpu.VMEM((ng, b * pitch, LANES), F32)
    return pl.pallas_call(
        functools.partial(_rg_body, pitch=pitch),
        grid=(2, nc),
        in_specs=[
            pl.BlockSpec((b, tl, width), lambda d, c: (0, _seq_order(d, c, nc), 0)),
            pl.BlockSpec((1, width, 2 * width), lambda d, c: (d, 0, 0)),
            pl.BlockSpec((1, 1, 2 * width), lambda d, c: (d, 0, 0)),
            pl.BlockSpec((1, 1, width), lambda d, c: (d, 0, 0)),
        ],
        out_specs=pl.BlockSpec((1, b, tl, width), lambda d, c: (d, 0, _seq_order(d, c, nc), 0)),
        out_shape=jax.ShapeDtypeStruct((2, b, t, width), F32),
        scratch_shapes=[slab, slab, slab, pltpu.VMEM((b, width), F32)],
        compiler_params=_cparams(),
        name="rglru_scan",
    )(u, wg, bias, lam)


def _rms(x, w):
    return x * lax.rsqrt(jnp.mean(x * x, axis=-1, keepdims=True) + EPS) * w


def _layer_norm(t, w, b):
    mu = jnp.mean(t, axis=-1, keepdims=True)
    tc = t - mu
    var = jnp.mean(tc * tc, axis=-1, keepdims=True)
    return tc * lax.rsqrt(var + EPS) * w + b


def _gelu_tanh(x):
    return 0.5 * x * (1.0 + jnp.tanh(math.sqrt(2.0 / math.pi) * (x + 0.044715 * (x * x * x))))


def _pack_bf16_pair(lo, hi):
    lb = pltpu.bitcast(lo.astype(BF16).astype(F32), jnp.uint32)
    hb = pltpu.bitcast(hi.astype(BF16).astype(F32), jnp.uint32)
    return (lb >> 16) | (hb & jnp.uint32(0xFFFF0000))


def _merge_body(*refs, has_ctx):
    if has_ctx:
        actx_ref, refs = refs[0], refs[1:]
    (alat_ref, ys_ref, z_ref, hr_ref, g_ref, x_ref, g1_ref, sc2_ref, sh2_ref, aw_ref, sw_ref, rw_ref,
     wo_ref, lnw_ref, lnb_ref, wr_ref, x1_ref, hp_ref, aff_ref, afft_ref) = refs
    a = alat_ref[0]
    if has_ctx:
        a = jnp.where(pl.program_id(1) == 0, actx_ref[0], a)
    an = _rms(a.T, aw_ref[...])
    sn = _rms((ys_ref[0, 0] + ys_ref[1, 0]) * _silu(z_ref[0]), sw_ref[...])
    rn = _rms((hr_ref[0, 0] + hr_ref[1, 0]) * _gelu_tanh(g_ref[0]), rw_ref[...])
    cat = jnp.concatenate([an, sn, rn], axis=1).astype(BF16)
    proj = jnp.dot(cat, wo_ref[...], preferred_element_type=F32)
    x1 = _layer_norm(DEEPNORM_ALPHA * x_ref[0] + g1_ref[0] * proj, lnw_ref[...], lnb_ref[...])
    x1_ref[0] = x1
    h2 = x1 * (1.0 + sc2_ref[0]) + sh2_ref[0]
    half = h2.shape[1] // 2
    hp_ref[0] = _pack_bf16_pair(h2[:, :half], h2[:, half:])
    logits = jnp.dot(h2.astype(BF16), wr_ref[...], preferred_element_type=F32)
    lane = lax.broadcasted_iota(I32, logits.shape, 1)
    logits = jnp.where(lane < N_EXPERTS, logits, -jnp.inf)
    e = jnp.exp(logits - jnp.max(logits, axis=-1, keepdims=True))
    aff = e / jnp.sum(e, axis=-1, keepdims=True)
    aff_ref[0] = aff
    afft_ref[0] = aff.T[:N_EXPERTS, :]


def _merge(attn_ctx, attn_lat, ys, z, hr, g, xt, modl, aw, sw, rw, wo, lnw, lnb, wr, ctx_row, off):
    b, t, d = xt.shape
    tl = SEQ_TILE
    nt = t // tl - off
    has_ctx = off == 0
    aw_ = attn_lat.shape[1]
    seq = lambda w: pl.BlockSpec((1, tl, w), lambda bb, i: (bb, i + off, 0))
    pair = lambda w: pl.BlockSpec((2, 1, tl, w), lambda bb, i: (0, bb, i + off, 0))
    full = lambda shape: pl.BlockSpec(shape, lambda bb, i: (0,) * len(shape))
    out = lambda w: pl.BlockSpec((1, tl, w), lambda bb, i: (bb, i, 0))
    lat_off = 1 - off
    in_specs = [
        pl.BlockSpec((1, aw_, tl), lambda bb, i: (bb, 0, jnp.maximum(i - lat_off, 0))),
        pair(ys.shape[-1]), seq(z.shape[-1]), pair(hr.shape[-1]), seq(g.shape[-1]), seq(d),
        _mod_spec(2, d, ctx_row, off), _mod_spec(4, d, ctx_row, off), _mod_spec(3, d, ctx_row, off),
        full(aw.shape), full(sw.shape), full(rw.shape), full(wo.shape), full(lnw.shape), full(lnb.shape),
        full(wr.shape),
    ]
    args = [attn_lat, ys, z, hr, g, xt, modl, modl, modl, aw, sw, rw, wo, lnw, lnb, wr]
    if has_ctx:
        in_specs = [pl.BlockSpec((1, aw_, tl), lambda bb, i: (bb, 0, 0))] + in_specs
        args = [attn_ctx] + args
    rows = nt * tl
    return pl.pallas_call(
        functools.partial(_merge_body, has_ctx=has_ctx),
        grid=(b, nt),
        in_specs=in_specs,
        out_specs=[out(d), out(d // 2), out(LANES), pl.BlockSpec((1, N_EXPERTS, tl), lambda bb, i: (bb, 0, i))],
        out_shape=[
            jax.ShapeDtypeStruct((b, rows, d), F32),
            jax.ShapeDtypeStruct((b, rows, d // 2), jnp.uint32),
            jax.ShapeDtypeStruct((b, rows, LANES), F32),
            jax.ShapeDtypeStruct((b, N_EXPERTS, rows), F32),
        ],
        compiler_params=_cparams(),
        name="merge_outproj_ln1_router",
    )(*args)


def _topk_body(aff_ref, idx_ref, cum_scr, *, cap):
    aff = aff_ref[0]
    ne, n = aff.shape
    bits = pltpu.bitcast(aff, I32)

    def search(i, thr):
        cand = thr | lax.shift_left(jnp.int32(1), 30 - i)
        cnt = jnp.sum(jnp.where(bits >= cand, 1.0, 0.0), axis=1, keepdims=True)
        return jnp.where(cnt >= cap, cand, thr)

    thr = lax.fori_loop(0, 31, search, jnp.zeros((ne, 1), I32))
    gt = jnp.where(bits > thr, 1.0, 0.0)
    eq = jnp.where(bits == thr, 1.0, 0.0)
    need = cap - jnp.sum(gt, axis=1, keepdims=True)
    nblk = n // LANES
    r_ = lax.broadcasted_iota(I32, (LANES, LANES), 0)
    c_ = lax.broadcasted_iota(I32, (LANES, LANES), 1)
    upper = jnp.where(r_ <= c_, 1.0, 0.0).astype(BF16)

    def prefix(blocks):
        outs = []
        off = jnp.zeros((ne, 1), F32)
        for mk in blocks:
            w = jnp.dot(mk.astype(BF16), upper, preferred_element_type=F32) + off
            outs.append(w)
            off = w[:, LANES - 1:LANES]
        return outs

    blk = lambda a, k: a[:, k * LANES:(k + 1) * LANES]
    tie_rank = prefix([blk(eq, k) for k in range(nblk)])
    cum = prefix([jnp.maximum(blk(gt, k), blk(eq, k) * jnp.where(tie_rank[k] <= need, 1.0, 0.0))
                  for k in range(nblk)])
    for k in range(nblk):
        cum_scr[k] = cum[k]
    rows = -(-cap // LANES) * LANES
    slot = lax.broadcasted_iota(I32, (rows, LANES), 0).astype(F32)

    def per_expert(e, carry):
        acc = jnp.zeros((rows, LANES), F32)
        for k in range(nblk):
            acc = acc + jnp.where(cum_scr[k, pl.ds(e, 1), :] <= slot, 1.0, 0.0)
        idx_ref[0, pl.ds(e, 1), :] = jnp.sum(acc.T, axis=0, keepdims=True)[:, :cap].astype(I32)
        return carry

    lax.fori_loop(0, ne, per_expert, 0)


def _topk(aff_t, cap):
    b, e, n = aff_t.shape
    return pl.pallas_call(
        functools.partial(_topk_body, cap=cap),
        grid=(b,),
        in_specs=[pl.BlockSpec((1, e, n), lambda bb: (bb, 0, 0))],
        out_specs=pl.BlockSpec((1, e, cap), lambda bb: (bb, 0, 0)),
        out_shape=jax.ShapeDtypeStruct((b, e, cap), I32),
        scratch_shapes=[pltpu.VMEM((n // LANES, e, LANES), F32)],
        compiler_params=_cparams(),
        name="expert_choice_topk",
    )(aff_t)


def _moe_body(idxp_ref, idxc_ref, idxn_ref, hp_ref, aff_ref, wg_ref, wu_ref, wd_ref, out_ref,
              xs_scr, ag_scr, y_scr, xlo_scr, xhi_scr, hid_scr, *, cap):
    e = pl.program_id(1)
    f = pl.program_id(2)
    ne = pl.num_programs(1)
    nf = pl.num_programs(2)
    cur = e % 2
    oth = 1 - cur

    @pl.when((pl.program_id(0) == 0) & (e == 0) & (f == 0))
    def _():
        y_scr[...] = jnp.zeros_like(y_scr)

    @pl.when((e == 0) & (f == 0))
    def _():
        out_ref[...] = jnp.zeros_like(out_ref)

        def gather(j, carry):
            for u in range(SUBLANES):
                t = idxc_ref[0, 0, 0, j * SUBLANES + u]
                xs_scr[0, j, u:u + 1, :] = hp_ref[0, pl.ds(t, 1), :]
                ag_scr[0, j, u:u + 1, :] = aff_ref[0, pl.ds(t, 1), :]
            return carry

        lax.fori_loop(0, cap // SUBLANES, gather, 0)

    w = xs_scr[cur].reshape(cap, xs_scr.shape[-1])
    half = w.shape[1]
    xlo_scr[...] = pltpu.bitcast(w << 16, F32).astype(BF16)
    xhi_scr[...] = pltpu.bitcast(w & jnp.uint32(0xFFFF0000), F32).astype(BF16)
    lane = lax.broadcasted_iota(I32, (cap, LANES), 1)
    gate = jnp.sum(jnp.where(lane == e, ag_scr[cur].reshape(cap, LANES), 0.0), axis=1, keepdims=True)
    share = cap // nf
    base = f * share
    base_tile = f * (share // SUBLANES)
    has_prev = e > 0

    def row_copies(r0, r1):
        for g0 in range(r0, r1, MOE_ROW_GROUP):
            rows = range(g0, min(g0 + MOE_ROW_GROUP, r1))
            tps = [idxp_ref[0, 0, 0, base + r] for r in rows]
            sums = [out_ref[0, pl.ds(tp, 1), :]
                    + jnp.where(has_prev, y_scr[oth, base_tile + r // SUBLANES, r % SUBLANES:r % SUBLANES + 1, :], 0.0)
                    for tp, r in zip(tps, rows)]
            for tp, v in zip(tps, sums):
                out_ref[0, pl.ds(tp, 1), :] = v
            for r in rows:
                tn = idxn_ref[0, 0, 0, base + r]
                j, u = base_tile + r // SUBLANES, r % SUBLANES
                xs_scr[oth, j, u:u + 1, :] = hp_ref[0, pl.ds(tn, 1), :]
                ag_scr[oth, j, u:u + 1, :] = aff_ref[0, pl.ds(tn, 1), :]

    tf = wg_ref.shape[3]
    nsl = tf // MOE_FF_SLICE
    sl = lambda c: slice(c * MOE_FF_SLICE, (c + 1) * MOE_FF_SLICE)

    def hidden(c):
        xlo = xlo_scr[...]
        xhi = xhi_scr[...]
        hg = (jnp.dot(xlo, wg_ref[0, 0, :half, sl(c)], preferred_element_type=F32)
              + jnp.dot(xhi, wg_ref[0, 0, half:, sl(c)], preferred_element_type=F32))
        hu = (jnp.dot(xlo, wu_ref[0, 0, :half, sl(c)], preferred_element_type=F32)
              + jnp.dot(xhi, wu_ref[0, 0, half:, sl(c)], preferred_element_type=F32))
        hid_scr[c % 2] = (_silu(hg) * hu).astype(BF16)

    hidden(0)
    yp = None
    for c in range(nsl):
        if c + 1 < nsl:
            hidden(c + 1)
        part = jnp.dot(hid_scr[c % 2], wd_ref[0, 0, sl(c), :], preferred_element_type=F32)
        yp = part if yp is None else yp + part
        row_copies(share * c // nsl, share * (c + 1) // nsl)
    y_old = y_scr[cur].reshape(cap, y_scr.shape[-1])
    y_new = (jnp.where(f == 0, 0.0, y_old) + yp) * jnp.where(f == nf - 1, gate, 1.0)
    y_scr[cur] = y_new.reshape(y_scr.shape[1:])

    @pl.when((e == ne - 1) & (f == nf - 1))
    def _():
        def scatter(j, carry):
            for u in range(SUBLANES):
                t = idxc_ref[0, 0, 0, j * SUBLANES + u]
                out_ref[0, pl.ds(t, 1), :] = out_ref[0, pl.ds(t, 1), :] + y_scr[cur, j, u:u + 1, :]
            return carry

        lax.fori_loop(0, cap // SUBLANES, scatter, 0)


def _moe(idx, hp, aff, wg, wu, wd, layer, tf):
    b, t, half = hp.shape
    d = 2 * half
    ne, ff = wg.shape[1], wg.shape[3]
    cap = idx.shape[-1]
    nf = ff // tf
    assert nf >= 2 and cap % nf == 0
    one = pl.Buffered(1)
    idx4 = idx.reshape(b, ne, 1, cap)
    smem = lambda shift: pl.BlockSpec((1, 1, 1, cap), lambda bb, e, f: (bb, jnp.clip(e + shift, 0, ne - 1), 0, 0),
                                      memory_space=pltpu.SMEM)
    return pl.pallas_call(
        functools.partial(_moe_body, cap=cap),
        grid=(b, ne, nf),
        in_specs=[
            smem(-1), smem(0), smem(1),
            pl.BlockSpec((1, t, half), lambda bb, e, f: (bb, 0, 0), pipeline_mode=one),
            pl.BlockSpec((1, t, LANES), lambda bb, e, f: (bb, 0, 0), pipeline_mode=one),
            pl.BlockSpec((1, 1, d, tf), lambda bb, e, f: (layer, e, 0, f)),
            pl.BlockSpec((1, 1, d, tf), lambda bb, e, f: (layer, e, 0, f)),
            pl.BlockSpec((1, 1, tf, d), lambda bb, e, f: (layer, e, f, 0)),
        ],
        out_specs=pl.BlockSpec((1, t, d), lambda bb, e, f: (bb, 0, 0), pipeline_mode=one),
        out_shape=jax.ShapeDtypeStruct((b, t, d), F32),
        scratch_shapes=[
            pltpu.VMEM((2, cap // SUBLANES, SUBLANES, half), jnp.uint32),
            pltpu.VMEM((2, cap // SUBLANES, SUBLANES, LANES), F32),
            pltpu.VMEM((2, cap // SUBLANES, SUBLANES, d), F32),
            pltpu.VMEM((cap, half), BF16),
            pltpu.VMEM((cap, half), BF16),
            pltpu.VMEM((2, cap, MOE_FF_SLICE), BF16),
        ],
        compiler_params=_cparams(VMEM_LIMIT_MOE),
        name="expert_ffn",
    )(idx4, idx4, idx4, hp, aff, wg, wu, wd)


def _ln2_body(x1_ref, moe_ref, g2_ref, w_ref, b_ref, o_ref):
    o_ref[0] = _layer_norm(DEEPNORM_ALPHA * x1_ref[0] + g2_ref[0] * moe_ref[0], w_ref[...], b_ref[...])


def _ln2(x1, moe, modl, w, bias, ctx_row, off):
    b, t, d = x1.shape
    tl = SEQ_TILE
    seq = pl.BlockSpec((1, tl, d), lambda bb, i: (bb, i, 0))
    full = pl.BlockSpec((1, d), lambda bb, i: (0, 0))
    return pl.pallas_call(
        _ln2_body,
        grid=(b, t // tl),
        in_specs=[seq, seq, _mod_spec(5, d, ctx_row, off), full, full],
        out_specs=seq,
        out_shape=jax.ShapeDtypeStruct((b, t, d), F32),
        compiler_params=_cparams(),
        name="ln2",
    )(x1, moe, modl, w, bias)


def _block_diag_ones(width, block):
    r = jnp.arange(width)[:, None] // block
    c = jnp.arange(width)[None, :] // block
    return (r == c).astype(BF16)


def _block_diag(w):
    k, d, e = w.shape
    eye = jnp.eye(k, dtype=w.dtype)
    return (eye[:, None, :, None] * w[:, :, None, :]).reshape(k * d, k * e)


def _rope_tables(n_ctx, n_lat):
    pos = jnp.arange(n_lat)
    row = (pos // GRID_W).astype(F32)
    col = (pos % GRID_W).astype(F32)
    inv_freq = ROPE_THETA ** (-jnp.arange(0, AXIS_DIM, 2, dtype=F32) / AXIS_DIM)
    ang_r = row[:, None] * inv_freq
    ang_c = col[:, None] * inv_freq
    cos_h = jnp.concatenate([jnp.cos(ang_r)] * 2 + [jnp.cos(ang_c)] * 2, axis=1)
    sin_h = jnp.concatenate([-jnp.sin(ang_r), jnp.sin(ang_r), -jnp.sin(ang_c), jnp.sin(ang_c)], axis=1)
    reps = LANES // HEAD_DIM
    cos_t = jnp.concatenate([jnp.ones((n_ctx, LANES), F32), jnp.tile(cos_h, (1, reps))], axis=0)
    sin_t = jnp.concatenate([jnp.zeros((n_ctx, LANES), F32), jnp.tile(sin_h, (1, reps))], axis=0)
    return cos_t, sin_t


def _layer(xt, modl, rope, p, experts, layer, last, n_ctx, ctx_row):
    b, t, d = xt.shape
    n_lat = t - n_ctx
    cos_t, sin_t = rope
    w = p["w_in"]
    pad = jnp.zeros((d, LANES - SSD_HEADS), F32)
    wcat = jnp.concatenate(
        [w[:, 0:768], w[:, 768:1280], w[:, 1544:1800], w[:, 1280:1536], w[:, 1800:2056],
         w[:, 1536:1540], pad, w[:, 1540:1544], pad], axis=1).astype(BF16)
    qw = jnp.tile(p["q_norm"], W_Q // HEAD_DIM)[None]
    kw = jnp.tile(p["k_norm"], W_K // HEAD_DIM)[None]
    qt, k, v, xu, z, g, dt = _inproj(xt, modl, wcat, cos_t, sin_t, qw, kw,
                                     _block_diag_ones(W_Q, HEAD_DIM), _block_diag_ones(W_K, HEAD_DIM), ctx_row)

    qt = qt.reshape(b, W_Q // HEAD_DIM, HEAD_DIM, t)
    kh = k.reshape(b, t, KV_HEADS, HEAD_DIM).transpose(0, 2, 1, 3)
    vt = v.reshape(b, t, KV_HEADS, HEAD_DIM).transpose(0, 2, 3, 1)
    attn_lat = _attention(qt, kh, vt, n_lat, n_ctx, t, ATTN_Q_TILE)
    attn_ctx = None if last else _attention(qt, kh, vt, n_ctx, 0, n_ctx, ATTN_Q_TILE)

    conv_w = jnp.concatenate([p["ssd_conv_w"], p["rg_conv_w"]], axis=1)
    conv_b = jnp.concatenate([p["ssd_conv_b"], p["rg_conv_b"]])[None]
    xbc, u = _conv(xu, conv_w, conv_b, p["ssd_conv_w"].shape[1])
    lane_pad = lambda a: jnp.pad(a, ((0, 0), (0, LANES - a.shape[1])))[:, None, :]
    ys = _ssd(xbc, dt, lane_pad(p["ssd_dt_bias"]), lane_pad(-jnp.exp(p["ssd_a_log"])),
              jnp.repeat(p["ssd_d"], SSD_HEAD_DIM)[None])
    wgate = jnp.stack([jnp.concatenate([_block_diag(p["rg_wa"][j]), _block_diag(p["rg_wx"][j])], axis=1)
                       for j in range(2)]).astype(BF16)
    bgate = jnp.concatenate([p["rg_ba"], p["rg_bx"]], axis=1)[:, None, :]
    hr = _rglru(u, wgate, bgate, p["rg_lambda"][:, None, :])

    off = 1 if last else 0
    wr = jnp.pad(p["w_router"], ((0, 0), (0, LANES - N_EXPERTS))).astype(BF16)
    x1, hp, aff, aff_t = _merge(attn_ctx, attn_lat, ys, z, hr, g, xt, modl,
                                p["attn_out_norm"][None], p["ssd_norm"][None], p["rg_out_norm"][None],
                                p["w_out"].astype(BF16), p["ln1_w"][None], p["ln1_b"][None], wr, ctx_row, off)

    if last:
        idx = _topk(aff_t, EC_CAPACITY * n_lat // N_EXPERTS)
    else:
        idx_lat = _topk(aff_t[:, :, n_ctx:], EC_CAPACITY * n_lat // N_EXPERTS) + n_ctx
        idx_ctx = _topk(aff_t[:, :, :n_ctx], EC_CAPACITY * n_ctx // N_EXPERTS)
        idx = jnp.concatenate([idx_lat, idx_ctx], axis=-1)
    moe = _moe(idx, hp, aff, *experts, layer, MOE_FF_TILE)
    return _ln2(x1, moe, modl, p["ln2_w"][None], p["ln2_b"][None], ctx_row, off)


def kernel(x, c, ctx, c_ctx, w_mod, b_mod, w_in, q_norm, k_norm, attn_out_norm, ssd_conv_w, ssd_conv_b, ssd_dt_bias, ssd_a_log, ssd_d, ssd_norm, rg_conv_w, rg_conv_b, rg_wa, rg_ba, rg_wx, rg_bx, rg_lambda, rg_out_norm, w_out, ln1_w, ln1_b, w_router, w_gate, w_up, w_down, ln2_w, ln2_b):
    b, n_lat, d = x.shape
    n_ctx = ctx.shape[1]
    assert n_ctx == SEQ_TILE and n_lat % 512 == 0 and b == SUBLANES
    params = dict(w_in=w_in, q_norm=q_norm, k_norm=k_norm, attn_out_norm=attn_out_norm, ssd_conv_w=ssd_conv_w,
                  ssd_conv_b=ssd_conv_b, ssd_dt_bias=ssd_dt_bias, ssd_a_log=ssd_a_log, ssd_d=ssd_d,
                  ssd_norm=ssd_norm, rg_conv_w=rg_conv_w, rg_conv_b=rg_conv_b, rg_wa=rg_wa, rg_ba=rg_ba,
                  rg_wx=rg_wx, rg_bx=rg_bx, rg_lambda=rg_lambda, rg_out_norm=rg_out_norm, w_out=w_out,
                  ln1_w=ln1_w, ln1_b=ln1_b, w_router=w_router, ln2_w=ln2_w, ln2_b=ln2_b)
    experts = (w_gate.astype(BF16), w_up.astype(BF16), w_down.astype(BF16))
    mod = _modulation(c, c_ctx, w_mod, b_mod)
    rope = _rope_tables(n_ctx, n_lat)
    xt = jnp.concatenate([ctx, x], axis=1)
    depth = w_mod.shape[0]
    for l in range(depth):
        p = {name: val[l] for name, val in params.items()}
        modl = mod[l].reshape(mod.shape[1], 1, mod.shape[2])
        xt = _layer(xt, modl, rope, p, experts, l, l == depth - 1, n_ctx, b)
    return xt
```

```python
import functools
import math

import jax
import jax.numpy as jnp
from jax import lax
from jax.experimental import pallas as pl
from jax.experimental.pallas import tpu as pltpu

F32 = jnp.float32
BF16 = jnp.bfloat16
I32 = jnp.int32

DEPTH = 2
GRID_W = 64
HEAD_DIM = 64
KV_HEADS = 2
Q_PER_KV = 4
AXIS_DIM = HEAD_DIM // 2
ROPE_THETA = 10000.0
SSD_HEADS = 4
SSD_HEAD_DIM = 64
SSD_STATE = 64
RG_BLOCKS = 4
RG_C = 8.0
CONV_W = 4
N_EXPERTS = 16
EC_CAPACITY = 2
EPS = 1e-6
DEEPNORM_ALPHA = (2 * DEPTH) ** 0.25

LANES = 128
SUBLANES = 8
BF16_ROWS = 16
SEQ_TILE = 256
ATTN_Q_TILE = 256
ATTN_KEY_CHUNK = 256
SSD_BATCH = 4
MOE_FF_TILE = 1024
MOE_FF_SLICE = 256
MOE_ROW_GROUP = 4
LOG2E = math.log2(math.e)
VMEM_LIMIT = 48 * 1024 * 1024
VMEM_LIMIT_MOE = 60 * 1024 * 1024


def _cparams(limit=VMEM_LIMIT):
    return pltpu.CompilerParams(vmem_limit_bytes=limit)


def _sigmoid(x):
    return 1.0 / (1.0 + jnp.exp(-x))


def _silu(x):
    return x * _sigmoid(x)


def _softplus(x):
    return jnp.maximum(x, 0.0) + jnp.log1p(jnp.exp(-jnp.abs(x)))


def _bdot(a, b):
    return jnp.dot(a.astype(BF16), b.astype(BF16), preferred_element_type=F32)


def _mod_body(c_ref, w_ref, b_ref, o_ref):
    c = c_ref[...]
    o_ref[0] = _bdot(_silu(c), w_ref[0]) + b_ref[0]


def _modulation(c, c_ctx, w_mod, b_mod):
    depth, d, n6 = w_mod.shape
    b = c.shape[0]
    rows = 2 * SUBLANES
    cc = jnp.zeros((rows, d), F32).at[:b].set(c).at[b].set(c_ctx)
    tn = 1536
    return pl.pallas_call(
        _mod_body,
        grid=(depth, n6 // tn),
        in_specs=[
            pl.BlockSpec((rows, d), lambda l, j: (0, 0)),
            pl.BlockSpec((1, d, tn), lambda l, j: (l, 0, j)),
            pl.BlockSpec((1, 1, tn), lambda l, j: (l, 0, j)),
        ],
        out_specs=pl.BlockSpec((1, rows, tn), lambda l, j: (l, 0, j)),
        out_shape=jax.ShapeDtypeStruct((depth, rows, n6), F32),
        compiler_params=_cparams(),
        name="adaln_mod",
    )(cc, w_mod, b_mod.reshape(depth, 1, n6))


def _mod_spec(j, d, ctx_row, off):
    if off == 0:
        return pl.BlockSpec((1, 1, d), lambda b, i: (jnp.where(i == 0, ctx_row, b), 0, j))
    return pl.BlockSpec((1, 1, d), lambda b, i: (b, 0, j))


W_Q, W_K, W_V, W_XU, W_Z, W_G, W_DT = 512, 128, 128, 768, 256, 256, 256
IN_PAD = W_Q + W_K + W_V + W_XU + W_Z + W_G + W_DT


def _rope(t, cos, sin_signed):
    rows = t.shape[0]
    lane = lax.broadcasted_iota(I32, (rows, LANES), 1)
    first = (lane % AXIS_DIM) < (AXIS_DIM // 2)
    outs = []
    for c in range(t.shape[1] // LANES):
        tc = t[:, c * LANES:(c + 1) * LANES]
        partner = jnp.where(first, pltpu.roll(tc, LANES - AXIS_DIM // 2, 1), pltpu.roll(tc, AXIS_DIM // 2, 1))
        outs.append(tc * cos + partner * sin_signed)
    return outs[0] if len(outs) == 1 else jnp.concatenate(outs, axis=1)


def _inproj_body(x_ref, sc_ref, sh_ref, w_ref, cos_ref, sin_ref, qw_ref, kw_ref, oq_ref, ok_ref,
                 q_ref, k_ref, v_ref, xu_ref, z_ref, g_ref, dt_ref):
    h = x_ref[0] * (1.0 + sc_ref[0]) + sh_ref[0]
    y = jnp.dot(h.astype(BF16), w_ref[...], preferred_element_type=F32)
    o = 0
    q = y[:, o:o + W_Q]; o += W_Q
    k = y[:, o:o + W_K]; o += W_K
    v = y[:, o:o + W_V]; o += W_V
    xu_ref[0] = y[:, o:o + W_XU]; o += W_XU
    z_ref[0] = y[:, o:o + W_Z]; o += W_Z
    g_ref[0] = y[:, o:o + W_G]; o += W_G
    dt_ref[0] = y[:, o:o + W_DT]
    cos = cos_ref[...]
    sin = sin_ref[...]
    ssq = jnp.dot((q * q).astype(BF16), oq_ref[...], preferred_element_type=F32)
    qn = q * lax.rsqrt(ssq * (1.0 / HEAD_DIM) + EPS) * qw_ref[...]
    q_ref[0] = (_rope(qn, cos, sin) * (HEAD_DIM ** -0.5 * LOG2E)).T.astype(BF16)
    ssk = jnp.dot((k * k).astype(BF16), ok_ref[...], preferred_element_type=F32)
    kn = k * lax.rsqrt(ssk * (1.0 / HEAD_DIM) + EPS) * kw_ref[...]
    k_ref[0] = _rope(kn, cos, sin).astype(BF16)
    v_ref[0] = v.astype(BF16)


def _inproj(xt, modl, wcat, cos_t, sin_t, qw, kw, ones_q, ones_k, ctx_row):
    b, t, d = xt.shape
    tl = SEQ_TILE
    grid = (b, t // tl)
    full = lambda shape: pl.BlockSpec(shape, lambda bb, i: (0,) * len(shape))
    seq = lambda w: pl.BlockSpec((1, tl, w), lambda bb, i: (bb, i, 0))
    outs = [(W_K, BF16), (W_V, BF16), (W_XU, F32), (W_Z, F32), (W_G, F32), (W_DT, F32)]
    return pl.pallas_call(
        _inproj_body,
        grid=grid,
        in_specs=[
            seq(d),
            _mod_spec(1, d, ctx_row, 0),
            _mod_spec(0, d, ctx_row, 0),
            full((d, IN_PAD)),
            pl.BlockSpec((tl, LANES), lambda bb, i: (i, 0)),
            pl.BlockSpec((tl, LANES), lambda bb, i: (i, 0)),
            full((1, W_Q)), full((1, W_K)), full((W_Q, W_Q)), full((W_K, W_K)),
        ],
        out_specs=[pl.BlockSpec((1, W_Q, tl), lambda bb, i: (bb, 0, i))] + [seq(w) for w, _ in outs],
        out_shape=[jax.ShapeDtypeStruct((b, W_Q, t), BF16)]
        + [jax.ShapeDtypeStruct((b, t, w), dt) for w, dt in outs],
        compiler_params=_cparams(),
        name="in_proj",
    )(xt, modl, modl, wcat, cos_t, sin_t, qw, kw, ones_q, ones_k)


def _conv_body(x_ref, prev_ref, next_ref, w_ref, b_ref, xbc_ref, u_ref, *, n_xbc):
    i = pl.program_id(1)
    n = pl.num_programs(1)
    x = x_ref[0]
    tl = x.shape[0]
    has_prev = i > 1
    has_next = (i > 0) & (i < n - 1)
    pm = jnp.where(has_prev, prev_ref[0, SUBLANES - 1:SUBLANES, :], 0.0)
    n0 = jnp.where(has_next, next_ref[0, 0:1, :], 0.0)
    n1 = jnp.where(has_next, next_ref[0, 1:2, :], 0.0)
    row = lax.broadcasted_iota(I32, x.shape, 0)
    xm1 = jnp.where(row == 0, pm, pltpu.roll(x, 1, 0))
    xp1 = jnp.where(row == tl - 1, n0, pltpu.roll(x, tl - 1, 0))
    xp2 = jnp.where(row == tl - 1, n1, jnp.where(row == tl - 2, n0, pltpu.roll(x, tl - 2, 0)))
    w = w_ref[...]
    y = xm1 * w[0:1] + x * w[1:2] + xp1 * w[2:3] + xp2 * w[3:4] + b_ref[...]
    xbc = y[:, :n_xbc]
    xbc_ref[0] = _silu(xbc)
    u_ref[0] = y[:, n_xbc:]


def _conv(xu, w, bias, n_xbc):
    b, t, c = xu.shape
    tl = SEQ_TILE
    nt = t // tl
    r = tl // SUBLANES
    nb = t // SUBLANES
    return pl.pallas_call(
        functools.partial(_conv_body, n_xbc=n_xbc),
        grid=(b, nt),
        in_specs=[
            pl.BlockSpec((1, tl, c), lambda bb, i: (bb, i, 0)),
            pl.BlockSpec((1, SUBLANES, c), lambda bb, i: (bb, jnp.maximum(i * r - 1, 0), 0)),
            pl.BlockSpec((1, SUBLANES, c), lambda bb, i: (bb, jnp.minimum((i + 1) * r, nb - 1), 0)),
            pl.BlockSpec((CONV_W, c), lambda bb, i: (0, 0)),
            pl.BlockSpec((1, c), lambda bb, i: (0, 0)),
        ],
        out_specs=[
            pl.BlockSpec((1, tl, n_xbc), lambda bb, i: (bb, i, 0)),
            pl.BlockSpec((1, tl, c - n_xbc), lambda bb, i: (bb, i, 0)),
        ],
        out_shape=[jax.ShapeDtypeStruct((b, t, n_xbc), F32), jax.ShapeDtypeStruct((b, t, c - n_xbc), F32)],
        compiler_params=_cparams(),
        name="dwconv",
    )(xu, xu, xu, w, bias)


def _attn_body(qt_ref, k_ref, vt_ref, o_ref, s_buf, p_buf, acc_ref, *, n_keys, kc):
    nh, hd, tq = qt_ref.shape[1:]
    qt = jnp.concatenate([qt_ref[0, h] for h in range(nh)], axis=1)
    cols = nh * tq
    rem = n_keys % kc
    chunks = ([(0, rem)] if rem else []) + [(rem + kc * i, kc) for i in range(n_keys // kc)]
    n = len(chunks)

    def scores(c):
        s0, sz = chunks[c]
        s_buf[c % 2, :sz, :] = jnp.dot(k_ref[0, 0, s0:s0 + sz, :], qt, preferred_element_type=F32)

    def softmax(c, m):
        sz = chunks[c][1]
        s = s_buf[c % 2, :sz, :]
        m_new = jnp.maximum(m, jnp.max(s, axis=0, keepdims=True))
        p_buf[c % 2, :sz, :] = jnp.exp2(s - m_new).astype(BF16)
        return m_new, jnp.exp2(m - m_new)

    def weighted_values(c, alpha):
        s0, sz = chunks[c]
        acc_ref[...] = alpha * acc_ref[...] + jnp.dot(vt_ref[0, 0, :, s0:s0 + sz], p_buf[c % 2, :sz, :],
                                                      preferred_element_type=F32)

    acc_ref[...] = jnp.zeros_like(acc_ref)
    m = jnp.full((1, cols), -jnp.inf, F32)
    alpha = None
    scores(0)
    for j in range(n + 1):
        if j + 1 < n:
            scores(j + 1)
        prev_alpha = alpha
        if j < n:
            m, alpha = softmax(j, m)
        if j >= 1:
            weighted_values(j - 1, prev_alpha)
    o = acc_ref[:hd, :] * (1.0 / acc_ref[hd:hd + 1, :])
    for h in range(nh):
        o_ref[0, h * hd:(h + 1) * hd, :] = o[:, h * tq:(h + 1) * tq]


def _attention(qt, k, vt, n_q, q_off, n_keys, tq):
    b = qt.shape[0]
    off = q_off // tq
    kc = min(ATTN_KEY_CHUNK, n_keys)
    cols = Q_PER_KV * tq
    vrows = vt.shape[2]
    return pl.pallas_call(
        functools.partial(_attn_body, n_keys=n_keys, kc=kc),
        grid=(b, KV_HEADS, n_q // tq),
        scratch_shapes=[pltpu.VMEM((2, kc, cols), F32), pltpu.VMEM((2, kc, cols), BF16),
                        pltpu.VMEM((vrows, cols), F32)],
        in_specs=[
            pl.BlockSpec((1, Q_PER_KV, HEAD_DIM, tq), lambda bb, g, i: (bb, g, 0, i + off)),
            pl.BlockSpec((1, 1, n_keys, HEAD_DIM), lambda bb, g, i: (bb, g, 0, 0)),
            pl.BlockSpec((1, 1, vrows, n_keys), lambda bb, g, i: (bb, g, 0, 0)),
        ],
        out_specs=pl.BlockSpec((1, Q_PER_KV * HEAD_DIM, tq), lambda bb, g, i: (bb, g, i)),
        out_shape=jax.ShapeDtypeStruct((b, KV_HEADS * Q_PER_KV * HEAD_DIM, n_q), F32),
        compiler_params=_cparams(),
        name="gqa_attention",
    )(qt, k, vt)


def _seq_order(d, c, nc):
    return jnp.where(c == 0, 0, jnp.where(d == 0, c, nc - c))


def _ssd_body(xf_ref, xb_ref, dtf_ref, dtb_ref, bias_ref, a_ref, dvec_ref, yf_ref, yb_ref, s_scr):
    @pl.when(pl.program_id(1) == 0)
    def _():
        s_scr[...] = jnp.zeros_like(s_scr)

    nb = xf_ref.shape[0]
    streams = range(2 * nb)
    xbc = [(xf_ref, xb_ref)[k % 2][k // 2] for k in streams]
    q = xbc[0].shape[0]
    xw = SSD_HEADS * SSD_HEAD_DIM
    gw = 2 * SSD_STATE
    x = [v[:, :xw] for v in xbc]
    bm = [v[:, xw:xw + gw] for v in xbc]
    cm = [v[:, xw + gw:xw + 2 * gw] for v in xbc]
    dt_raw = [(dtf_ref, dtb_ref)[k % 2][k // 2] for k in streams]
    dt = [_softplus(dt_raw[k] + bias_ref[k % 2]) for k in streams]
    da = [dt[k] * a_ref[k % 2] for k in streams]
    ii = lax.broadcasted_iota(I32, (q, q), 0)
    jj = lax.broadcasted_iota(I32, (q, q), 1)
    mask = [jj <= ii, jj >= ii]
    tm = [jnp.where(m_, 1.0, 0.0).astype(BF16) for m_ in mask]
    da_hi = [da[k].astype(BF16) for k in streams]
    da_lo = [(da[k] - da_hi[k].astype(F32)).astype(BF16) for k in streams]
    cs = [jnp.dot(tm[k % 2], da_hi[k], preferred_element_type=F32)
          + jnp.dot(tm[k % 2], da_lo[k], preferred_element_type=F32) for k in streams]
    tot = [cs[k][q - 1:q, :] if k % 2 == 0 else cs[k][0:1, :] for k in streams]
    cst = [cs[k].T for k in streams]
    dec = [jnp.exp(tot[k] - cs[k]) for k in streams]
    ecs = [jnp.exp(cs[k]) for k in streams]
    etot = [jnp.exp(tot[k]) for k in streams]
    ys = [[] for _ in streams]
    nt = (((1,), (1,)), ((), ()))
    tn = (((0,), (0,)), ((), ()))
    for g in range(2):
        gs = slice(g * SSD_STATE, (g + 1) * SSD_STATE)
        bg = [bm[k][:, gs].astype(BF16) for k in streams]
        cg = [cm[k][:, gs].astype(BF16) for k in streams]
        gmat = [lax.dot_general(cg[k], bg[k], nt, preferred_element_type=F32) for k in streams]
        for hh in range(SSD_HEADS // 2):
            h = 2 * g + hh
            hs = slice(h * SSD_HEAD_DIM, (h + 1) * SSD_HEAD_DIM)
            lmat = [jnp.exp(jnp.where(mask[k % 2], cs[k][:, h:h + 1] - cst[k][h:h + 1, :], -jnp.inf))
                    for k in streams]
            xdt = [x[k][:, hs] * dt[k][:, h:h + 1] for k in streams]
            y_diag = [_bdot(gmat[k] * lmat[k], xdt[k]) for k in streams]
            s_in = [s_scr[k, h] for k in streams]
            y_off = [lax.dot_general(cg[k], s_in[k].astype(BF16), nt, preferred_element_type=F32) * ecs[k][:, h:h + 1]
                     for k in streams]
            xd = [(xdt[k] * dec[k][:, h:h + 1]).astype(BF16) for k in streams]
            for k in streams:
                ys[k].append(y_diag[k] + y_off[k])
                s_scr[k, h] = etot[k][:, h:h + 1] * s_in[k] + lax.dot_general(xd[k], bg[k], tn,
                                                                               preferred_element_type=F32)
    for k in streams:
        if k % 2 == 0:
            yf_ref[k // 2] = jnp.concatenate(ys[k], axis=1) + x[k] * dvec_ref[...]
        else:
            yb_ref[k // 2] = jnp.concatenate(ys[k], axis=1)


def _ssd(xbc, dt, bias, a_neg, dvec):
    b, t, cw = xbc.shape
    q = SEQ_TILE
    nc = t // q
    xw = SSD_HEADS * SSD_HEAD_DIM
    fwd = lambda c: c
    bwd = lambda c: _seq_order(1, c, nc)
    nb = SSD_BATCH
    return pl.pallas_call(
        _ssd_body,
        grid=(b // nb, nc),
        in_specs=[
            pl.BlockSpec((nb, q, cw), lambda bb, c: (bb, fwd(c), 0)),
            pl.BlockSpec((nb, q, cw), lambda bb, c: (bb, bwd(c), 0)),
            pl.BlockSpec((nb, q, LANES), lambda bb, c: (bb, fwd(c), 0)),
            pl.BlockSpec((nb, q, LANES), lambda bb, c: (bb, bwd(c), 1)),
            pl.BlockSpec((2, 1, LANES), lambda bb, c: (0, 0, 0)),
            pl.BlockSpec((2, 1, LANES), lambda bb, c: (0, 0, 0)),
            pl.BlockSpec((1, xw), lambda bb, c: (0, 0)),
        ],
        out_specs=[pl.BlockSpec((nb, q, xw), lambda bb, c: (bb, fwd(c), 0)),
                   pl.BlockSpec((nb, q, xw), lambda bb, c: (bb, bwd(c), 0))],
        out_shape=[jax.ShapeDtypeStruct((b, t, xw), F32)] * 2,
        scratch_shapes=[pltpu.VMEM((2 * nb, SSD_HEADS, SSD_HEAD_DIM, SSD_STATE), F32)],
        compiler_params=_cparams(),
        name="ssd_scan",
    )(xbc, xbc, dt, dt, bias, a_neg, dvec)


def _rg_body(u_ref, w_ref, bias_ref, lam_ref, y_ref, a_s, v_s, o_s, h_s, *, pitch):
    d = pl.program_id(0)
    c = pl.program_id(1)
    nb, tl, width = u_ref.shape
    ng = width // LANES

    @pl.when(c == 0)
    def _():
        h_s[...] = jnp.zeros_like(h_s)

    sp = _softplus(-lam_ref[0])
    w = w_ref[0]
    bias = bias_ref[0]
    for b in range(nb):
        ub = u_ref[b]
        pre = jnp.dot(ub.astype(BF16), w, preferred_element_type=F32) + bias
        r = _sigmoid(pre[:, :width])
        ig = _sigmoid(pre[:, width:])
        a = jnp.exp((-RG_C) * r * sp)
        v = jnp.sqrt(1.0 - a * a) * ig * ub
        for j in range(ng):
            a_s[j, pl.ds(b * pitch, tl), :] = a[:, j * LANES:(j + 1) * LANES]
            v_s[j, pl.ds(b * pitch, tl), :] = v[:, j * LANES:(j + 1) * LANES]

    def step(t, hs):
        te = jnp.where(d == 0, t, tl - 1 - t)
        out = []
        for j in range(ng):
            at = a_s[j, pl.ds(te, nb, stride=pitch), :]
            vt = v_s[j, pl.ds(te, nb, stride=pitch), :]
            hj = at * hs[j] + vt
            o_s[j, pl.ds(te, nb, stride=pitch), :] = hj
            out.append(hj)
        return tuple(out)

    h0 = tuple(h_s[:, j * LANES:(j + 1) * LANES] for j in range(ng))
    hf = lax.fori_loop(0, tl, step, h0, unroll=8)
    for j in range(ng):
        h_s[:, j * LANES:(j + 1) * LANES] = hf[j]
    for b in range(nb):
        y_ref[0, b] = jnp.concatenate([o_s[j, pl.ds(b * pitch, tl), :] for j in range(ng)], axis=1)


def _rglru(u, wg, bias, lam):
    b, t, width = u.shape
    assert b == SUBLANES, "the recurrence keeps one sample per sublane"
    tl = SEQ_TILE
    nc = t // tl
    pitch = tl + SUBLANES
    ng = width // LANES
    slab = pltpu.VMEM((ng, b * pitch, LANES), F32)
    return pl.pallas_call(
        functools.partial(_rg_body, pitch=pitch),
        grid=(2, nc),
        in_specs=[
            pl.BlockSpec((b, tl, width), lambda d, c: (0, _seq_order(d, c, nc), 0)),
            pl.BlockSpec((1, width, 2 * width), lambda d, c: (d, 0, 0)),
            pl.BlockSpec((1, 1, 2 * width), lambda d, c: (d, 0, 0)),
            pl.BlockSpec((1, 1, width), lambda d, c: (d, 0, 0)),
        ],
        out_specs=pl.BlockSpec((1, b, tl, width), lambda d, c: (d, 0, _seq_order(d, c, nc), 0)),
        out_shape=jax.ShapeDtypeStruct((2, b, t, width), F32),
        scratch_shapes=[slab, slab, slab, pltpu.VMEM((b, width), F32)],
        compiler_params=_cparams(),
        name="rglru_scan",
    )(u, wg, bias, lam)


def _rms(x, w):
    return x * lax.rsqrt(jnp.mean(x * x, axis=-1, keepdims=True) + EPS) * w


def _layer_norm(t, w, b):
    mu = jnp.mean(t, axis=-1, keepdims=True)
    tc = t - mu
    var = jnp.mean(tc * tc, axis=-1, keepdims=True)
    return tc * lax.rsqrt(var + EPS) * w + b


def _gelu_tanh(x):
    return 0.5 * x * (1.0 + jnp.tanh(math.sqrt(2.0 / math.pi) * (x + 0.044715 * (x * x * x))))


def _pack_bf16_pair(lo, hi):
    lb = pltpu.bitcast(lo.astype(BF16).astype(F32), jnp.uint32)
    hb = pltpu.bitcast(hi.astype(BF16).astype(F32), jnp.uint32)
    return (lb >> 16) | (hb & jnp.uint32(0xFFFF0000))


def _merge_body(*refs, has_ctx):
    if has_ctx:
        actx_ref, refs = refs[0], refs[1:]
    (alat_ref, ysf_ref, ysb_ref, z_ref, hr_ref, g_ref, x_ref, g1_ref, sc2_ref, sh2_ref, aw_ref, sw_ref, rw_ref,
     wo_ref, lnw_ref, lnb_ref, wr_ref, x1_ref, hp_ref, aff_ref, afft_ref) = refs
    a = alat_ref[0]
    if has_ctx:
        a = jnp.where(pl.program_id(1) == 0, actx_ref[0], a)
    an = _rms(a.T, aw_ref[...])
    sn = _rms((ysf_ref[0] + ysb_ref[0]) * _silu(z_ref[0]), sw_ref[...])
    rn = _rms((hr_ref[0, 0] + hr_ref[1, 0]) * _gelu_tanh(g_ref[0]), rw_ref[...])
    cat = jnp.concatenate([an, sn, rn], axis=1).astype(BF16)
    proj = jnp.dot(cat, wo_ref[...], preferred_element_type=F32)
    x1 = _layer_norm(DEEPNORM_ALPHA * x_ref[0] + g1_ref[0] * proj, lnw_ref[...], lnb_ref[...])
    x1_ref[0] = x1
    h2 = x1 * (1.0 + sc2_ref[0]) + sh2_ref[0]
    half = h2.shape[1] // 2
    hp_ref[0] = _pack_bf16_pair(h2[:, :half], h2[:, half:])
    logits = jnp.dot(h2.astype(BF16), wr_ref[...], preferred_element_type=F32)
    lane = lax.broadcasted_iota(I32, logits.shape, 1)
    logits = jnp.where(lane < N_EXPERTS, logits, -jnp.inf)
    e = jnp.exp(logits - jnp.max(logits, axis=-1, keepdims=True))
    aff = e / jnp.sum(e, axis=-1, keepdims=True)
    aff_ref[0] = aff
    afft_ref[0] = aff.T[:N_EXPERTS, :]


def _merge(attn_ctx, attn_lat, ys, z, hr, g, xt, modl, aw, sw, rw, wo, lnw, lnb, wr, ctx_row, off):
    b, t, d = xt.shape
    tl = SEQ_TILE
    nt = t // tl - off
    has_ctx = off == 0
    aw_ = attn_lat.shape[1]
    seq = lambda w: pl.BlockSpec((1, tl, w), lambda bb, i: (bb, i + off, 0))
    pair = lambda w: pl.BlockSpec((2, 1, tl, w), lambda bb, i: (0, bb, i + off, 0))
    full = lambda shape: pl.BlockSpec(shape, lambda bb, i: (0,) * len(shape))
    out = lambda w: pl.BlockSpec((1, tl, w), lambda bb, i: (bb, i, 0))
    lat_off = 1 - off
    in_specs = [
        pl.BlockSpec((1, aw_, tl), lambda bb, i: (bb, 0, jnp.maximum(i - lat_off, 0))),
        seq(ys[0].shape[-1]), seq(ys[1].shape[-1]), seq(z.shape[-1]), pair(hr.shape[-1]), seq(g.shape[-1]), seq(d),
        _mod_spec(2, d, ctx_row, off), _mod_spec(4, d, ctx_row, off), _mod_spec(3, d, ctx_row, off),
        full(aw.shape), full(sw.shape), full(rw.shape), full(wo.shape), full(lnw.shape), full(lnb.shape),
        full(wr.shape),
    ]
    args = [attn_lat, ys[0], ys[1], z, hr, g, xt, modl, modl, modl, aw, sw, rw, wo, lnw, lnb, wr]
    if has_ctx:
        in_specs = [pl.BlockSpec((1, aw_, tl), lambda bb, i: (bb, 0, 0))] + in_specs
        args = [attn_ctx] + args
    rows = nt * tl
    return pl.pallas_call(
        functools.partial(_merge_body, has_ctx=has_ctx),
        grid=(b, nt),
        in_specs=in_specs,
        out_specs=[out(d), out(d // 2), out(LANES), pl.BlockSpec((1, N_EXPERTS, tl), lambda bb, i: (bb, 0, i))],
        out_shape=[
            jax.ShapeDtypeStruct((b, rows, d), F32),
            jax.ShapeDtypeStruct((b, rows, d // 2), jnp.uint32),
            jax.ShapeDtypeStruct((b, rows, LANES), F32),
            jax.ShapeDtypeStruct((b, N_EXPERTS, rows), F32),
        ],
        compiler_params=_cparams(),
        name="merge_outproj_ln1_router",
    )(*args)


def _topk_body(aff_ref, idx_ref, cum_scr, *, cap):
    aff = aff_ref[0]
    ne, n = aff.shape
    bits = pltpu.bitcast(aff, I32)

    def search(i, thr):
        cand = thr | lax.shift_left(jnp.int32(1), 30 - i)
        cnt = jnp.sum(jnp.where(bits >= cand, 1.0, 0.0), axis=1, keepdims=True)
        return jnp.where(cnt >= cap, cand, thr)

    thr = lax.fori_loop(0, 31, search, jnp.zeros((ne, 1), I32))
    gt = jnp.where(bits > thr, 1.0, 0.0)
    eq = jnp.where(bits == thr, 1.0, 0.0)
    need = cap - jnp.sum(gt, axis=1, keepdims=True)
    nblk = n // LANES
    r_ = lax.broadcasted_iota(I32, (LANES, LANES), 0)
    c_ = lax.broadcasted_iota(I32, (LANES, LANES), 1)
    upper = jnp.where(r_ <= c_, 1.0, 0.0).astype(BF16)

    def prefix(blocks):
        outs = []
        off = jnp.zeros((ne, 1), F32)
        for mk in blocks:
            w = jnp.dot(mk.astype(BF16), upper, preferred_element_type=F32) + off
            outs.append(w)
            off = w[:, LANES - 1:LANES]
        return outs

    blk = lambda a, k: a[:, k * LANES:(k + 1) * LANES]
    tie_rank = prefix([blk(eq, k) for k in range(nblk)])
    cum = prefix([jnp.maximum(blk(gt, k), blk(eq, k) * jnp.where(tie_rank[k] <= need, 1.0, 0.0))
                  for k in range(nblk)])
    for k in range(nblk):
        cum_scr[k] = cum[k]
    rows = -(-cap // LANES) * LANES
    slot = lax.broadcasted_iota(I32, (rows, LANES), 0).astype(F32)

    def per_expert(e, carry):
        acc = jnp.zeros((rows, LANES), F32)
        for k in range(nblk):
            acc = acc + jnp.where(cum_scr[k, pl.ds(e, 1), :] <= slot, 1.0, 0.0)
        idx_ref[0, pl.ds(e, 1), :] = jnp.sum(acc.T, axis=0, keepdims=True)[:, :cap].astype(I32)
        return carry

    lax.fori_loop(0, ne, per_expert, 0)


def _topk(aff_t, cap):
    b, e, n = aff_t.shape
    return pl.pallas_call(
        functools.partial(_topk_body, cap=cap),
        grid=(b,),
        in_specs=[pl.BlockSpec((1, e, n), lambda bb: (bb, 0, 0))],
        out_specs=pl.BlockSpec((1, e, cap), lambda bb: (bb, 0, 0)),
        out_shape=jax.ShapeDtypeStruct((b, e, cap), I32),
        scratch_shapes=[pltpu.VMEM((n // LANES, e, LANES), F32)],
        compiler_params=_cparams(),
        name="expert_choice_topk",
    )(aff_t)


def _moe_body(idxp_ref, idxc_ref, idxn_ref, hp_ref, aff_ref, wg_ref, wu_ref, wd_ref, out_ref,
              xs_scr, ag_scr, y_scr, xlo_scr, xhi_scr, hid_scr, *, cap):
    e = pl.program_id(1)
    f = pl.program_id(2)
    ne = pl.num_programs(1)
    nf = pl.num_programs(2)
    cur = e % 2
    oth = 1 - cur

    @pl.when((pl.program_id(0) == 0) & (e == 0) & (f == 0))
    def _():
        y_scr[...] = jnp.zeros_like(y_scr)

    @pl.when((e == 0) & (f == 0))
    def _():
        out_ref[...] = jnp.zeros_like(out_ref)

        def gather(j, carry):
            for u in range(SUBLANES):
                t = idxc_ref[0, 0, 0, j * SUBLANES + u]
                xs_scr[0, j, u:u + 1, :] = hp_ref[0, pl.ds(t, 1), :]
                ag_scr[0, j, u:u + 1, :] = aff_ref[0, pl.ds(t, 1), :]
            return carry

        lax.fori_loop(0, cap // SUBLANES, gather, 0)

    w = xs_scr[cur].reshape(cap, xs_scr.shape[-1])
    half = w.shape[1]
    xlo_scr[...] = pltpu.bitcast(w << 16, F32).astype(BF16)
    xhi_scr[...] = pltpu.bitcast(w & jnp.uint32(0xFFFF0000), F32).astype(BF16)
    lane = lax.broadcasted_iota(I32, (cap, LANES), 1)
    gate = jnp.sum(jnp.where(lane == e, ag_scr[cur].reshape(cap, LANES), 0.0), axis=1, keepdims=True)
    share = cap // nf
    base = f * share
    base_tile = f * (share // SUBLANES)
    has_prev = e > 0

    def row_copies(r0, r1):
        for g0 in range(r0, r1, MOE_ROW_GROUP):
            rows = range(g0, min(g0 + MOE_ROW_GROUP, r1))
            tps = [idxp_ref[0, 0, 0, base + r] for r in rows]
            sums = [out_ref[0, pl.ds(tp, 1), :]
                    + jnp.where(has_prev, y_scr[oth, base_tile + r // SUBLANES, r % SUBLANES:r % SUBLANES + 1, :], 0.0)
                    for tp, r in zip(tps, rows)]
            for tp, v in zip(tps, sums):
                out_ref[0, pl.ds(tp, 1), :] = v
            for r in rows:
                tn = idxn_ref[0, 0, 0, base + r]
                j, u = base_tile + r // SUBLANES, r % SUBLANES
                xs_scr[oth, j, u:u + 1, :] = hp_ref[0, pl.ds(tn, 1), :]
                ag_scr[oth, j, u:u + 1, :] = aff_ref[0, pl.ds(tn, 1), :]

    tf = wg_ref.shape[3]
    nsl = tf // MOE_FF_SLICE
    sl = lambda c: slice(c * MOE_FF_SLICE, (c + 1) * MOE_FF_SLICE)

    def hidden(c):
        xlo = xlo_scr[...]
        xhi = xhi_scr[...]
        hg = (jnp.dot(xlo, wg_ref[0, 0, :half, sl(c)], preferred_element_type=F32)
              + jnp.dot(xhi, wg_ref[0, 0, half:, sl(c)], preferred_element_type=F32))
        hu = (jnp.dot(xlo, wu_ref[0, 0, :half, sl(c)], preferred_element_type=F32)
              + jnp.dot(xhi, wu_ref[0, 0, half:, sl(c)], preferred_element_type=F32))
        hid_scr[c % 2] = (_silu(hg) * hu).astype(BF16)

    hidden(0)
    yp = None
    for c in range(nsl):
        if c + 1 < nsl:
            hidden(c + 1)
        part = jnp.dot(hid_scr[c % 2], wd_ref[0, 0, sl(c), :], preferred_element_type=F32)
        yp = part if yp is None else yp + part
        row_copies(share * c // nsl, share * (c + 1) // nsl)
    y_old = y_scr[cur].reshape(cap, y_scr.shape[-1])
    y_new = (jnp.where(f == 0, 0.0, y_old) + yp) * jnp.where(f == nf - 1, gate, 1.0)
    y_scr[cur] = y_new.reshape(y_scr.shape[1:])

    @pl.when((e == ne - 1) & (f == nf - 1))
    def _():
        def scatter(j, carry):
            for u in range(SUBLANES):
                t = idxc_ref[0, 0, 0, j * SUBLANES + u]
                out_ref[0, pl.ds(t, 1), :] = out_ref[0, pl.ds(t, 1), :] + y_scr[cur, j, u:u + 1, :]
            return carry

        lax.fori_loop(0, cap // SUBLANES, scatter, 0)


def _moe(idx, hp, aff, wg, wu, wd, layer, tf):
    b, t, half = hp.shape
    d = 2 * half
    ne, ff = wg.shape[1], wg.shape[3]
    cap = idx.shape[-1]
    nf = ff // tf
    assert nf >= 2 and cap % nf == 0
    one = pl.Buffered(1)
    idx4 = idx.reshape(b, ne, 1, cap)
    smem = lambda shift: pl.BlockSpec((1, 1, 1, cap), lambda bb, e, f: (bb, jnp.clip(e + shift, 0, ne - 1), 0, 0),
                                      memory_space=pltpu.SMEM)
    return pl.pallas_call(
        functools.partial(_moe_body, cap=cap),
        grid=(b, ne, nf),
        in_specs=[
            smem(-1), smem(0), smem(1),
            pl.BlockSpec((1, t, half), lambda bb, e, f: (bb, 0, 0), pipeline_mode=one),
            pl.BlockSpec((1, t, LANES), lambda bb, e, f: (bb, 0, 0), pipeline_mode=one),
            pl.BlockSpec((1, 1, d, tf), lambda bb, e, f: (layer, e, 0, f)),
            pl.BlockSpec((1, 1, d, tf), lambda bb, e, f: (layer, e, 0, f)),
            pl.BlockSpec((1, 1, tf, d), lambda bb, e, f: (layer, e, f, 0)),
        ],
        out_specs=pl.BlockSpec((1, t, d), lambda bb, e, f: (bb, 0, 0), pipeline_mode=one),
        out_shape=jax.ShapeDtypeStruct((b, t, d), F32),
        scratch_shapes=[
            pltpu.VMEM((2, cap // SUBLANES, SUBLANES, half), jnp.uint32),
            pltpu.VMEM((2, cap // SUBLANES, SUBLANES, LANES), F32),
            pltpu.VMEM((2, cap // SUBLANES, SUBLANES, d), F32),
            pltpu.VMEM((cap, half), BF16),
            pltpu.VMEM((cap, half), BF16),
            pltpu.VMEM((2, cap, MOE_FF_SLICE), BF16),
        ],
        compiler_params=_cparams(VMEM_LIMIT_MOE),
        name="expert_ffn",
    )(idx4, idx4, idx4, hp, aff, wg, wu, wd)


def _ln2_body(x1_ref, moe_ref, g2_ref, w_ref, b_ref, o_ref):
    o_ref[0] = _layer_norm(DEEPNORM_ALPHA * x1_ref[0] + g2_ref[0] * moe_ref[0], w_ref[...], b_ref[...])


def _ln2(x1, moe, modl, w, bias, ctx_row, off):
    b, t, d = x1.shape
    tl = SEQ_TILE
    seq = pl.BlockSpec((1, tl, d), lambda bb, i: (bb, i, 0))
    full = pl.BlockSpec((1, d), lambda bb, i: (0, 0))
    return pl.pallas_call(
        _ln2_body,
        grid=(b, t // tl),
        in_specs=[seq, seq, _mod_spec(5, d, ctx_row, off), full, full],
        out_specs=seq,
        out_shape=jax.ShapeDtypeStruct((b, t, d), F32),
        compiler_params=_cparams(),
        name="ln2",
    )(x1, moe, modl, w, bias)


def _block_diag_ones(width, block):
    r = jnp.arange(width)[:, None] // block
    c = jnp.arange(width)[None, :] // block
    return (r == c).astype(BF16)


def _block_diag(w):
    k, d, e = w.shape
    eye = jnp.eye(k, dtype=w.dtype)
    return (eye[:, None, :, None] * w[:, :, None, :]).reshape(k * d, k * e)


def _rope_tables(n_ctx, n_lat):
    pos = jnp.arange(n_lat)
    row = (pos // GRID_W).astype(F32)
    col = (pos % GRID_W).astype(F32)
    inv_freq = ROPE_THETA ** (-jnp.arange(0, AXIS_DIM, 2, dtype=F32) / AXIS_DIM)
    ang_r = row[:, None] * inv_freq
    ang_c = col[:, None] * inv_freq
    cos_h = jnp.concatenate([jnp.cos(ang_r)] * 2 + [jnp.cos(ang_c)] * 2, axis=1)
    sin_h = jnp.concatenate([-jnp.sin(ang_r), jnp.sin(ang_r), -jnp.sin(ang_c), jnp.sin(ang_c)], axis=1)
    reps = LANES // HEAD_DIM
    cos_t = jnp.concatenate([jnp.ones((n_ctx, LANES), F32), jnp.tile(cos_h, (1, reps))], axis=0)
    sin_t = jnp.concatenate([jnp.zeros((n_ctx, LANES), F32), jnp.tile(sin_h, (1, reps))], axis=0)
    return cos_t, sin_t


def _layer(xt, modl, rope, p, experts, layer, last, n_ctx, ctx_row):
    b, t, d = xt.shape
    n_lat = t - n_ctx
    cos_t, sin_t = rope
    w = p["w_in"]
    pad = jnp.zeros((d, LANES - SSD_HEADS), F32)
    wcat = jnp.concatenate(
        [w[:, 0:768], w[:, 768:1280], w[:, 1544:1800], w[:, 1280:1536], w[:, 1800:2056],
         w[:, 1536:1540], pad, w[:, 1540:1544], pad], axis=1).astype(BF16)
    qw = jnp.tile(p["q_norm"], W_Q // HEAD_DIM)[None]
    kw = jnp.tile(p["k_norm"], W_K // HEAD_DIM)[None]
    qt, k, v, xu, z, g, dt = _inproj(xt, modl, wcat, cos_t, sin_t, qw, kw,
                                     _block_diag_ones(W_Q, HEAD_DIM), _block_diag_ones(W_K, HEAD_DIM), ctx_row)

    qt = qt.reshape(b, W_Q // HEAD_DIM, HEAD_DIM, t)
    kh = k.reshape(b, t, KV_HEADS, HEAD_DIM).transpose(0, 2, 1, 3)
    vt = v.reshape(b, t, KV_HEADS, HEAD_DIM).transpose(0, 2, 3, 1)
    ones_pad = jnp.zeros((b, KV_HEADS, BF16_ROWS, t), BF16).at[:, :, 0, :].set(1.0)
    vt = jnp.concatenate([vt, ones_pad], axis=2)
    attn_lat = _attention(qt, kh, vt, n_lat, n_ctx, t, ATTN_Q_TILE)
    attn_ctx = None if last else _attention(qt, kh, vt, n_ctx, 0, n_ctx, ATTN_Q_TILE)

    conv_w = jnp.concatenate([p["ssd_conv_w"], p["rg_conv_w"]], axis=1)
    conv_b = jnp.concatenate([p["ssd_conv_b"], p["rg_conv_b"]])[None]
    xbc, u = _conv(xu, conv_w, conv_b, p["ssd_conv_w"].shape[1])
    lane_pad = lambda a: jnp.pad(a, ((0, 0), (0, LANES - a.shape[1])))[:, None, :]
    ys = _ssd(xbc, dt, lane_pad(p["ssd_dt_bias"]), lane_pad(-jnp.exp(p["ssd_a_log"])),
              jnp.repeat(p["ssd_d"], SSD_HEAD_DIM)[None])
    wgate = jnp.stack([jnp.concatenate([_block_diag(p["rg_wa"][j]), _block_diag(p["rg_wx"][j])], axis=1)
                       for j in range(2)]).astype(BF16)
    bgate = jnp.concatenate([p["rg_ba"], p["rg_bx"]], axis=1)[:, None, :]
    hr = _rglru(u, wgate, bgate, p["rg_lambda"][:, None, :])

    off = 1 if last else 0
    wr = jnp.pad(p["w_router"], ((0, 0), (0, LANES - N_EXPERTS))).astype(BF16)
    x1, hp, aff, aff_t = _merge(attn_ctx, attn_lat, ys, z, hr, g, xt, modl,
                                p["attn_out_norm"][None], p["ssd_norm"][None], p["rg_out_norm"][None],
                                p["w_out"].astype(BF16), p["ln1_w"][None], p["ln1_b"][None], wr, ctx_row, off)

    if last:
        idx = _topk(aff_t, EC_CAPACITY * n_lat // N_EXPERTS)
    else:
        idx_lat = _topk(aff_t[:, :, n_ctx:], EC_CAPACITY * n_lat // N_EXPERTS) + n_ctx
        idx_ctx = _topk(aff_t[:, :, :n_ctx], EC_CAPACITY * n_ctx // N_EXPERTS)
        idx = jnp.concatenate([idx_lat, idx_ctx], axis=-1)
    moe = _moe(idx, hp, aff, *experts, layer, MOE_FF_TILE)
    return _ln2(x1, moe, modl, p["ln2_w"][None], p["ln2_b"][None], ctx_row, off)


def kernel(x, c, ctx, c_ctx, w_mod, b_mod, w_in, q_norm, k_norm, attn_out_norm, ssd_conv_w, ssd_conv_b, ssd_dt_bias, ssd_a_log, ssd_d, ssd_norm, rg_conv_w, rg_conv_b, rg_wa, rg_ba, rg_wx, rg_bx, rg_lambda, rg_out_norm, w_out, ln1_w, ln1_b, w_router, w_gate, w_up, w_down, ln2_w, ln2_b):
    b, n_lat, d = x.shape
    n_ctx = ctx.shape[1]
    assert n_ctx == SEQ_TILE and n_lat % 512 == 0 and b == SUBLANES
    params = dict(w_in=w_in, q_norm=q_norm, k_norm=k_norm, attn_out_norm=attn_out_norm, ssd_conv_w=ssd_conv_w,
                  ssd_conv_b=ssd_conv_b, ssd_dt_bias=ssd_dt_bias, ssd_a_log=ssd_a_log, ssd_d=ssd_d,
                  ssd_norm=ssd_norm, rg_conv_w=rg_conv_w, rg_conv_b=rg_conv_b, rg_wa=rg_wa, rg_ba=rg_ba,
                  rg_wx=rg_wx, rg_bx=rg_bx, rg_lambda=rg_lambda, rg_out_norm=rg_out_norm, w_out=w_out,
                  ln1_w=ln1_w, ln1_b=ln1_b, w_router=w_router, ln2_w=ln2_w, ln2_b=ln2_b)
    experts = (w_gate.astype(BF16), w_up.astype(BF16), w_down.astype(BF16))
    mod = _modulation(c, c_ctx, w_mod, b_mod)
    rope = _rope_tables(n_ctx, n_lat)
    xt = jnp.concatenate([ctx, x], axis=1)
    depth = w_mod.shape[0]
    for l in range(depth):
        p = {name: val[l] for name, val in params.items()}
        modl = mod[l].reshape(mod.shape[1], 1, mod.shape[2])
        xt = _layer(xt, modl, rope, p, experts, l, l == depth - 1, n_ctx, b)
    return xt
```

```python
import functools
import math

import jax
import jax.numpy as jnp
from jax import lax
from jax.experimental import pallas as pl
from jax.experimental.pallas import tpu as pltpu

F32 = jnp.float32
BF16 = jnp.bfloat16
I32 = jnp.int32

DEPTH = 2
GRID_W = 64
HEAD_DIM = 64
KV_HEADS = 2
Q_PER_KV = 4
AXIS_DIM = HEAD_DIM // 2
ROPE_THETA = 10000.0
SSD_HEADS = 4
SSD_HEAD_DIM = 64
SSD_STATE = 64
RG_BLOCKS = 4
RG_C = 8.0
CONV_W = 4
N_EXPERTS = 16
EC_CAPACITY = 2
EPS = 1e-6
DEEPNORM_ALPHA = (2 * DEPTH) ** 0.25

LANES = 128
SUBLANES = 8
BF16_ROWS = 16
SEQ_TILE = 256
ATTN_Q_TILE = 512
ATTN_KEY_CHUNK = 256
MERGE_ROW_SPLIT = 2
SSD_BATCH = 4
MOE_FF_TILE = 1024
MOE_FF_SLICE = 256
MOE_ROW_GROUP = 4
LOG2E = math.log2(math.e)
VMEM_LIMIT = 48 * 1024 * 1024
VMEM_LIMIT_MOE = 60 * 1024 * 1024


def _cparams(limit=VMEM_LIMIT):
    return pltpu.CompilerParams(vmem_limit_bytes=limit)


def _sigmoid(x):
    return 1.0 / (1.0 + jnp.exp(-x))


def _silu(x):
    return x * _sigmoid(x)


def _softplus(x):
    return jnp.maximum(x, 0.0) + jnp.log1p(jnp.exp(-jnp.abs(x)))


def _bdot(a, b):
    return jnp.dot(a.astype(BF16), b.astype(BF16), preferred_element_type=F32)


def _mod_body(c_ref, w_ref, b_ref, o_ref):
    c = c_ref[...]
    o_ref[0] = _bdot(_silu(c), w_ref[0]) + b_ref[0]


def _modulation(c, c_ctx, w_mod, b_mod):
    depth, d, n6 = w_mod.shape
    b = c.shape[0]
    rows = 2 * SUBLANES
    cc = jnp.zeros((rows, d), F32).at[:b].set(c).at[b].set(c_ctx)
    tn = 1536
    return pl.pallas_call(
        _mod_body,
        grid=(depth, n6 // tn),
        in_specs=[
            pl.BlockSpec((rows, d), lambda l, j: (0, 0)),
            pl.BlockSpec((1, d, tn), lambda l, j: (l, 0, j)),
            pl.BlockSpec((1, 1, tn), lambda l, j: (l, 0, j)),
        ],
        out_specs=pl.BlockSpec((1, rows, tn), lambda l, j: (l, 0, j)),
        out_shape=jax.ShapeDtypeStruct((depth, rows, n6), F32),
        compiler_params=_cparams(),
        name="adaln_mod",
    )(cc, w_mod, b_mod.reshape(depth, 1, n6))


def _mod_spec(j, d, ctx_row, off):
    if off == 0:
        return pl.BlockSpec((1, 1, d), lambda b, i: (jnp.where(i == 0, ctx_row, b), 0, j))
    return pl.BlockSpec((1, 1, d), lambda b, i: (b, 0, j))


W_Q, W_K, W_V, W_XU, W_Z, W_G, W_DT = 512, 128, 128, 768, 256, 256, 256
IN_PAD = W_Q + W_K + W_V + W_XU + W_Z + W_G + W_DT


def _rope(t, cos, sin_signed):
    rows = t.shape[0]
    lane = lax.broadcasted_iota(I32, (rows, LANES), 1)
    first = (lane % AXIS_DIM) < (AXIS_DIM // 2)
    outs = []
    for c in range(t.shape[1] // LANES):
        tc = t[:, c * LANES:(c + 1) * LANES]
        partner = jnp.where(first, pltpu.roll(tc, LANES - AXIS_DIM // 2, 1), pltpu.roll(tc, AXIS_DIM // 2, 1))
        outs.append(tc * cos + partner * sin_signed)
    return outs[0] if len(outs) == 1 else jnp.concatenate(outs, axis=1)


def _inproj_body(x_ref, xp_ref, xn_ref, sc_ref, sh_ref, w_ref, cos_ref, sin_ref, qw_ref, kw_ref, oq_ref, ok_ref,
                 cw_ref, cb_ref, q_ref, k_ref, v_ref, xbc_ref, u_ref, z_ref, g_ref, dt_ref, qkv_scr):
    i = pl.program_id(1)
    n = pl.num_programs(1)
    scale = 1.0 + sc_ref[0]
    shift = sh_ref[0]
    h = (x_ref[0] * scale + shift).astype(BF16)
    n_qkv = W_Q + W_K + W_V
    qkv_scr[...] = jnp.dot(h, w_ref[:, :n_qkv], preferred_element_type=F32)
    rest = jnp.dot(h, w_ref[:, n_qkv:], preferred_element_type=F32)
    halo = (jnp.concatenate([xp_ref[0], xn_ref[0]], axis=0) * scale + shift).astype(BF16)
    xu_halo = jnp.dot(halo, w_ref[:, n_qkv:n_qkv + W_XU], preferred_element_type=F32)
    xu = rest[:, :W_XU]
    z_ref[0] = rest[:, W_XU:W_XU + W_Z]
    g_ref[0] = rest[:, W_XU + W_Z:W_XU + W_Z + W_G]
    dt_ref[0] = rest[:, W_XU + W_Z + W_G:]

    tl = xu.shape[0]
    has_prev = i > 1
    has_next = (i > 0) & (i < n - 1)
    pm = jnp.where(has_prev, xu_halo[SUBLANES - 1:SUBLANES, :], 0.0)
    n0 = jnp.where(has_next, xu_halo[SUBLANES:SUBLANES + 1, :], 0.0)
    n1 = jnp.where(has_next, xu_halo[SUBLANES + 1:SUBLANES + 2, :], 0.0)
    row = lax.broadcasted_iota(I32, xu.shape, 0)
    xm1 = jnp.where(row == 0, pm, pltpu.roll(xu, 1, 0))
    xp1 = jnp.where(row == tl - 1, n0, pltpu.roll(xu, tl - 1, 0))
    xp2 = jnp.where(row == tl - 1, n1, jnp.where(row == tl - 2, n0, pltpu.roll(xu, tl - 2, 0)))
    cw = cw_ref[...]
    yc = xm1 * cw[0:1] + xu * cw[1:2] + xp1 * cw[2:3] + xp2 * cw[3:4] + cb_ref[...]
    n_xbc = xbc_ref.shape[-1]
    xbc_ref[0] = _silu(yc[:, :n_xbc])
    u_ref[0] = yc[:, n_xbc:]

    q = qkv_scr[:, :W_Q]
    k = qkv_scr[:, W_Q:W_Q + W_K]
    cos = cos_ref[...]
    sin = sin_ref[...]
    ssq = jnp.dot((q * q).astype(BF16), oq_ref[...], preferred_element_type=F32)
    qn = q * lax.rsqrt(ssq * (1.0 / HEAD_DIM) + EPS) * qw_ref[...]
    q_ref[0] = (_rope(qn, cos, sin) * (HEAD_DIM ** -0.5 * LOG2E)).T.astype(BF16)
    ssk = jnp.dot((k * k).astype(BF16), ok_ref[...], preferred_element_type=F32)
    kn = k * lax.rsqrt(ssk * (1.0 / HEAD_DIM) + EPS) * kw_ref[...]
    k_ref[0] = _rope(kn, cos, sin).astype(BF16)
    v_ref[0] = qkv_scr[:, W_Q + W_K:].astype(BF16)


def _inproj(xt, modl, wcat, cos_t, sin_t, qw, kw, ones_q, ones_k, conv_w, conv_b, n_xbc, ctx_row):
    b, t, d = xt.shape
    tl = SEQ_TILE
    grid = (b, t // tl)
    r = tl // SUBLANES
    nblk = t // SUBLANES
    full = lambda shape: pl.BlockSpec(shape, lambda bb, i: (0,) * len(shape))
    seq = lambda w: pl.BlockSpec((1, tl, w), lambda bb, i: (bb, i, 0))
    outs = [(W_K, BF16), (W_V, BF16), (n_xbc, F32), (W_XU - n_xbc, F32), (W_Z, F32), (W_G, F32), (W_DT, F32)]
    return pl.pallas_call(
        _inproj_body,
        grid=grid,
        in_specs=[
            seq(d),
            pl.BlockSpec((1, SUBLANES, d), lambda bb, i: (bb, jnp.maximum(i * r - 1, 0), 0)),
            pl.BlockSpec((1, SUBLANES, d), lambda bb, i: (bb, jnp.minimum((i + 1) * r, nblk - 1), 0)),
            _mod_spec(1, d, ctx_row, 0),
            _mod_spec(0, d, ctx_row, 0),
            full((d, IN_PAD)),
            pl.BlockSpec((tl, LANES), lambda bb, i: (i, 0)),
            pl.BlockSpec((tl, LANES), lambda bb, i: (i, 0)),
            full((1, W_Q)), full((1, W_K)), full((W_Q, W_Q)), full((W_K, W_K)),
            full((CONV_W, W_XU)), full((1, W_XU)),
        ],
        out_specs=[pl.BlockSpec((1, W_Q, tl), lambda bb, i: (bb, 0, i))] + [seq(w) for w, _ in outs],
        out_shape=[jax.ShapeDtypeStruct((b, W_Q, t), BF16)]
        + [jax.ShapeDtypeStruct((b, t, w), dt) for w, dt in outs],
        scratch_shapes=[pltpu.VMEM((tl, W_Q + W_K + W_V), F32)],
        compiler_params=_cparams(),
        name="in_proj",
    )(xt, xt, xt, modl, modl, wcat, cos_t, sin_t, qw, kw, ones_q, ones_k, conv_w, conv_b)


def _attn_body(qt_ref, k_ref, vt_ref, o_ref, s_buf, p_buf, acc_ref, *, n_keys, kc):
    nh, hd, tq = qt_ref.shape[1:]
    qt = jnp.concatenate([qt_ref[0, h] for h in range(nh)], axis=1)
    cols = nh * tq
    rem = n_keys % kc
    chunks = ([(0, rem)] if rem else []) + [(rem + kc * i, kc) for i in range(n_keys // kc)]
    n = len(chunks)

    def scores(c):
        s0, sz = chunks[c]
        s_buf[c % 2, :sz, :] = jnp.dot(k_ref[0, 0, s0:s0 + sz, :], qt, preferred_element_type=F32)

    def softmax(c, m):
        sz = chunks[c][1]
        s = s_buf[c % 2, :sz, :]
        m_new = jnp.maximum(m, jnp.max(s, axis=0, keepdims=True))
        p_buf[c % 2, :sz, :] = jnp.exp2(s - m_new).astype(BF16)
        return m_new, jnp.exp2(m - m_new)

    def weighted_values(c, alpha):
        s0, sz = chunks[c]
        acc_ref[...] = alpha * acc_ref[...] + jnp.dot(vt_ref[0, 0, :, s0:s0 + sz], p_buf[c % 2, :sz, :],
                                                      preferred_element_type=F32)

    acc_ref[...] = jnp.zeros_like(acc_ref)
    m = jnp.full((1, cols), -jnp.inf, F32)
    alpha = None
    scores(0)
    for j in range(n + 1):
        if j + 1 < n:
            scores(j + 1)
        prev_alpha = alpha
        if j < n:
            m, alpha = softmax(j, m)
        if j >= 1:
            weighted_values(j - 1, prev_alpha)
    o = acc_ref[:hd, :] * (1.0 / acc_ref[hd:hd + 1, :])
    for h in range(nh):
        o_ref[0, h * hd:(h + 1) * hd, :] = o[:, h * tq:(h + 1) * tq]


def _attention(qt, k, vt, n_q, q_off, n_keys, tq):
    b = qt.shape[0]
    assert q_off % tq == 0
    off = q_off // tq
    kc = min(ATTN_KEY_CHUNK, n_keys)
    cols = Q_PER_KV * tq
    vrows = vt.shape[2]
    return pl.pallas_call(
        functools.partial(_attn_body, n_keys=n_keys, kc=kc),
        grid=(b, KV_HEADS, n_q // tq),
        scratch_shapes=[pltpu.VMEM((2, kc, cols), F32), pltpu.VMEM((2, kc, cols), BF16),
                        pltpu.VMEM((vrows, cols), F32)],
        in_specs=[
            pl.BlockSpec((1, Q_PER_KV, HEAD_DIM, tq), lambda bb, g, i: (bb, g, 0, i + off)),
            pl.BlockSpec((1, 1, n_keys, HEAD_DIM), lambda bb, g, i: (bb, g, 0, 0)),
            pl.BlockSpec((1, 1, vrows, n_keys), lambda bb, g, i: (bb, g, 0, 0)),
        ],
        out_specs=pl.BlockSpec((1, Q_PER_KV * HEAD_DIM, tq), lambda bb, g, i: (bb, g, i)),
        out_shape=jax.ShapeDtypeStruct((b, KV_HEADS * Q_PER_KV * HEAD_DIM, n_q), F32),
        compiler_params=_cparams(),
        name="gqa_attention",
    )(qt, k, vt)


def _seq_order(d, c, nc):
    return jnp.where(c == 0, 0, jnp.where(d == 0, c, nc - c))


def _ssd_body(xf_ref, xb_ref, dtf_ref, dtb_ref, bias_ref, a_ref, dvec_ref, yf_ref, yb_ref, s_scr):
    @pl.when(pl.program_id(1) == 0)
    def _():
        s_scr[...] = jnp.zeros_like(s_scr)

    nb = xf_ref.shape[0]
    streams = range(2 * nb)
    xbc = [(xf_ref, xb_ref)[k % 2][k // 2] for k in streams]
    q = xbc[0].shape[0]
    xw = SSD_HEADS * SSD_HEAD_DIM
    gw = 2 * SSD_STATE
    x = [v[:, :xw] for v in xbc]
    bm = [v[:, xw:xw + gw] for v in xbc]
    cm = [v[:, xw + gw:xw + 2 * gw] for v in xbc]
    dt_raw = [(dtf_ref, dtb_ref)[k % 2][k // 2] for k in streams]
    dt = [_softplus(dt_raw[k] + bias_ref[k % 2]) for k in streams]
    da = [dt[k] * a_ref[k % 2] for k in streams]
    ii = lax.broadcasted_iota(I32, (q, q), 0)
    jj = lax.broadcasted_iota(I32, (q, q), 1)
    mask = [jj <= ii, jj >= ii]
    tm = [jnp.where(m_, 1.0, 0.0).astype(BF16) for m_ in mask]
    da_hi = [da[k].astype(BF16) for k in streams]
    da_lo = [(da[k] - da_hi[k].astype(F32)).astype(BF16) for k in streams]
    cs = [jnp.dot(tm[k % 2], da_hi[k], preferred_element_type=F32)
          + jnp.dot(tm[k % 2], da_lo[k], preferred_element_type=F32) for k in streams]
    tot = [cs[k][q - 1:q, :] if k % 2 == 0 else cs[k][0:1, :] for k in streams]
    cst = [cs[k].T for k in streams]
    dec = [jnp.exp(tot[k] - cs[k]) for k in streams]
    ecs = [jnp.exp(cs[k]) for k in streams]
    etot = [jnp.exp(tot[k]) for k in streams]
    ys = [[] for _ in streams]
    nt = (((1,), (1,)), ((), ()))
    tn = (((0,), (0,)), ((), ()))
    for g in range(2):
        gs = slice(g * SSD_STATE, (g + 1) * SSD_STATE)
        bg = [bm[k][:, gs].astype(BF16) for k in streams]
        cg = [cm[k][:, gs].astype(BF16) for k in streams]
        gmat = [lax.dot_general(cg[k], bg[k], nt, preferred_element_type=F32) for k in streams]
        for hh in range(SSD_HEADS // 2):
            h = 2 * g + hh
            hs = slice(h * SSD_HEAD_DIM, (h + 1) * SSD_HEAD_DIM)
            lmat = [jnp.exp(jnp.where(mask[k % 2], cs[k][:, h:h + 1] - cst[k][h:h + 1, :], -jnp.inf))
                    for k in streams]
            xdt = [x[k][:, hs] * dt[k][:, h:h + 1] for k in streams]
            y_diag = [_bdot(gmat[k] * lmat[k], xdt[k]) for k in streams]
            s_in = [s_scr[k, h] for k in streams]
            y_off = [lax.dot_general(cg[k], s_in[k].astype(BF16), nt, preferred_element_type=F32) * ecs[k][:, h:h + 1]
                     for k in streams]
            xd = [(xdt[k] * dec[k][:, h:h + 1]).astype(BF16) for k in streams]
            for k in streams:
                ys[k].append(y_diag[k] + y_off[k])
                s_scr[k, h] = etot[k][:, h:h + 1] * s_in[k] + lax.dot_general(xd[k], bg[k], tn,
                                                                               preferred_element_type=F32)
    for k in streams:
        if k % 2 == 0:
            yf_ref[k // 2] = jnp.concatenate(ys[k], axis=1) + x[k] * dvec_ref[...]
        else:
            yb_ref[k // 2] = jnp.concatenate(ys[k], axis=1)


def _ssd(xbc, dt, bias, a_neg, dvec):
    b, t, cw = xbc.shape
    q = SEQ_TILE
    nc = t // q
    xw = SSD_HEADS * SSD_HEAD_DIM
    fwd = lambda c: c
    bwd = lambda c: _seq_order(1, c, nc)
    nb = SSD_BATCH
    return pl.pallas_call(
        _ssd_body,
        grid=(b // nb, nc),
        in_specs=[
            pl.BlockSpec((nb, q, cw), lambda bb, c: (bb, fwd(c), 0)),
            pl.BlockSpec((nb, q, cw), lambda bb, c: (bb, bwd(c), 0)),
            pl.BlockSpec((nb, q, LANES), lambda bb, c: (bb, fwd(c), 0)),
            pl.BlockSpec((nb, q, LANES), lambda bb, c: (bb, bwd(c), 1)),
            pl.BlockSpec((2, 1, LANES), lambda bb, c: (0, 0, 0)),
            pl.BlockSpec((2, 1, LANES), lambda bb, c: (0, 0, 0)),
            pl.BlockSpec((1, xw), lambda bb, c: (0, 0)),
        ],
        out_specs=[pl.BlockSpec((nb, q, xw), lambda bb, c: (bb, fwd(c), 0)),
                   pl.BlockSpec((nb, q, xw), lambda bb, c: (bb, bwd(c), 0))],
        out_shape=[jax.ShapeDtypeStruct((b, t, xw), F32)] * 2,
        scratch_shapes=[pltpu.VMEM((2 * nb, SSD_HEADS, SSD_HEAD_DIM, SSD_STATE), F32)],
        compiler_params=_cparams(),
        name="ssd_scan",
    )(xbc, xbc, dt, dt, bias, a_neg, dvec)


def _rg_body(u_ref, w_ref, bias_ref, lam_ref, y_ref, a_s, v_s, o_s, h_s, *, pitch):
    d = pl.program_id(0)
    c = pl.program_id(1)
    nb, tl, width = u_ref.shape
    ng = width // LANES

    @pl.when(c == 0)
    def _():
        h_s[...] = jnp.zeros_like(h_s)

    sp = _softplus(-lam_ref[0])
    w = w_ref[0]
    bias = bias_ref[0]
    for b in range(nb):
        ub = u_ref[b]
        pre = jnp.dot(ub.astype(BF16), w, preferred_element_type=F32) + bias
        r = _sigmoid(pre[:, :width])
        ig = _sigmoid(pre[:, width:])
        a = jnp.exp((-RG_C) * r * sp)
        v = jnp.sqrt(1.0 - a * a) * ig * ub
        for j in range(ng):
            a_s[j, pl.ds(b * pitch, tl), :] = a[:, j * LANES:(j + 1) * LANES]
            v_s[j, pl.ds(b * pitch, tl), :] = v[:, j * LANES:(j + 1) * LANES]

    def step(t, hs):
        te = jnp.where(d == 0, t, tl - 1 - t)
        out = []
        for j in range(ng):
            at = a_s[j, pl.ds(te, nb, stride=pitch), :]
            vt = v_s[j, pl.ds(te, nb, stride=pitch), :]
            hj = at * hs[j] + vt
            o_s[j, pl.ds(te, nb, stride=pitch), :] = hj
            out.append(hj)
        return tuple(out)

    h0 = tuple(h_s[:, j * LANES:(j + 1) * LANES] for j in range(ng))
    hf = lax.fori_loop(0, tl, step, h0, unroll=8)
    for j in range(ng):
        h_s[:, j * LANES:(j + 1) * LANES] = hf[j]
    for b in range(nb):
        y_ref[0, b] = jnp.concatenate([o_s[j, pl.ds(b * pitch, tl), :] for j in range(ng)], axis=1)


def _rglru(u, wg, bias, lam):
    b, t, width = u.shape
    assert b == SUBLANES, "the recurrence keeps one sample per sublane"
    tl = SEQ_TILE
    nc = t // tl
    pitch = tl + SUBLANES
    ng = width // LANES
    slab = pltpu.VMEM((ng, b * pitch, LANES), F32)
    return pl.pallas_call(
        functools.partial(_rg_body, pitch=pitch),
        grid=(2, nc),
        in_specs=[
            pl.BlockSpec((b, tl, width), lambda d, c: (0, _seq_order(d, c, nc), 0)),
            pl.BlockSpec((1, width, 2 * width), lambda d, c: (d, 0, 0)),
            pl.BlockSpec((1, 1, 2 * width), lambda d, c: (d, 0, 0)),
            pl.BlockSpec((1, 1, width), lambda d, c: (d, 0, 0)),
        ],
        out_specs=pl.BlockSpec((1, b, tl, width), lambda d, c: (d, 0, _seq_order(d, c, nc), 0)),
        out_shape=jax.ShapeDtypeStruct((2, b, t, width), F32),
        scratch_shapes=[slab, slab, slab, pltpu.VMEM((b, width), F32)],
        compiler_params=_cparams(),
        name="rglru_scan",
    )(u, wg, bias, lam)


def _rms(x, w):
    return x * lax.rsqrt(jnp.mean(x * x, axis=-1, keepdims=True) + EPS) * w


def _layer_norm(t, w, b):
    mu = jnp.mean(t, axis=-1, keepdims=True)
    tc = t - mu
    var = jnp.mean(tc * tc, axis=-1, keepdims=True)
    return tc * lax.rsqrt(var + EPS) * w + b


def _gelu_tanh(x):
    return 0.5 * x * (1.0 + jnp.tanh(math.sqrt(2.0 / math.pi) * (x + 0.044715 * (x * x * x))))


def _pack_bf16_pair(lo, hi):
    lb = pltpu.bitcast(lo.astype(BF16).astype(F32), jnp.uint32)
    hb = pltpu.bitcast(hi.astype(BF16).astype(F32), jnp.uint32)
    return (lb >> 16) | (hb & jnp.uint32(0xFFFF0000))


def _merge_body(*refs, has_ctx):
    if has_ctx:
        actx_ref, refs = refs[0], refs[1:]
    (alat_ref, ysf_ref, ysb_ref, z_ref, hr_ref, g_ref, x_ref, g1_ref, sc2_ref, sh2_ref, aw_ref, sw_ref, rw_ref,
     wo_ref, lnw_ref, lnb_ref, wr_ref, x1_ref, hp_ref, aff_ref, afft_ref, cat_scr, proj_scr) = refs
    tl = x_ref.shape[1]
    nh = MERGE_ROW_SPLIT
    rows = [slice(j * tl // nh, (j + 1) * tl // nh) for j in range(nh)]
    for rs in rows:
        a = alat_ref[0, :, rs]
        if has_ctx:
            a = jnp.where(pl.program_id(1) == 0, actx_ref[0, :, rs], a)
        an = _rms(a.T, aw_ref[...])
        sn = _rms((ysf_ref[0, rs] + ysb_ref[0, rs]) * _silu(z_ref[0, rs]), sw_ref[...])
        rn = _rms((hr_ref[0, 0, rs] + hr_ref[1, 0, rs]) * _gelu_tanh(g_ref[0, rs]), rw_ref[...])
        cat_scr[rs] = jnp.concatenate([an, sn, rn], axis=1).astype(BF16)
    for rs in rows:
        proj_scr[rs] = jnp.dot(cat_scr[rs], wo_ref[...], preferred_element_type=F32)
    for rs in rows:
        x1 = _layer_norm(DEEPNORM_ALPHA * x_ref[0, rs] + g1_ref[0] * proj_scr[rs], lnw_ref[...], lnb_ref[...])
        x1_ref[0, rs] = x1
        h2 = x1 * (1.0 + sc2_ref[0]) + sh2_ref[0]
        half = h2.shape[1] // 2
        hp_ref[0, rs] = _pack_bf16_pair(h2[:, :half], h2[:, half:])
        logits = jnp.dot(h2.astype(BF16), wr_ref[...], preferred_element_type=F32)
        lane = lax.broadcasted_iota(I32, logits.shape, 1)
        logits = jnp.where(lane < N_EXPERTS, logits, -jnp.inf)
        e = jnp.exp(logits - jnp.max(logits, axis=-1, keepdims=True))
        aff = e / jnp.sum(e, axis=-1, keepdims=True)
        aff_ref[0, rs] = aff
        afft_ref[0, :, rs] = aff.T[:N_EXPERTS, :]


def _merge(attn_ctx, attn_lat, ys, z, hr, g, xt, modl, aw, sw, rw, wo, lnw, lnb, wr, ctx_row, off):
    b, t, d = xt.shape
    tl = SEQ_TILE
    nt = t // tl - off
    has_ctx = off == 0
    aw_ = attn_lat.shape[1]
    seq = lambda w: pl.BlockSpec((1, tl, w), lambda bb, i: (bb, i + off, 0))
    pair = lambda w: pl.BlockSpec((2, 1, tl, w), lambda bb, i: (0, bb, i + off, 0))
    full = lambda shape: pl.BlockSpec(shape, lambda bb, i: (0,) * len(shape))
    out = lambda w: pl.BlockSpec((1, tl, w), lambda bb, i: (bb, i, 0))
    lat_off = 1 - off
    in_specs = [
        pl.BlockSpec((1, aw_, tl), lambda bb, i: (bb, 0, jnp.maximum(i - lat_off, 0))),
        seq(ys[0].shape[-1]), seq(ys[1].shape[-1]), seq(z.shape[-1]), pair(hr.shape[-1]), seq(g.shape[-1]), seq(d),
        _mod_spec(2, d, ctx_row, off), _mod_spec(4, d, ctx_row, off), _mod_spec(3, d, ctx_row, off),
        full(aw.shape), full(sw.shape), full(rw.shape), full(wo.shape), full(lnw.shape), full(lnb.shape),
        full(wr.shape),
    ]
    args = [attn_lat, ys[0], ys[1], z, hr, g, xt, modl, modl, modl, aw, sw, rw, wo, lnw, lnb, wr]
    if has_ctx:
        in_specs = [pl.BlockSpec((1, aw_, tl), lambda bb, i: (bb, 0, 0))] + in_specs
        args = [attn_ctx] + args
    rows = nt * tl
    return pl.pallas_call(
        functools.partial(_merge_body, has_ctx=has_ctx),
        grid=(b, nt),
        in_specs=in_specs,
        out_specs=[out(d), out(d // 2), out(LANES), pl.BlockSpec((1, N_EXPERTS, tl), lambda bb, i: (bb, 0, i))],
        out_shape=[
            jax.ShapeDtypeStruct((b, rows, d), F32),
            jax.ShapeDtypeStruct((b, rows, d // 2), jnp.uint32),
            jax.ShapeDtypeStruct((b, rows, LANES), F32),
            jax.ShapeDtypeStruct((b, N_EXPERTS, rows), F32),
        ],
        scratch_shapes=[pltpu.VMEM((tl, wo.shape[0]), BF16), pltpu.VMEM((tl, d), F32)],
        compiler_params=_cparams(),
        name="merge_outproj_ln1_router",
    )(*args)


def _topk_body(aff_ref, idx_ref, cum_scr, *, cap):
    aff = aff_ref[0]
    ne, n = aff.shape
    bits = pltpu.bitcast(aff, I32)

    def search(i, thr):
        cand = thr | lax.shift_left(jnp.int32(1), 30 - i)
        cnt = jnp.sum(jnp.where(bits >= cand, 1.0, 0.0), axis=1, keepdims=True)
        return jnp.where(cnt >= cap, cand, thr)

    thr = lax.fori_loop(0, 31, search, jnp.zeros((ne, 1), I32))
    gt = jnp.where(bits > thr, 1.0, 0.0)
    eq = jnp.where(bits == thr, 1.0, 0.0)
    need = cap - jnp.sum(gt, axis=1, keepdims=True)
    nblk = n // LANES
    r_ = lax.broadcasted_iota(I32, (LANES, LANES), 0)
    c_ = lax.broadcasted_iota(I32, (LANES, LANES), 1)
    upper = jnp.where(r_ <= c_, 1.0, 0.0).astype(BF16)

    def prefix(blocks):
        outs = []
        off = jnp.zeros((ne, 1), F32)
        for mk in blocks:
            w = jnp.dot(mk.astype(BF16), upper, preferred_element_type=F32) + off
            outs.append(w)
            off = w[:, LANES - 1:LANES]
        return outs

    blk = lambda a, k: a[:, k * LANES:(k + 1) * LANES]
    tie_rank = prefix([blk(eq, k) for k in range(nblk)])
    cum = prefix([jnp.maximum(blk(gt, k), blk(eq, k) * jnp.where(tie_rank[k] <= need, 1.0, 0.0))
                  for k in range(nblk)])
    for k in range(nblk):
        cum_scr[k] = cum[k]
    rows = -(-cap // LANES) * LANES
    slot = lax.broadcasted_iota(I32, (rows, LANES), 0).astype(F32)

    def per_expert(e, carry):
        acc = jnp.zeros((rows, LANES), F32)
        for k in range(nblk):
            acc = acc + jnp.where(cum_scr[k, pl.ds(e, 1), :] <= slot, 1.0, 0.0)
        idx_ref[0, pl.ds(e, 1), :] = jnp.sum(acc.T, axis=0, keepdims=True)[:, :cap].astype(I32)
        return carry

    lax.fori_loop(0, ne, per_expert, 0)


def _topk(aff_t, cap):
    b, e, n = aff_t.shape
    return pl.pallas_call(
        functools.partial(_topk_body, cap=cap),
        grid=(b,),
        in_specs=[pl.BlockSpec((1, e, n), lambda bb: (bb, 0, 0))],
        out_specs=pl.BlockSpec((1, e, cap), lambda bb: (bb, 0, 0)),
        out_shape=jax.ShapeDtypeStruct((b, e, cap), I32),
        scratch_shapes=[pltpu.VMEM((n // LANES, e, LANES), F32)],
        compiler_params=_cparams(),
        name="expert_choice_topk",
    )(aff_t)


def _moe_body(idxp_ref, idxc_ref, idxn_ref, hp_ref, aff_ref, wg_ref, wu_ref, wd_ref, out_ref,
              xs_scr, ag_scr, y_scr, xlo_scr, xhi_scr, hid_scr, *, cap):
    e = pl.program_id(1)
    f = pl.program_id(2)
    ne = pl.num_programs(1)
    nf = pl.num_programs(2)
    cur = e % 2
    oth = 1 - cur

    @pl.when((pl.program_id(0) == 0) & (e == 0) & (f == 0))
    def _():
        y_scr[...] = jnp.zeros_like(y_scr)

    @pl.when((e == 0) & (f == 0))
    def _():
        out_ref[...] = jnp.zeros_like(out_ref)

        def gather(j, carry):
            for u in range(SUBLANES):
                t = idxc_ref[0, 0, 0, j * SUBLANES + u]
                xs_scr[0, j, u:u + 1, :] = hp_ref[0, pl.ds(t, 1), :]
                ag_scr[0, j, u:u + 1, :] = aff_ref[0, pl.ds(t, 1), :]
            return carry

        lax.fori_loop(0, cap // SUBLANES, gather, 0)

    w = xs_scr[cur].reshape(cap, xs_scr.shape[-1])
    half = w.shape[1]
    xlo_scr[...] = pltpu.bitcast(w << 16, F32).astype(BF16)
    xhi_scr[...] = pltpu.bitcast(w & jnp.uint32(0xFFFF0000), F32).astype(BF16)
    lane = lax.broadcasted_iota(I32, (cap, LANES), 1)
    gate = jnp.sum(jnp.where(lane == e, ag_scr[cur].reshape(cap, LANES), 0.0), axis=1, keepdims=True)
    share = cap // nf
    base = f * share
    base_tile = f * (share // SUBLANES)
    has_prev = e > 0

    def row_copies(r0, r1):
        for g0 in range(r0, r1, MOE_ROW_GROUP):
            rows = range(g0, min(g0 + MOE_ROW_GROUP, r1))
            tps = [idxp_ref[0, 0, 0, base + r] for r in rows]
            sums = [out_ref[0, pl.ds(tp, 1), :]
                    + jnp.where(has_prev, y_scr[oth, base_tile + r // SUBLANES, r % SUBLANES:r % SUBLANES + 1, :], 0.0)
                    for tp, r in zip(tps, rows)]
            for tp, v in zip(tps, sums):
                out_ref[0, pl.ds(tp, 1), :] = v
            for r in rows:
                tn = idxn_ref[0, 0, 0, base + r]
                j, u = base_tile + r // SUBLANES, r % SUBLANES
                xs_scr[oth, j, u:u + 1, :] = hp_ref[0, pl.ds(tn, 1), :]
                ag_scr[oth, j, u:u + 1, :] = aff_ref[0, pl.ds(tn, 1), :]

    tf = wg_ref.shape[3]
    nsl = tf // MOE_FF_SLICE
    sl = lambda c: slice(c * MOE_FF_SLICE, (c + 1) * MOE_FF_SLICE)

    def hidden(c):
        xlo = xlo_scr[...]
        xhi = xhi_scr[...]
        hg = (jnp.dot(xlo, wg_ref[0, 0, :half, sl(c)], preferred_element_type=F32)
              + jnp.dot(xhi, wg_ref[0, 0, half:, sl(c)], preferred_element_type=F32))
        hu = (jnp.dot(xlo, wu_ref[0, 0, :half, sl(c)], preferred_element_type=F32)
              + jnp.dot(xhi, wu_ref[0, 0, half:, sl(c)], preferred_element_type=F32))
        hid_scr[c % 2] = (_silu(hg) * hu).astype(BF16)

    hidden(0)
    yp = None
    for c in range(nsl):
        if c + 1 < nsl:
            hidden(c + 1)
        part = jnp.dot(hid_scr[c % 2], wd_ref[0, 0, sl(c), :], preferred_element_type=F32)
        yp = part if yp is None else yp + part
        row_copies(share * c // nsl, share * (c + 1) // nsl)
    y_old = y_scr[cur].reshape(cap, y_scr.shape[-1])
    y_new = (jnp.where(f == 0, 0.0, y_old) + yp) * jnp.where(f == nf - 1, gate, 1.0)
    y_scr[cur] = y_new.reshape(y_scr.shape[1:])

    @pl.when((e == ne - 1) & (f == nf - 1))
    def _():
        def scatter(j, carry):
            for u in range(SUBLANES):
                t = idxc_ref[0, 0, 0, j * SUBLANES + u]
                out_ref[0, pl.ds(t, 1), :] = out_ref[0, pl.ds(t, 1), :] + y_scr[cur, j, u:u + 1, :]
            return carry

        lax.fori_loop(0, cap // SUBLANES, scatter, 0)


def _moe(idx, hp, aff, wg, wu, wd, layer, tf):
    b, t, half = hp.shape
    d = 2 * half
    ne, ff = wg.shape[1], wg.shape[3]
    cap = idx.shape[-1]
    nf = ff // tf
    assert nf >= 2 and cap % nf == 0
    one = pl.Buffered(1)
    idx4 = idx.reshape(b, ne, 1, cap)
    smem = lambda shift: pl.BlockSpec((1, 1, 1, cap), lambda bb, e, f: (bb, jnp.clip(e + shift, 0, ne - 1), 0, 0),
                                      memory_space=pltpu.SMEM)
    return pl.pallas_call(
        functools.partial(_moe_body, cap=cap),
        grid=(b, ne, nf),
        in_specs=[
            smem(-1), smem(0), smem(1),
            pl.BlockSpec((1, t, half), lambda bb, e, f: (bb, 0, 0), pipeline_mode=one),
            pl.BlockSpec((1, t, LANES), lambda bb, e, f: (bb, 0, 0), pipeline_mode=one),
            pl.BlockSpec((1, 1, d, tf), lambda bb, e, f: (layer, e, 0, f)),
            pl.BlockSpec((1, 1, d, tf), lambda bb, e, f: (layer, e, 0, f)),
            pl.BlockSpec((1, 1, tf, d), lambda bb, e, f: (layer, e, f, 0)),
        ],
        out_specs=pl.BlockSpec((1, t, d), lambda bb, e, f: (bb, 0, 0), pipeline_mode=one),
        out_shape=jax.ShapeDtypeStruct((b, t, d), F32),
        scratch_shapes=[
            pltpu.VMEM((2, cap // SUBLANES, SUBLANES, half), jnp.uint32),
            pltpu.VMEM((2, cap // SUBLANES, SUBLANES, LANES), F32),
            pltpu.VMEM((2, cap // SUBLANES, SUBLANES, d), F32),
            pltpu.VMEM((cap, half), BF16),
            pltpu.VMEM((cap, half), BF16),
            pltpu.VMEM((2, cap, MOE_FF_SLICE), BF16),
        ],
        compiler_params=_cparams(VMEM_LIMIT_MOE),
        name="expert_ffn",
    )(idx4, idx4, idx4, hp, aff, wg, wu, wd)


def _ln2_body(x1_ref, moe_ref, g2_ref, w_ref, b_ref, o_ref):
    o_ref[0] = _layer_norm(DEEPNORM_ALPHA * x1_ref[0] + g2_ref[0] * moe_ref[0], w_ref[...], b_ref[...])


def _ln2(x1, moe, modl, w, bias, ctx_row, off):
    b, t, d = x1.shape
    tl = SEQ_TILE
    seq = pl.BlockSpec((1, tl, d), lambda bb, i: (bb, i, 0))
    full = pl.BlockSpec((1, d), lambda bb, i: (0, 0))
    return pl.pallas_call(
        _ln2_body,
        grid=(b, t // tl),
        in_specs=[seq, seq, _mod_spec(5, d, ctx_row, off), full, full],
        out_specs=seq,
        out_shape=jax.ShapeDtypeStruct((b, t, d), F32),
        compiler_params=_cparams(),
        name="ln2",
    )(x1, moe, modl, w, bias)


def _block_diag_ones(width, block):
    r = jnp.arange(width)[:, None] // block
    c = jnp.arange(width)[None, :] // block
    return (r == c).astype(BF16)


def _block_diag(w):
    k, d, e = w.shape
    eye = jnp.eye(k, dtype=w.dtype)
    return (eye[:, None, :, None] * w[:, :, None, :]).reshape(k * d, k * e)


def _rope_tables(n_ctx, n_lat):
    pos = jnp.arange(n_lat)
    row = (pos // GRID_W).astype(F32)
    col = (pos % GRID_W).astype(F32)
    inv_freq = ROPE_THETA ** (-jnp.arange(0, AXIS_DIM, 2, dtype=F32) / AXIS_DIM)
    ang_r = row[:, None] * inv_freq
    ang_c = col[:, None] * inv_freq
    cos_h = jnp.concatenate([jnp.cos(ang_r)] * 2 + [jnp.cos(ang_c)] * 2, axis=1)
    sin_h = jnp.concatenate([-jnp.sin(ang_r), jnp.sin(ang_r), -jnp.sin(ang_c), jnp.sin(ang_c)], axis=1)
    reps = LANES // HEAD_DIM
    cos_t = jnp.concatenate([jnp.ones((n_ctx, LANES), F32), jnp.tile(cos_h, (1, reps))], axis=0)
    sin_t = jnp.concatenate([jnp.zeros((n_ctx, LANES), F32), jnp.tile(sin_h, (1, reps))], axis=0)
    return cos_t, sin_t


def _layer(xt, modl, rope, p, experts, layer, last, n_ctx, ctx_row):
    b, t, d = xt.shape
    n_lat = t - n_ctx
    cos_t, sin_t = rope
    w = p["w_in"]
    pad = jnp.zeros((d, LANES - SSD_HEADS), F32)
    wcat = jnp.concatenate(
        [w[:, 0:768], w[:, 768:1280], w[:, 1544:1800], w[:, 1280:1536], w[:, 1800:2056],
         w[:, 1536:1540], pad, w[:, 1540:1544], pad], axis=1).astype(BF16)
    qw = jnp.tile(p["q_norm"], W_Q // HEAD_DIM)[None]
    kw = jnp.tile(p["k_norm"], W_K // HEAD_DIM)[None]
    conv_w = jnp.concatenate([p["ssd_conv_w"], p["rg_conv_w"]], axis=1)
    conv_b = jnp.concatenate([p["ssd_conv_b"], p["rg_conv_b"]])[None]
    qt, k, v, xbc, u, z, g, dt = _inproj(xt, modl, wcat, cos_t, sin_t, qw, kw,
                                         _block_diag_ones(W_Q, HEAD_DIM), _block_diag_ones(W_K, HEAD_DIM),
                                         conv_w, conv_b, p["ssd_conv_w"].shape[1], ctx_row)

    qt = qt.reshape(b, W_Q // HEAD_DIM, HEAD_DIM, t)
    kh = k.reshape(b, t, KV_HEADS, HEAD_DIM).transpose(0, 2, 1, 3)
    vt = v.reshape(b, t, KV_HEADS, HEAD_DIM).transpose(0, 2, 3, 1)
    ones_pad = jnp.zeros((b, KV_HEADS, BF16_ROWS, t), BF16).at[:, :, 0, :].set(1.0)
    vt = jnp.concatenate([vt, ones_pad], axis=2)
    attn_lat = _attention(qt[..., n_ctx:], kh, vt, n_lat, 0, t, ATTN_Q_TILE)
    attn_ctx = None if last else _attention(qt, kh, vt, n_ctx, 0, n_ctx, min(ATTN_Q_TILE, n_ctx))

    lane_pad = lambda a: jnp.pad(a, ((0, 0), (0, LANES - a.shape[1])))[:, None, :]
    ys = _ssd(xbc, dt, lane_pad(p["ssd_dt_bias"]), lane_pad(-jnp.exp(p["ssd_a_log"])),
              jnp.repeat(p["ssd_d"], SSD_HEAD_DIM)[None])
    wgate = jnp.stack([jnp.concatenate([_block_diag(p["rg_wa"][j]), _block_diag(p["rg_wx"][j])], axis=1)
                       for j in range(2)]).astype(BF16)
    bgate = jnp.concatenate([p["rg_ba"], p["rg_bx"]], axis=1)[:, None, :]
    hr = _rglru(u, wgate, bgate, p["rg_lambda"][:, None, :])

    off = 1 if last else 0
    wr = jnp.pad(p["w_router"], ((0, 0), (0, LANES - N_EXPERTS))).astype(BF16)
    x1, hp, aff, aff_t = _merge(attn_ctx, attn_lat, ys, z, hr, g, xt, modl,
                                p["attn_out_norm"][None], p["ssd_norm"][None], p["rg_out_norm"][None],
                                p["w_out"].astype(BF16), p["ln1_w"][None], p["ln1_b"][None], wr, ctx_row, off)

    if last:
        idx = _topk(aff_t, EC_CAPACITY * n_lat // N_EXPERTS)
    else:
        idx_lat = _topk(aff_t[:, :, n_ctx:], EC_CAPACITY * n_lat // N_EXPERTS) + n_ctx
        idx_ctx = _topk(aff_t[:, :, :n_ctx], EC_CAPACITY * n_ctx // N_EXPERTS)
        idx = jnp.concatenate([idx_lat, idx_ctx], axis=-1)
    moe = _moe(idx, hp, aff, *experts, layer, MOE_FF_TILE)
    return _ln2(x1, moe, modl, p["ln2_w"][None], p["ln2_b"][None], ctx_row, off)


def kernel(x, c, ctx, c_ctx, w_mod, b_mod, w_in, q_norm, k_norm, attn_out_norm, ssd_conv_w, ssd_conv_b, ssd_dt_bias, ssd_a_log, ssd_d, ssd_norm, rg_conv_w, rg_conv_b, rg_wa, rg_ba, rg_wx, rg_bx, rg_lambda, rg_out_norm, w_out, ln1_w, ln1_b, w_router, w_gate, w_up, w_down, ln2_w, ln2_b):
    b, n_lat, d = x.shape
    n_ctx = ctx.shape[1]
    assert n_ctx == SEQ_TILE and n_lat % 512 == 0 and b == SUBLANES
    params = dict(w_in=w_in, q_norm=q_norm, k_norm=k_norm, attn_out_norm=attn_out_norm, ssd_conv_w=ssd_conv_w,
                  ssd_conv_b=ssd_conv_b, ssd_dt_bias=ssd_dt_bias, ssd_a_log=ssd_a_log, ssd_d=ssd_d,
                  ssd_norm=ssd_norm, rg_conv_w=rg_conv_w, rg_conv_b=rg_conv_b, rg_wa=rg_wa, rg_ba=rg_ba,
                  rg_wx=rg_wx, rg_bx=rg_bx, rg_lambda=rg_lambda, rg_out_norm=rg_out_norm, w_out=w_out,
                  ln1_w=ln1_w, ln1_b=ln1_b, w_router=w_router, ln2_w=ln2_w, ln2_b=ln2_b)
    experts = (w_gate.astype(BF16), w_up.astype(BF16), w_down.astype(BF16))
    mod = _modulation(c, c_ctx, w_mod, b_mod)
    rope = _rope_tables(n_ctx, n_lat)
    xt = jnp.concatenate([ctx, x], axis=1)
    depth = w_mod.shape[0]
    for l in range(depth):
        p = {name: val[l] for name, val in params.items()}
        modl = mod[l].reshape(mod.shape[1], 1, mod.shape[2])
        xt = _layer(xt, modl, rope, p, experts, l, l == depth - 1, n_ctx, b)
    return xt
```

```python
import functools
import math

import jax
import jax.numpy as jnp
from jax import lax
from jax.experimental import pallas as pl
from jax.experimental.pallas import tpu as pltpu

F32 = jnp.float32
BF16 = jnp.bfloat16
I32 = jnp.int32

DEPTH = 2
GRID_W = 64
HEAD_DIM = 64
KV_HEADS = 2
Q_PER_KV = 4
AXIS_DIM = HEAD_DIM // 2
ROPE_THETA = 10000.0
SSD_HEADS = 4
SSD_HEAD_DIM = 64
SSD_STATE = 64
RG_BLOCKS = 4
RG_C = 8.0
CONV_W = 4
N_EXPERTS = 16
EC_CAPACITY = 2
EPS = 1e-6
DEEPNORM_ALPHA = (2 * DEPTH) ** 0.25

LANES = 128
SUBLANES = 8
BF16_ROWS = 16
SEQ_TILE = 256
ATTN_Q_TILE = 512
ATTN_KEY_CHUNK = 256
MERGE_ROW_SPLIT = 2
SSD_BATCH = 4
MOE_FF_TILE = 1024
MOE_FF_SLICE = 256
MOE_ROW_GROUP = 4
LOG2E = math.log2(math.e)
VMEM_LIMIT = 48 * 1024 * 1024
VMEM_LIMIT_MOE = 60 * 1024 * 1024


def _cparams(limit=VMEM_LIMIT):
    return pltpu.CompilerParams(vmem_limit_bytes=limit)


def _sigmoid(x):
    return 1.0 / (1.0 + jnp.exp(-x))


def _silu(x):
    return x * _sigmoid(x)


def _softplus(x):
    return jnp.maximum(x, 0.0) + jnp.log1p(jnp.exp(-jnp.abs(x)))


def _bdot(a, b):
    return jnp.dot(a.astype(BF16), b.astype(BF16), preferred_element_type=F32)


def _mod_body(c_ref, w_ref, b_ref, o_ref):
    c = c_ref[...]
    o_ref[0] = _bdot(_silu(c), w_ref[0]) + b_ref[0]


def _modulation(c, c_ctx, w_mod, b_mod):
    depth, d, n6 = w_mod.shape
    b = c.shape[0]
    rows = 2 * SUBLANES
    cc = jnp.zeros((rows, d), F32).at[:b].set(c).at[b].set(c_ctx)
    tn = 1536
    return pl.pallas_call(
        _mod_body,
        grid=(depth, n6 // tn),
        in_specs=[
            pl.BlockSpec((rows, d), lambda l, j: (0, 0)),
            pl.BlockSpec((1, d, tn), lambda l, j: (l, 0, j)),
            pl.BlockSpec((1, 1, tn), lambda l, j: (l, 0, j)),
        ],
        out_specs=pl.BlockSpec((1, rows, tn), lambda l, j: (l, 0, j)),
        out_shape=jax.ShapeDtypeStruct((depth, rows, n6), F32),
        compiler_params=_cparams(),
        name="adaln_mod",
    )(cc, w_mod, b_mod.reshape(depth, 1, n6))


def _mod_spec(j, d, ctx_row, off):
    if off == 0:
        return pl.BlockSpec((1, 1, d), lambda b, i: (jnp.where(i == 0, ctx_row, b), 0, j))
    return pl.BlockSpec((1, 1, d), lambda b, i: (b, 0, j))


W_Q, W_K, W_V, W_XU, W_Z, W_G, W_DT = 512, 128, 128, 768, 256, 256, 256
IN_PAD = W_Q + W_K + W_V + W_XU + W_Z + W_G + W_DT


def _rope(t, cos, sin_signed):
    rows = t.shape[0]
    lane = lax.broadcasted_iota(I32, (rows, LANES), 1)
    first = (lane % AXIS_DIM) < (AXIS_DIM // 2)
    outs = []
    for c in range(t.shape[1] // LANES):
        tc = t[:, c * LANES:(c + 1) * LANES]
        partner = jnp.where(first, pltpu.roll(tc, LANES - AXIS_DIM // 2, 1), pltpu.roll(tc, AXIS_DIM // 2, 1))
        outs.append(tc * cos + partner * sin_signed)
    return outs[0] if len(outs) == 1 else jnp.concatenate(outs, axis=1)


def _inproj_body(x_ref, xp_ref, xn_ref, sc_ref, sh_ref, w_ref, cos_ref, sin_ref, qw_ref, kw_ref, oq_ref, ok_ref,
                 cw_ref, cb_ref, q_ref, k_ref, v_ref, xbc_ref, u_ref, z_ref, g_ref, dt_ref, qkv_scr):
    i = pl.program_id(1)
    n = pl.num_programs(1)
    scale = 1.0 + sc_ref[0]
    shift = sh_ref[0]
    h = (x_ref[0] * scale + shift).astype(BF16)
    n_qkv = W_Q + W_K + W_V
    qkv_scr[...] = jnp.dot(h, w_ref[:, :n_qkv], preferred_element_type=F32)
    rest = jnp.dot(h, w_ref[:, n_qkv:], preferred_element_type=F32)
    halo = (jnp.concatenate([xp_ref[0], xn_ref[0]], axis=0) * scale + shift).astype(BF16)
    xu_halo = jnp.dot(halo, w_ref[:, n_qkv:n_qkv + W_XU], preferred_element_type=F32)
    xu = rest[:, :W_XU]
    z_ref[0] = rest[:, W_XU:W_XU + W_Z]
    g_ref[0] = rest[:, W_XU + W_Z:W_XU + W_Z + W_G]
    dt_ref[0] = rest[:, W_XU + W_Z + W_G:]

    tl = xu.shape[0]
    has_prev = i > 1
    has_next = (i > 0) & (i < n - 1)
    pm = jnp.where(has_prev, xu_halo[SUBLANES - 1:SUBLANES, :], 0.0)
    n0 = jnp.where(has_next, xu_halo[SUBLANES:SUBLANES + 1, :], 0.0)
    n1 = jnp.where(has_next, xu_halo[SUBLANES + 1:SUBLANES + 2, :], 0.0)
    row = lax.broadcasted_iota(I32, xu.shape, 0)
    xm1 = jnp.where(row == 0, pm, pltpu.roll(xu, 1, 0))
    xp1 = jnp.where(row == tl - 1, n0, pltpu.roll(xu, tl - 1, 0))
    xp2 = jnp.where(row == tl - 1, n1, jnp.where(row == tl - 2, n0, pltpu.roll(xu, tl - 2, 0)))
    cw = cw_ref[...]
    yc = xm1 * cw[0:1] + xu * cw[1:2] + xp1 * cw[2:3] + xp2 * cw[3:4] + cb_ref[...]
    n_xbc = xbc_ref.shape[-1]
    xbc_ref[0] = _silu(yc[:, :n_xbc])
    u_ref[0] = yc[:, n_xbc:]

    q = qkv_scr[:, :W_Q]
    k = qkv_scr[:, W_Q:W_Q + W_K]
    cos = cos_ref[...]
    sin = sin_ref[...]
    ssq = jnp.dot((q * q).astype(BF16), oq_ref[...], preferred_element_type=F32)
    qn = q * lax.rsqrt(ssq * (1.0 / HEAD_DIM) + EPS) * qw_ref[...]
    q_ref[0] = (_rope(qn, cos, sin) * (HEAD_DIM ** -0.5 * LOG2E)).T.astype(BF16)
    ssk = jnp.dot((k * k).astype(BF16), ok_ref[...], preferred_element_type=F32)
    kn = k * lax.rsqrt(ssk * (1.0 / HEAD_DIM) + EPS) * kw_ref[...]
    k_ref[0] = _rope(kn, cos, sin).astype(BF16)
    v_ref[0] = qkv_scr[:, W_Q + W_K:].astype(BF16)


def _inproj(xt, modl, wcat, cos_t, sin_t, qw, kw, ones_q, ones_k, conv_w, conv_b, n_xbc, ctx_row):
    b, t, d = xt.shape
    tl = SEQ_TILE
    grid = (b, t // tl)
    r = tl // SUBLANES
    nblk = t // SUBLANES
    full = lambda shape: pl.BlockSpec(shape, lambda bb, i: (0,) * len(shape))
    seq = lambda w: pl.BlockSpec((1, tl, w), lambda bb, i: (bb, i, 0))
    outs = [(W_K, BF16), (W_V, BF16), (n_xbc, F32), (W_XU - n_xbc, F32), (W_Z, F32), (W_G, F32), (W_DT, F32)]
    return pl.pallas_call(
        _inproj_body,
        grid=grid,
        in_specs=[
            seq(d),
            pl.BlockSpec((1, SUBLANES, d), lambda bb, i: (bb, jnp.maximum(i * r - 1, 0), 0)),
            pl.BlockSpec((1, SUBLANES, d), lambda bb, i: (bb, jnp.minimum((i + 1) * r, nblk - 1), 0)),
            _mod_spec(1, d, ctx_row, 0),
            _mod_spec(0, d, ctx_row, 0),
            full((d, IN_PAD)),
            pl.BlockSpec((tl, LANES), lambda bb, i: (i, 0)),
            pl.BlockSpec((tl, LANES), lambda bb, i: (i, 0)),
            full((1, W_Q)), full((1, W_K)), full((W_Q, W_Q)), full((W_K, W_K)),
            full((CONV_W, W_XU)), full((1, W_XU)),
        ],
        out_specs=[pl.BlockSpec((1, W_Q, tl), lambda bb, i: (bb, 0, i))] + [seq(w) for w, _ in outs],
        out_shape=[jax.ShapeDtypeStruct((b, W_Q, t), BF16)]
        + [jax.ShapeDtypeStruct((b, t, w), dt) for w, dt in outs],
        scratch_shapes=[pltpu.VMEM((tl, W_Q + W_K + W_V), F32)],
        compiler_params=_cparams(),
        name="in_proj",
    )(xt, xt, xt, modl, modl, wcat, cos_t, sin_t, qw, kw, ones_q, ones_k, conv_w, conv_b)


def _attn_body(qt_ref, k_ref, vt_ref, o_ref, s_buf, p_buf, acc_ref, *, n_keys, kc):
    nh, hd, tq = qt_ref.shape[1:]
    qt = jnp.concatenate([qt_ref[0, h] for h in range(nh)], axis=1)
    cols = nh * tq
    rem = n_keys % kc
    chunks = ([(0, rem)] if rem else []) + [(rem + kc * i, kc) for i in range(n_keys // kc)]
    n = len(chunks)

    def scores(c):
        s0, sz = chunks[c]
        s_buf[c % 2, :sz, :] = jnp.dot(k_ref[0, 0, s0:s0 + sz, :], qt, preferred_element_type=F32)

    def softmax(c, m):
        sz = chunks[c][1]
        s = s_buf[c % 2, :sz, :]
        m_new = jnp.maximum(m, jnp.max(s, axis=0, keepdims=True))
        p_buf[c % 2, :sz, :] = jnp.exp2(s - m_new).astype(BF16)
        return m_new, jnp.exp2(m - m_new)

    def weighted_values(c, alpha):
        s0, sz = chunks[c]
        acc_ref[...] = alpha * acc_ref[...] + jnp.dot(vt_ref[0, 0, :, s0:s0 + sz], p_buf[c % 2, :sz, :],
                                                      preferred_element_type=F32)

    acc_ref[...] = jnp.zeros_like(acc_ref)
    m = jnp.full((1, cols), -jnp.inf, F32)
    alpha = None
    scores(0)
    for j in range(n + 1):
        if j + 1 < n:
            scores(j + 1)
        if j >= 1:
            weighted_values(j - 1, alpha)
        if j < n:
            m, alpha = softmax(j, m)
    o = acc_ref[:hd, :] * (1.0 / acc_ref[hd:hd + 1, :])
    for h in range(nh):
        o_ref[0, h * hd:(h + 1) * hd, :] = o[:, h * tq:(h + 1) * tq]


def _attention(qt, k, vt, n_q, q_off, n_keys, tq):
    b = qt.shape[0]
    assert q_off % tq == 0
    off = q_off // tq
    kc = min(ATTN_KEY_CHUNK, n_keys)
    cols = Q_PER_KV * tq
    vrows = vt.shape[2]
    return pl.pallas_call(
        functools.partial(_attn_body, n_keys=n_keys, kc=kc),
        grid=(b, KV_HEADS, n_q // tq),
        scratch_shapes=[pltpu.VMEM((2, kc, cols), F32), pltpu.VMEM((2, kc, cols), BF16),
                        pltpu.VMEM((vrows, cols), F32)],
        in_specs=[
            pl.BlockSpec((1, Q_PER_KV, HEAD_DIM, tq), lambda bb, g, i: (bb, g, 0, i + off)),
            pl.BlockSpec((1, 1, n_keys, HEAD_DIM), lambda bb, g, i: (bb, g, 0, 0)),
            pl.BlockSpec((1, 1, vrows, n_keys), lambda bb, g, i: (bb, g, 0, 0)),
        ],
        out_specs=pl.BlockSpec((1, Q_PER_KV * HEAD_DIM, tq), lambda bb, g, i: (bb, g, i)),
        out_shape=jax.ShapeDtypeStruct((b, KV_HEADS * Q_PER_KV * HEAD_DIM, n_q), F32),
        compiler_params=_cparams(),
        name="gqa_attention",
    )(qt, k, vt)


def _seq_order(d, c, nc):
    return jnp.where(c == 0, 0, jnp.where(d == 0, c, nc - c))


def _ssd_body(xf_ref, xb_ref, dtf_ref, dtb_ref, bias_ref, a_ref, dvec_ref, yf_ref, yb_ref, s_scr):
    @pl.when(pl.program_id(1) == 0)
    def _():
        s_scr[...] = jnp.zeros_like(s_scr)

    nb = xf_ref.shape[0]
    streams = range(2 * nb)
    xbc = [(xf_ref, xb_ref)[k % 2][k // 2] for k in streams]
    q = xbc[0].shape[0]
    xw = SSD_HEADS * SSD_HEAD_DIM
    gw = 2 * SSD_STATE
    x = [v[:, :xw] for v in xbc]
    bm = [v[:, xw:xw + gw] for v in xbc]
    cm = [v[:, xw + gw:xw + 2 * gw] for v in xbc]
    dt_raw = [(dtf_ref, dtb_ref)[k % 2][k // 2] for k in streams]
    dt = [_softplus(dt_raw[k] + bias_ref[k % 2]) for k in streams]
    da = [dt[k] * a_ref[k % 2] for k in streams]
    ii = lax.broadcasted_iota(I32, (q, q), 0)
    jj = lax.broadcasted_iota(I32, (q, q), 1)
    mask = [jj <= ii, jj >= ii]
    tm = [jnp.where(m_, 1.0, 0.0).astype(BF16) for m_ in mask]
    da_hi = [da[k].astype(BF16) for k in streams]
    da_lo = [(da[k] - da_hi[k].astype(F32)).astype(BF16) for k in streams]
    cs = [jnp.dot(tm[k % 2], da_hi[k], preferred_element_type=F32)
          + jnp.dot(tm[k % 2], da_lo[k], preferred_element_type=F32) for k in streams]
    tot = [cs[k][q - 1:q, :] if k % 2 == 0 else cs[k][0:1, :] for k in streams]
    cst = [cs[k].T for k in streams]
    dec = [jnp.exp(tot[k] - cs[k]) for k in streams]
    ecs = [jnp.exp(cs[k]) for k in streams]
    etot = [jnp.exp(tot[k]) for k in streams]
    ys = [[] for _ in streams]
    nt = (((1,), (1,)), ((), ()))
    tn = (((0,), (0,)), ((), ()))
    for g in range(2):
        gs = slice(g * SSD_STATE, (g + 1) * SSD_STATE)
        bg = [bm[k][:, gs].astype(BF16) for k in streams]
        cg = [cm[k][:, gs].astype(BF16) for k in streams]
        gmat = [lax.dot_general(cg[k], bg[k], nt, preferred_element_type=F32) for k in streams]
        for hh in range(SSD_HEADS // 2):
            h = 2 * g + hh
            hs = slice(h * SSD_HEAD_DIM, (h + 1) * SSD_HEAD_DIM)
            lmat = [jnp.exp(jnp.where(mask[k % 2], cs[k][:, h:h + 1] - cst[k][h:h + 1, :], -jnp.inf))
                    for k in streams]
            xdt = [x[k][:, hs] * dt[k][:, h:h + 1] for k in streams]
            y_diag = [_bdot(gmat[k] * lmat[k], xdt[k]) for k in streams]
            s_in = [s_scr[k, h] for k in streams]
            y_off = [lax.dot_general(cg[k], s_in[k].astype(BF16), nt, preferred_element_type=F32) * ecs[k][:, h:h + 1]
                     for k in streams]
            xd = [(xdt[k] * dec[k][:, h:h + 1]).astype(BF16) for k in streams]
            for k in streams:
                ys[k].append(y_diag[k] + y_off[k])
                s_scr[k, h] = etot[k][:, h:h + 1] * s_in[k] + lax.dot_general(xd[k], bg[k], tn,
                                                                               preferred_element_type=F32)
    for k in streams:
        if k % 2 == 0:
            yf_ref[k // 2] = jnp.concatenate(ys[k], axis=1) + x[k] * dvec_ref[...]
        else:
            yb_ref[k // 2] = jnp.concatenate(ys[k], axis=1)


def _ssd(xbc, dt, bias, a_neg, dvec):
    b, t, cw = xbc.shape
    q = SEQ_TILE
    nc = t // q
    xw = SSD_HEADS * SSD_HEAD_DIM
    fwd = lambda c: c
    bwd = lambda c: _seq_order(1, c, nc)
    nb = SSD_BATCH
    return pl.pallas_call(
        _ssd_body,
        grid=(b // nb, nc),
        in_specs=[
            pl.BlockSpec((nb, q, cw), lambda bb, c: (bb, fwd(c), 0)),
            pl.BlockSpec((nb, q, cw), lambda bb, c: (bb, bwd(c), 0)),
            pl.BlockSpec((nb, q, LANES), lambda bb, c: (bb, fwd(c), 0)),
            pl.BlockSpec((nb, q, LANES), lambda bb, c: (bb, bwd(c), 1)),
            pl.BlockSpec((2, 1, LANES), lambda bb, c: (0, 0, 0)),
            pl.BlockSpec((2, 1, LANES), lambda bb, c: (0, 0, 0)),
            pl.BlockSpec((1, xw), lambda bb, c: (0, 0)),
        ],
        out_specs=[pl.BlockSpec((nb, q, xw), lambda bb, c: (bb, fwd(c), 0)),
                   pl.BlockSpec((nb, q, xw), lambda bb, c: (bb, bwd(c), 0))],
        out_shape=[jax.ShapeDtypeStruct((b, t, xw), F32)] * 2,
        scratch_shapes=[pltpu.VMEM((2 * nb, SSD_HEADS, SSD_HEAD_DIM, SSD_STATE), F32)],
        compiler_params=_cparams(),
        name="ssd_scan",
    )(xbc, xbc, dt, dt, bias, a_neg, dvec)


def _rg_body(u_ref, w_ref, bias_ref, lam_ref, y_ref, a_s, v_s, o_s, h_s, *, pitch):
    d = pl.program_id(0)
    c = pl.program_id(1)
    nb, tl, width = u_ref.shape
    ng = width // LANES

    @pl.when(c == 0)
    def _():
        h_s[...] = jnp.zeros_like(h_s)

    sp = _softplus(-lam_ref[0])
    w = w_ref[0]
    bias = bias_ref[0]
    for b in range(nb):
        ub = u_ref[b]
        pre = jnp.dot(ub.astype(BF16), w, preferred_element_type=F32) + bias
        r = _sigmoid(pre[:, :width])
        ig = _sigmoid(pre[:, width:])
        a = jnp.exp((-RG_C) * r * sp)
        v = jnp.sqrt(1.0 - a * a) * ig * ub
        for j in range(ng):
            a_s[j, pl.ds(b * pitch, tl), :] = a[:, j * LANES:(j + 1) * LANES]
            v_s[j, pl.ds(b * pitch, tl), :] = v[:, j * LANES:(j + 1) * LANES]

    def step(t, hs):
        te = jnp.where(d == 0, t, tl - 1 - t)
        out = []
        for j in range(ng):
            at = a_s[j, pl.ds(te, nb, stride=pitch), :]
            vt = v_s[j, pl.ds(te, nb, stride=pitch), :]
            hj = at * hs[j] + vt
            o_s[j, pl.ds(te, nb, stride=pitch), :] = hj
            out.append(hj)
        return tuple(out)

    h0 = tuple(h_s[:, j * LANES:(j + 1) * LANES] for j in range(ng))
    hf = lax.fori_loop(0, tl, step, h0, unroll=8)
    for j in range(ng):
        h_s[:, j * LANES:(j + 1) * LANES] = hf[j]
    for b in range(nb):
        y_ref[0, b] = jnp.concatenate([o_s[j, pl.ds(b * pitch, tl), :] for j in range(ng)], axis=1)


def _rglru(u, wg, bias, lam):
    b, t, width = u.shape
    assert b == SUBLANES, "the recurrence keeps one sample per sublane"
    tl = SEQ_TILE
    nc = t // tl
    pitch = tl + SUBLANES
    ng = width // LANES
    slab = pltpu.VMEM((ng, b * pitch, LANES), F32)
    return pl.pallas_call(
        functools.partial(_rg_body, pitch=pitch),
        grid=(2, nc),
        in_specs=[
            pl.BlockSpec((b, tl, width), lambda d, c: (0, _seq_order(d, c, nc), 0)),
            pl.BlockSpec((1, width, 2 * width), lambda d, c: (d, 0, 0)),
            pl.BlockSpec((1, 1, 2 * width), lambda d, c: (d, 0, 0)),
            pl.BlockSpec((1, 1, width), lambda d, c: (d, 0, 0)),
        ],
        out_specs=pl.BlockSpec((1, b, tl, width), lambda d, c: (d, 0, _seq_order(d, c, nc), 0)),
        out_shape=jax.ShapeDtypeStruct((2, b, t, width), F32),
        scratch_shapes=[slab, slab, slab, pltpu.VMEM((b, width), F32)],
        compiler_params=_cparams(),
        name="rglru_scan",
    )(u, wg, bias, lam)


def _rms(x, w):
    return x * lax.rsqrt(jnp.mean(x * x, axis=-1, keepdims=True) + EPS) * w


def _layer_norm(t, w, b):
    mu = jnp.mean(t, axis=-1, keepdims=True)
    tc = t - mu
    var = jnp.mean(tc * tc, axis=-1, keepdims=True)
    return tc * lax.rsqrt(var + EPS) * w + b


def _gelu_tanh(x):
    return 0.5 * x * (1.0 + jnp.tanh(math.sqrt(2.0 / math.pi) * (x + 0.044715 * (x * x * x))))


def _pack_bf16_pair(lo, hi):
    lb = pltpu.bitcast(lo.astype(BF16).astype(F32), jnp.uint32)
    hb = pltpu.bitcast(hi.astype(BF16).astype(F32), jnp.uint32)
    return (lb >> 16) | (hb & jnp.uint32(0xFFFF0000))


def _merge_body(*refs, has_ctx):
    if has_ctx:
        actx_ref, refs = refs[0], refs[1:]
    (alat_ref, ysf_ref, ysb_ref, z_ref, hr_ref, g_ref, x_ref, g1_ref, sc2_ref, sh2_ref, aw_ref, sw_ref, rw_ref,
     wo_ref, lnw_ref, lnb_ref, wr_ref, x1_ref, hp_ref, aff_ref, afft_ref, cat_scr, proj_scr) = refs
    tl = x_ref.shape[1]
    nh = MERGE_ROW_SPLIT
    rows = [slice(j * tl // nh, (j + 1) * tl // nh) for j in range(nh)]
    for rs in rows:
        a = alat_ref[0, :, rs]
        if has_ctx:
            a = jnp.where(pl.program_id(1) == 0, actx_ref[0, :, rs], a)
        an = _rms(a.T, aw_ref[...])
        sn = _rms((ysf_ref[0, rs] + ysb_ref[0, rs]) * _silu(z_ref[0, rs]), sw_ref[...])
        rn = _rms((hr_ref[0, 0, rs] + hr_ref[1, 0, rs]) * _gelu_tanh(g_ref[0, rs]), rw_ref[...])
        cat_scr[rs] = jnp.concatenate([an, sn, rn], axis=1).astype(BF16)
    for rs in rows:
        proj_scr[rs] = jnp.dot(cat_scr[rs], wo_ref[...], preferred_element_type=F32)
    for rs in rows:
        x1 = _layer_norm(DEEPNORM_ALPHA * x_ref[0, rs] + g1_ref[0] * proj_scr[rs], lnw_ref[...], lnb_ref[...])
        x1_ref[0, rs] = x1
        h2 = x1 * (1.0 + sc2_ref[0]) + sh2_ref[0]
        half = h2.shape[1] // 2
        hp_ref[0, rs] = _pack_bf16_pair(h2[:, :half], h2[:, half:])
        logits = jnp.dot(h2.astype(BF16), wr_ref[...], preferred_element_type=F32)
        lane = lax.broadcasted_iota(I32, logits.shape, 1)
        logits = jnp.where(lane < N_EXPERTS, logits, -jnp.inf)
        e = jnp.exp(logits - jnp.max(logits, axis=-1, keepdims=True))
        aff = e / jnp.sum(e, axis=-1, keepdims=True)
        aff_ref[0, rs] = aff
        afft_ref[0, :, rs] = aff.T[:N_EXPERTS, :]


def _merge(attn_ctx, attn_lat, ys, z, hr, g, xt, modl, aw, sw, rw, wo, lnw, lnb, wr, ctx_row, off):
    b, t, d = xt.shape
    tl = SEQ_TILE
    nt = t // tl - off
    has_ctx = off == 0
    aw_ = attn_lat.shape[1]
    seq = lambda w: pl.BlockSpec((1, tl, w), lambda bb, i: (bb, i + off, 0))
    pair = lambda w: pl.BlockSpec((2, 1, tl, w), lambda bb, i: (0, bb, i + off, 0))
    full = lambda shape: pl.BlockSpec(shape, lambda bb, i: (0,) * len(shape))
    out = lambda w: pl.BlockSpec((1, tl, w), lambda bb, i: (bb, i, 0))
    lat_off = 1 - off
    in_specs = [
        pl.BlockSpec((1, aw_, tl), lambda bb, i: (bb, 0, jnp.maximum(i - lat_off, 0))),
        seq(ys[0].shape[-1]), seq(ys[1].shape[-1]), seq(z.shape[-1]), pair(hr.shape[-1]), seq(g.shape[-1]), seq(d),
        _mod_spec(2, d, ctx_row, off), _mod_spec(4, d, ctx_row, off), _mod_spec(3, d, ctx_row, off),
        full(aw.shape), full(sw.shape), full(rw.shape), full(wo.shape), full(lnw.shape), full(lnb.shape),
        full(wr.shape),
    ]
    args = [attn_lat, ys[0], ys[1], z, hr, g, xt, modl, modl, modl, aw, sw, rw, wo, lnw, lnb, wr]
    if has_ctx:
        in_specs = [pl.BlockSpec((1, aw_, tl), lambda bb, i: (bb, 0, 0))] + in_specs
        args = [attn_ctx] + args
    rows = nt * tl
    return pl.pallas_call(
        functools.partial(_merge_body, has_ctx=has_ctx),
        grid=(b, nt),
        in_specs=in_specs,
        out_specs=[out(d), out(d // 2), out(LANES), pl.BlockSpec((1, N_EXPERTS, tl), lambda bb, i: (bb, 0, i))],
        out_shape=[
            jax.ShapeDtypeStruct((b, rows, d), F32),
            jax.ShapeDtypeStruct((b, rows, d // 2), jnp.uint32),
            jax.ShapeDtypeStruct((b, rows, LANES), F32),
            jax.ShapeDtypeStruct((b, N_EXPERTS, rows), F32),
        ],
        scratch_shapes=[pltpu.VMEM((tl, wo.shape[0]), BF16), pltpu.VMEM((tl, d), F32)],
        compiler_params=_cparams(),
        name="merge_outproj_ln1_router",
    )(*args)


def _topk_body(aff_ref, idx_ref, cum_scr, *, cap):
    aff = aff_ref[0]
    ne, n = aff.shape
    bits = pltpu.bitcast(aff, I32)

    def search(i, thr):
        cand = thr | lax.shift_left(jnp.int32(1), 30 - i)
        cnt = jnp.sum(jnp.where(bits >= cand, 1.0, 0.0), axis=1, keepdims=True)
        return jnp.where(cnt >= cap, cand, thr)

    thr = lax.fori_loop(0, 31, search, jnp.zeros((ne, 1), I32))
    gt = jnp.where(bits > thr, 1.0, 0.0)
    eq = jnp.where(bits == thr, 1.0, 0.0)
    need = cap - jnp.sum(gt, axis=1, keepdims=True)
    nblk = n // LANES
    r_ = lax.broadcasted_iota(I32, (LANES, LANES), 0)
    c_ = lax.broadcasted_iota(I32, (LANES, LANES), 1)
    upper = jnp.where(r_ <= c_, 1.0, 0.0).astype(BF16)

    def prefix(blocks):
        outs = []
        off = jnp.zeros((ne, 1), F32)
        for mk in blocks:
            w = jnp.dot(mk.astype(BF16), upper, preferred_element_type=F32) + off
            outs.append(w)
            off = w[:, LANES - 1:LANES]
        return outs

    blk = lambda a, k: a[:, k * LANES:(k + 1) * LANES]
    tie_rank = prefix([blk(eq, k) for k in range(nblk)])
    cum = prefix([jnp.maximum(blk(gt, k), blk(eq, k) * jnp.where(tie_rank[k] <= need, 1.0, 0.0))
                  for k in range(nblk)])
    for k in range(nblk):
        cum_scr[k] = cum[k]
    rows = -(-cap // LANES) * LANES
    slot = lax.broadcasted_iota(I32, (rows, LANES), 0).astype(F32)

    def per_expert(e, carry):
        acc = jnp.zeros((rows, LANES), F32)
        for k in range(nblk):
            acc = acc + jnp.where(cum_scr[k, pl.ds(e, 1), :] <= slot, 1.0, 0.0)
        idx_ref[0, pl.ds(e, 1), :] = jnp.sum(acc.T, axis=0, keepdims=True)[:, :cap].astype(I32)
        return carry

    lax.fori_loop(0, ne, per_expert, 0)


def _topk(aff_t, cap):
    b, e, n = aff_t.shape
    return pl.pallas_call(
        functools.partial(_topk_body, cap=cap),
        grid=(b,),
        in_specs=[pl.BlockSpec((1, e, n), lambda bb: (bb, 0, 0))],
        out_specs=pl.BlockSpec((1, e, cap), lambda bb: (bb, 0, 0)),
        out_shape=jax.ShapeDtypeStruct((b, e, cap), I32),
        scratch_shapes=[pltpu.VMEM((n // LANES, e, LANES), F32)],
        compiler_params=_cparams(),
        name="expert_choice_topk",
    )(aff_t)


def _moe_body(idxp_ref, idxc_ref, idxn_ref, hp_ref, aff_ref, wg_ref, wu_ref, wd_ref, out_ref,
              xs_scr, ag_scr, y_scr, xlo_scr, xhi_scr, hid_scr, *, cap):
    e = pl.program_id(1)
    f = pl.program_id(2)
    ne = pl.num_programs(1)
    nf = pl.num_programs(2)
    cur = e % 2
    oth = 1 - cur

    @pl.when((pl.program_id(0) == 0) & (e == 0) & (f == 0))
    def _():
        y_scr[...] = jnp.zeros_like(y_scr)

    @pl.when((e == 0) & (f == 0))
    def _():
        out_ref[...] = jnp.zeros_like(out_ref)

        def gather(j, carry):
            for u in range(SUBLANES):
                t = idxc_ref[0, 0, 0, j * SUBLANES + u]
                xs_scr[0, j, u:u + 1, :] = hp_ref[0, pl.ds(t, 1), :]
                ag_scr[0, j, u:u + 1, :] = aff_ref[0, pl.ds(t, 1), :]
            return carry

        lax.fori_loop(0, cap // SUBLANES, gather, 0)

    w = xs_scr[cur].reshape(cap, xs_scr.shape[-1])
    half = w.shape[1]
    xlo_scr[...] = pltpu.bitcast(w << 16, F32).astype(BF16)
    xhi_scr[...] = pltpu.bitcast(w & jnp.uint32(0xFFFF0000), F32).astype(BF16)
    lane = lax.broadcasted_iota(I32, (cap, LANES), 1)
    gate = jnp.sum(jnp.where(lane == e, ag_scr[cur].reshape(cap, LANES), 0.0), axis=1, keepdims=True)
    share = cap // nf
    base = f * share
    base_tile = f * (share // SUBLANES)
    has_prev = e > 0

    def row_copies(r0, r1):
        for g0 in range(r0, r1, MOE_ROW_GROUP):
            rows = range(g0, min(g0 + MOE_ROW_GROUP, r1))
            tps = [idxp_ref[0, 0, 0, base + r] for r in rows]
            sums = [out_ref[0, pl.ds(tp, 1), :]
                    + jnp.where(has_prev, y_scr[oth, base_tile + r // SUBLANES, r % SUBLANES:r % SUBLANES + 1, :], 0.0)
                    for tp, r in zip(tps, rows)]
            for tp, v in zip(tps, sums):
                out_ref[0, pl.ds(tp, 1), :] = v
            for r in rows:
                tn = idxn_ref[0, 0, 0, base + r]
                j, u = base_tile + r // SUBLANES, r % SUBLANES
                xs_scr[oth, j, u:u + 1, :] = hp_ref[0, pl.ds(tn, 1), :]
                ag_scr[oth, j, u:u + 1, :] = aff_ref[0, pl.ds(tn, 1), :]

    tf = wg_ref.shape[3]
    nsl = tf // MOE_FF_SLICE
    sl = lambda c: slice(c * MOE_FF_SLICE, (c + 1) * MOE_FF_SLICE)

    def hidden(c):
        xlo = xlo_scr[...]
        xhi = xhi_scr[...]
        hg = (jnp.dot(xlo, wg_ref[0, 0, :half, sl(c)], preferred_element_type=F32)
              + jnp.dot(xhi, wg_ref[0, 0, half:, sl(c)], preferred_element_type=F32))
        hu = (jnp.dot(xlo, wu_ref[0, 0, :half, sl(c)], preferred_element_type=F32)
              + jnp.dot(xhi, wu_ref[0, 0, half:, sl(c)], preferred_element_type=F32))
        hid_scr[:, sl(c)] = (_silu(hg) * hu).astype(BF16)

    for c in range(nsl):
        hidden(c)
        row_copies(share * c // nsl, share * (c + 1) // nsl)
    yp = jnp.dot(hid_scr[...], wd_ref[0, 0], preferred_element_type=F32)
    y_old = y_scr[cur].reshape(cap, y_scr.shape[-1])
    y_new = (jnp.where(f == 0, 0.0, y_old) + yp) * jnp.where(f == nf - 1, gate, 1.0)
    y_scr[cur] = y_new.reshape(y_scr.shape[1:])

    @pl.when((e == ne - 1) & (f == nf - 1))
    def _():
        def scatter(j, carry):
            for u in range(SUBLANES):
                t = idxc_ref[0, 0, 0, j * SUBLANES + u]
                out_ref[0, pl.ds(t, 1), :] = out_ref[0, pl.ds(t, 1), :] + y_scr[cur, j, u:u + 1, :]
            return carry

        lax.fori_loop(0, cap // SUBLANES, scatter, 0)


def _moe(idx, hp, aff, wg, wu, wd, layer, tf):
    b, t, half = hp.shape
    d = 2 * half
    ne, ff = wg.shape[1], wg.shape[3]
    cap = idx.shape[-1]
    nf = ff // tf
    assert nf >= 2 and cap % nf == 0
    one = pl.Buffered(1)
    idx4 = idx.reshape(b, ne, 1, cap)
    smem = lambda shift: pl.BlockSpec((1, 1, 1, cap), lambda bb, e, f: (bb, jnp.clip(e + shift, 0, ne - 1), 0, 0),
                                      memory_space=pltpu.SMEM)
    return pl.pallas_call(
        functools.partial(_moe_body, cap=cap),
        grid=(b, ne, nf),
        in_specs=[
            smem(-1), smem(0), smem(1),
            pl.BlockSpec((1, t, half), lambda bb, e, f: (bb, 0, 0), pipeline_mode=one),
            pl.BlockSpec((1, t, LANES), lambda bb, e, f: (bb, 0, 0), pipeline_mode=one),
            pl.BlockSpec((1, 1, d, tf), lambda bb, e, f: (layer, e, 0, f)),
            pl.BlockSpec((1, 1, d, tf), lambda bb, e, f: (layer, e, 0, f)),
            pl.BlockSpec((1, 1, tf, d), lambda bb, e, f: (layer, e, f, 0)),
        ],
        out_specs=pl.BlockSpec((1, t, d), lambda bb, e, f: (bb, 0, 0), pipeline_mode=one),
        out_shape=jax.ShapeDtypeStruct((b, t, d), F32),
        scratch_shapes=[
            pltpu.VMEM((2, cap // SUBLANES, SUBLANES, half), jnp.uint32),
            pltpu.VMEM((2, cap // SUBLANES, SUBLANES, LANES), F32),
            pltpu.VMEM((2, cap // SUBLANES, SUBLANES, d), F32),
            pltpu.VMEM((cap, half), BF16),
            pltpu.VMEM((cap, half), BF16),
            pltpu.VMEM((cap, tf), BF16),
        ],
        compiler_params=_cparams(VMEM_LIMIT_MOE),
        name="expert_ffn",
    )(idx4, idx4, idx4, hp, aff, wg, wu, wd)


def _ln2_body(x1_ref, moe_ref, g2_ref, w_ref, b_ref, o_ref):
    o_ref[0] = _layer_norm(DEEPNORM_ALPHA * x1_ref[0] + g2_ref[0] * moe_ref[0], w_ref[...], b_ref[...])


def _ln2(x1, moe, modl, w, bias, ctx_row, off):
    b, t, d = x1.shape
    tl = SEQ_TILE
    seq = pl.BlockSpec((1, tl, d), lambda bb, i: (bb, i, 0))
    full = pl.BlockSpec((1, d), lambda bb, i: (0, 0))
    return pl.pallas_call(
        _ln2_body,
        grid=(b, t // tl),
        in_specs=[seq, seq, _mod_spec(5, d, ctx_row, off), full, full],
        out_specs=seq,
        out_shape=jax.ShapeDtypeStruct((b, t, d), F32),
        compiler_params=_cparams(),
        name="ln2",
    )(x1, moe, modl, w, bias)


def _block_diag_ones(width, block):
    r = jnp.arange(width)[:, None] // block
    c = jnp.arange(width)[None, :] // block
    return (r == c).astype(BF16)


def _block_diag(w):
    k, d, e = w.shape
    eye = jnp.eye(k, dtype=w.dtype)
    return (eye[:, None, :, None] * w[:, :, None, :]).reshape(k * d, k * e)


def _rope_tables(n_ctx, n_lat):
    pos = jnp.arange(n_lat)
    row = (pos // GRID_W).astype(F32)
    col = (pos % GRID_W).astype(F32)
    inv_freq = ROPE_THETA ** (-jnp.arange(0, AXIS_DIM, 2, dtype=F32) / AXIS_DIM)
    ang_r = row[:, None] * inv_freq
    ang_c = col[:, None] * inv_freq
    cos_h = jnp.concatenate([jnp.cos(ang_r)] * 2 + [jnp.cos(ang_c)] * 2, axis=1)
    sin_h = jnp.concatenate([-jnp.sin(ang_r), jnp.sin(ang_r), -jnp.sin(ang_c), jnp.sin(ang_c)], axis=1)
    reps = LANES // HEAD_DIM
    cos_t = jnp.concatenate([jnp.ones((n_ctx, LANES), F32), jnp.tile(cos_h, (1, reps))], axis=0)
    sin_t = jnp.concatenate([jnp.zeros((n_ctx, LANES), F32), jnp.tile(sin_h, (1, reps))], axis=0)
    return cos_t, sin_t


def _layer(xt, modl, rope, p, experts, layer, last, n_ctx, ctx_row):
    b, t, d = xt.shape
    n_lat = t - n_ctx
    cos_t, sin_t = rope
    w = p["w_in"]
    pad = jnp.zeros((d, LANES - SSD_HEADS), F32)
    wcat = jnp.concatenate(
        [w[:, 0:768], w[:, 768:1280], w[:, 1544:1800], w[:, 1280:1536], w[:, 1800:2056],
         w[:, 1536:1540], pad, w[:, 1540:1544], pad], axis=1).astype(BF16)
    qw = jnp.tile(p["q_norm"], W_Q // HEAD_DIM)[None]
    kw = jnp.tile(p["k_norm"], W_K // HEAD_DIM)[None]
    conv_w = jnp.concatenate([p["ssd_conv_w"], p["rg_conv_w"]], axis=1)
    conv_b = jnp.concatenate([p["ssd_conv_b"], p["rg_conv_b"]])[None]
    qt, k, v, xbc, u, z, g, dt = _inproj(xt, modl, wcat, cos_t, sin_t, qw, kw,
                                         _block_diag_ones(W_Q, HEAD_DIM), _block_diag_ones(W_K, HEAD_DIM),
                                         conv_w, conv_b, p["ssd_conv_w"].shape[1], ctx_row)

    qt = qt.reshape(b, W_Q // HEAD_DIM, HEAD_DIM, t)
    kh = k.reshape(b, t, KV_HEADS, HEAD_DIM).transpose(0, 2, 1, 3)
    vt = v.reshape(b, t, KV_HEADS, HEAD_DIM).transpose(0, 2, 3, 1)
    ones_pad = jnp.zeros((b, KV_HEADS, BF16_ROWS, t), BF16).at[:, :, 0, :].set(1.0)
    vt = jnp.concatenate([vt, ones_pad], axis=2)
    attn_lat = _attention(qt[..., n_ctx:], kh, vt, n_lat, 0, t, ATTN_Q_TILE)
    attn_ctx = None if last else _attention(qt, kh, vt, n_ctx, 0, n_ctx, min(ATTN_Q_TILE, n_ctx))

    lane_pad = lambda a: jnp.pad(a, ((0, 0), (0, LANES - a.shape[1])))[:, None, :]
    ys = _ssd(xbc, dt, lane_pad(p["ssd_dt_bias"]), lane_pad(-jnp.exp(p["ssd_a_log"])),
              jnp.repeat(p["ssd_d"], SSD_HEAD_DIM)[None])
    wgate = jnp.stack([jnp.concatenate([_block_diag(p["rg_wa"][j]), _block_diag(p["rg_wx"][j])], axis=1)
                       for j in range(2)]).astype(BF16)
    bgate = jnp.concatenate([p["rg_ba"], p["rg_bx"]], axis=1)[:, None, :]
    hr = _rglru(u, wgate, bgate, p["rg_lambda"][:, None, :])

    off = 1 if last else 0
    wr = jnp.pad(p["w_router"], ((0, 0), (0, LANES - N_EXPERTS))).astype(BF16)
    x1, hp, aff, aff_t = _merge(attn_ctx, attn_lat, ys, z, hr, g, xt, modl,
                                p["attn_out_norm"][None], p["ssd_norm"][None], p["rg_out_norm"][None],
                                p["w_out"].astype(BF16), p["ln1_w"][None], p["ln1_b"][None], wr, ctx_row, off)

    if last:
        idx = _topk(aff_t, EC_CAPACITY * n_lat // N_EXPERTS)
    else:
        idx_lat = _topk(aff_t[:, :, n_ctx:], EC_CAPACITY * n_lat // N_EXPERTS) + n_ctx
        idx_ctx = _topk(aff_t[:, :, :n_ctx], EC_CAPACITY * n_ctx // N_EXPERTS)
        idx = jnp.concatenate([idx_lat, idx_ctx], axis=-1)
    moe = _moe(idx, hp, aff, *experts, layer, MOE_FF_TILE)
    return _ln2(x1, moe, modl, p["ln2_w"][None], p["ln2_b"][None], ctx_row, off)


def kernel(x, c, ctx, c_ctx, w_mod, b_mod, w_in, q_norm, k_norm, attn_out_norm, ssd_conv_w, ssd_conv_b, ssd_dt_bias, ssd_a_log, ssd_d, ssd_norm, rg_conv_w, rg_conv_b, rg_wa, rg_ba, rg_wx, rg_bx, rg_lambda, rg_out_norm, w_out, ln1_w, ln1_b, w_router, w_gate, w_up, w_down, ln2_w, ln2_b):
    b, n_lat, d = x.shape
    n_ctx = ctx.shape[1]
    assert n_ctx == SEQ_TILE and n_lat % 512 == 0 and b == SUBLANES
    params = dict(w_in=w_in, q_norm=q_norm, k_norm=k_norm, attn_out_norm=attn_out_norm, ssd_conv_w=ssd_conv_w,
                  ssd_conv_b=ssd_conv_b, ssd_dt_bias=ssd_dt_bias, ssd_a_log=ssd_a_log, ssd_d=ssd_d,
                  ssd_norm=ssd_norm, rg_conv_w=rg_conv_w, rg_conv_b=rg_conv_b, rg_wa=rg_wa, rg_ba=rg_ba,
                  rg_wx=rg_wx, rg_bx=rg_bx, rg_lambda=rg_lambda, rg_out_norm=rg_out_norm, w_out=w_out,
                  ln1_w=ln1_w, ln1_b=ln1_b, w_router=w_router, ln2_w=ln2_w, ln2_b=ln2_b)
    experts = (w_gate.astype(BF16), w_up.astype(BF16), w_down.astype(BF16))
    mod = _modulation(c, c_ctx, w_mod, b_mod)
    rope = _rope_tables(n_ctx, n_lat)
    xt = jnp.concatenate([ctx, x], axis=1)
    depth = w_mod.shape[0]
    for l in range(depth):
        p = {name: val[l] for name, val in params.items()}
        modl = mod[l].reshape(mod.shape[1], 1, mod.shape[2])
        xt = _layer(xt, modl, rope, p, experts, l, l == depth - 1, n_ctx, b)
    return xt
```

```python
import functools
import math

import jax
import jax.numpy as jnp
from jax import lax
from jax.experimental import pallas as pl
from jax.experimental.pallas import tpu as pltpu

F32 = jnp.float32
BF16 = jnp.bfloat16
I32 = jnp.int32

DEPTH = 2
GRID_W = 64
HEAD_DIM = 64
KV_HEADS = 2
Q_PER_KV = 4
AXIS_DIM = HEAD_DIM // 2
ROPE_THETA = 10000.0
SSD_HEADS = 4
SSD_HEAD_DIM = 64
SSD_STATE = 64
RG_BLOCKS = 4
RG_C = 8.0
CONV_W = 4
N_EXPERTS = 16
EC_CAPACITY = 2
EPS = 1e-6
DEEPNORM_ALPHA = (2 * DEPTH) ** 0.25

LANES = 128
SUBLANES = 8
BF16_ROWS = 16
SEQ_TILE = 256
ATTN_Q_TILE = 512
ATTN_KEY_CHUNK = 256
MERGE_ROW_SPLIT = 2
SSD_BATCH = 4
MOE_FF_TILE = 1024
MOE_FF_SLICE = 256
MOE_ROW_GROUP = 4
LOG2E = math.log2(math.e)
VMEM_LIMIT = 48 * 1024 * 1024
VMEM_LIMIT_MOE = 60 * 1024 * 1024


def _cparams(limit=VMEM_LIMIT):
    return pltpu.CompilerParams(vmem_limit_bytes=limit)


def _sigmoid(x):
    return 1.0 / (1.0 + jnp.exp(-x))


def _silu(x):
    return x * _sigmoid(x)


def _softplus(x):
    return jnp.maximum(x, 0.0) + jnp.log1p(jnp.exp(-jnp.abs(x)))


def _bdot(a, b):
    return jnp.dot(a.astype(BF16), b.astype(BF16), preferred_element_type=F32)


def _mod_body(c_ref, w_ref, b_ref, o_ref):
    c = c_ref[...]
    o_ref[0] = _bdot(_silu(c), w_ref[0]) + b_ref[0]


def _modulation(c, c_ctx, w_mod, b_mod):
    depth, d, n6 = w_mod.shape
    b = c.shape[0]
    rows = 2 * SUBLANES
    cc = jnp.zeros((rows, d), F32).at[:b].set(c).at[b].set(c_ctx)
    tn = 1536
    return pl.pallas_call(
        _mod_body,
        grid=(depth, n6 // tn),
        in_specs=[
            pl.BlockSpec((rows, d), lambda l, j: (0, 0)),
            pl.BlockSpec((1, d, tn), lambda l, j: (l, 0, j)),
            pl.BlockSpec((1, 1, tn), lambda l, j: (l, 0, j)),
        ],
        out_specs=pl.BlockSpec((1, rows, tn), lambda l, j: (l, 0, j)),
        out_shape=jax.ShapeDtypeStruct((depth, rows, n6), F32),
        compiler_params=_cparams(),
        name="adaln_mod",
    )(cc, w_mod, b_mod.reshape(depth, 1, n6))


def _mod_spec(j, d, ctx_row, off):
    if off == 0:
        return pl.BlockSpec((1, 1, d), lambda b, i: (jnp.where(i == 0, ctx_row, b), 0, j))
    return pl.BlockSpec((1, 1, d), lambda b, i: (b, 0, j))


W_Q, W_K, W_V, W_XU, W_Z, W_G, W_DT = 512, 128, 128, 768, 256, 256, 256
IN_PAD = W_Q + W_K + W_V + W_XU + W_Z + W_G + W_DT


def _rope(t, cos, sin_signed):
    rows = t.shape[0]
    lane = lax.broadcasted_iota(I32, (rows, LANES), 1)
    first = (lane % AXIS_DIM) < (AXIS_DIM // 2)
    outs = []
    for c in range(t.shape[1] // LANES):
        tc = t[:, c * LANES:(c + 1) * LANES]
        partner = jnp.where(first, pltpu.roll(tc, LANES - AXIS_DIM // 2, 1), pltpu.roll(tc, AXIS_DIM // 2, 1))
        outs.append(tc * cos + partner * sin_signed)
    return outs[0] if len(outs) == 1 else jnp.concatenate(outs, axis=1)


def _inproj_body(x_ref, xp_ref, xn_ref, sc_ref, sh_ref, w_ref, cos_ref, sin_ref, qw_ref, kw_ref, oq_ref, ok_ref,
                 cw_ref, cb_ref, q_ref, k_ref, v_ref, xbc_ref, u_ref, z_ref, g_ref, dt_ref, qkv_scr):
    i = pl.program_id(1)
    n = pl.num_programs(1)
    scale = 1.0 + sc_ref[0]
    shift = sh_ref[0]
    h = (x_ref[0] * scale + shift).astype(BF16)
    n_qkv = W_Q + W_K + W_V
    qkv_scr[...] = jnp.dot(h, w_ref[:, :n_qkv], preferred_element_type=F32)
    rest = jnp.dot(h, w_ref[:, n_qkv:], preferred_element_type=F32)
    halo = (jnp.concatenate([xp_ref[0], xn_ref[0]], axis=0) * scale + shift).astype(BF16)
    xu_halo = jnp.dot(halo, w_ref[:, n_qkv:n_qkv + W_XU], preferred_element_type=F32)
    xu = rest[:, :W_XU]
    z_ref[0] = rest[:, W_XU:W_XU + W_Z]
    g_ref[0] = rest[:, W_XU + W_Z:W_XU + W_Z + W_G]
    dt_ref[0] = rest[:, W_XU + W_Z + W_G:]

    tl = xu.shape[0]
    has_prev = i > 1
    has_next = (i > 0) & (i < n - 1)
    pm = jnp.where(has_prev, xu_halo[SUBLANES - 1:SUBLANES, :], 0.0)
    n0 = jnp.where(has_next, xu_halo[SUBLANES:SUBLANES + 1, :], 0.0)
    n1 = jnp.where(has_next, xu_halo[SUBLANES + 1:SUBLANES + 2, :], 0.0)
    row = lax.broadcasted_iota(I32, xu.shape, 0)
    xm1 = jnp.where(row == 0, pm, pltpu.roll(xu, 1, 0))
    xp1 = jnp.where(row == tl - 1, n0, pltpu.roll(xu, tl - 1, 0))
    xp2 = jnp.where(row == tl - 1, n1, jnp.where(row == tl - 2, n0, pltpu.roll(xu, tl - 2, 0)))
    cw = cw_ref[...]
    yc = xm1 * cw[0:1] + xu * cw[1:2] + xp1 * cw[2:3] + xp2 * cw[3:4] + cb_ref[...]
    n_xbc = xbc_ref.shape[-1]
    xbc_ref[0] = _silu(yc[:, :n_xbc])
    u_ref[0] = yc[:, n_xbc:]

    q = qkv_scr[:, :W_Q]
    k = qkv_scr[:, W_Q:W_Q + W_K]
    cos = cos_ref[...]
    sin = sin_ref[...]
    ssq = jnp.dot((q * q).astype(BF16), oq_ref[...], preferred_element_type=F32)
    qn = q * lax.rsqrt(ssq * (1.0 / HEAD_DIM) + EPS) * qw_ref[...]
    q_ref[0] = (_rope(qn, cos, sin) * (HEAD_DIM ** -0.5 * LOG2E)).T.astype(BF16)
    ssk = jnp.dot((k * k).astype(BF16), ok_ref[...], preferred_element_type=F32)
    kn = k * lax.rsqrt(ssk * (1.0 / HEAD_DIM) + EPS) * kw_ref[...]
    k_ref[0] = _rope(kn, cos, sin).astype(BF16)
    v_ref[0] = qkv_scr[:, W_Q + W_K:].astype(BF16)


def _inproj(xt, modl, wcat, cos_t, sin_t, qw, kw, ones_q, ones_k, conv_w, conv_b, n_xbc, ctx_row):
    b, t, d = xt.shape
    tl = SEQ_TILE
    grid = (b, t // tl)
    r = tl // SUBLANES
    nblk = t // SUBLANES
    full = lambda shape: pl.BlockSpec(shape, lambda bb, i: (0,) * len(shape))
    seq = lambda w: pl.BlockSpec((1, tl, w), lambda bb, i: (bb, i, 0))
    outs = [(W_K, BF16), (W_V, BF16), (n_xbc, F32), (W_XU - n_xbc, F32), (W_Z, F32), (W_G, F32), (W_DT, F32)]
    return pl.pallas_call(
        _inproj_body,
        grid=grid,
        in_specs=[
            seq(d),
            pl.BlockSpec((1, SUBLANES, d), lambda bb, i: (bb, jnp.maximum(i * r - 1, 0), 0)),
            pl.BlockSpec((1, SUBLANES, d), lambda bb, i: (bb, jnp.minimum((i + 1) * r, nblk - 1), 0)),
            _mod_spec(1, d, ctx_row, 0),
            _mod_spec(0, d, ctx_row, 0),
            full((d, IN_PAD)),
            pl.BlockSpec((tl, LANES), lambda bb, i: (i, 0)),
            pl.BlockSpec((tl, LANES), lambda bb, i: (i, 0)),
            full((1, W_Q)), full((1, W_K)), full((W_Q, W_Q)), full((W_K, W_K)),
            full((CONV_W, W_XU)), full((1, W_XU)),
        ],
        out_specs=[pl.BlockSpec((1, W_Q, tl), lambda bb, i: (bb, 0, i))] + [seq(w) for w, _ in outs],
        out_shape=[jax.ShapeDtypeStruct((b, W_Q, t), BF16)]
        + [jax.ShapeDtypeStruct((b, t, w), dt) for w, dt in outs],
        scratch_shapes=[pltpu.VMEM((tl, W_Q + W_K + W_V), F32)],
        compiler_params=_cparams(),
        name="in_proj",
    )(xt, xt, xt, modl, modl, wcat, cos_t, sin_t, qw, kw, ones_q, ones_k, conv_w, conv_b)


def _attn_body(qt_ref, k_ref, vt_ref, o_ref, s_buf, p_buf, acc_ref, *, n_keys, kc):
    nh, hd, tq = qt_ref.shape[1:]
    qt = jnp.concatenate([qt_ref[0, h] for h in range(nh)], axis=1)
    cols = nh * tq
    rem = n_keys % kc
    chunks = ([(0, rem)] if rem else []) + [(rem + kc * i, kc) for i in range(n_keys // kc)]
    n = len(chunks)

    def scores(c):
        s0, sz = chunks[c]
        s_buf[c % 2, :sz, :] = jnp.dot(k_ref[0, 0, s0:s0 + sz, :], qt, preferred_element_type=F32)

    def softmax(c, m):
        sz = chunks[c][1]
        s = s_buf[c % 2, :sz, :]
        m_new = jnp.maximum(m, jnp.max(s, axis=0, keepdims=True))
        p_buf[c % 2, :sz, :] = jnp.exp2(s - m_new).astype(BF16)
        return m_new, jnp.exp2(m - m_new)

    def weighted_values(c, alpha):
        s0, sz = chunks[c]
        acc_ref[...] = alpha * acc_ref[...] + jnp.dot(vt_ref[0, 0, :, s0:s0 + sz], p_buf[c % 2, :sz, :],
                                                      preferred_element_type=F32)

    acc_ref[...] = jnp.zeros_like(acc_ref)
    m = jnp.full((1, cols), -jnp.inf, F32)
    alpha = None
    scores(0)
    for j in range(n + 1):
        if j + 1 < n:
            scores(j + 1)
        prev_alpha = alpha
        if j < n:
            m, alpha = softmax(j, m)
        if j >= 1:
            weighted_values(j - 1, prev_alpha)
    o = acc_ref[:hd, :] * (1.0 / acc_ref[hd:hd + 1, :])
    for h in range(nh):
        o_ref[0, h * hd:(h + 1) * hd, :] = o[:, h * tq:(h + 1) * tq]


def _attention(qt, k, vt, n_q, q_off, n_keys, tq):
    b = qt.shape[0]
    assert q_off % tq == 0
    off = q_off // tq
    kc = min(ATTN_KEY_CHUNK, n_keys)
    cols = Q_PER_KV * tq
    vrows = vt.shape[2]
    return pl.pallas_call(
        functools.partial(_attn_body, n_keys=n_keys, kc=kc),
        grid=(b, KV_HEADS, n_q // tq),
        scratch_shapes=[pltpu.VMEM((2, kc, cols), F32), pltpu.VMEM((2, kc, cols), BF16),
                        pltpu.VMEM((vrows, cols), F32)],
        in_specs=[
            pl.BlockSpec((1, Q_PER_KV, HEAD_DIM, tq), lambda bb, g, i: (bb, g, 0, i + off)),
            pl.BlockSpec((1, 1, n_keys, HEAD_DIM), lambda bb, g, i: (bb, g, 0, 0)),
            pl.BlockSpec((1, 1, vrows, n_keys), lambda bb, g, i: (bb, g, 0, 0)),
        ],
        out_specs=pl.BlockSpec((1, Q_PER_KV * HEAD_DIM, tq), lambda bb, g, i: (bb, g, i)),
        out_shape=jax.ShapeDtypeStruct((b, KV_HEADS * Q_PER_KV * HEAD_DIM, n_q), F32),
        compiler_params=_cparams(),
        name="gqa_attention",
    )(qt, k, vt)


def _seq_order(d, c, nc):
    return jnp.where(c == 0, 0, jnp.where(d == 0, c, nc - c))


def _ssd_body(xf_ref, xb_ref, dtf_ref, dtb_ref, bias_ref, a_ref, dvec_ref, yf_ref, yb_ref, s_scr):
    @pl.when(pl.program_id(1) == 0)
    def _():
        s_scr[...] = jnp.zeros_like(s_scr)

    nb = xf_ref.shape[0]
    streams = range(2 * nb)
    xbc = [(xf_ref, xb_ref)[k % 2][k // 2] for k in streams]
    q = xbc[0].shape[0]
    xw = SSD_HEADS * SSD_HEAD_DIM
    gw = 2 * SSD_STATE
    x = [v[:, :xw] for v in xbc]
    bm = [v[:, xw:xw + gw] for v in xbc]
    cm = [v[:, xw + gw:xw + 2 * gw] for v in xbc]
    dt_raw = [(dtf_ref, dtb_ref)[k % 2][k // 2] for k in streams]
    dt = [_softplus(dt_raw[k] + bias_ref[k % 2]) for k in streams]
    da = [dt[k] * a_ref[k % 2] for k in streams]
    ii = lax.broadcasted_iota(I32, (q, q), 0)
    jj = lax.broadcasted_iota(I32, (q, q), 1)
    mask = [jj <= ii, jj >= ii]
    tm = [jnp.where(m_, 1.0, 0.0).astype(BF16) for m_ in mask]
    da_hi = [da[k].astype(BF16) for k in streams]
    da_lo = [(da[k] - da_hi[k].astype(F32)).astype(BF16) for k in streams]
    cs = [jnp.dot(tm[k % 2], da_hi[k], preferred_element_type=F32)
          + jnp.dot(tm[k % 2], da_lo[k], preferred_element_type=F32) for k in streams]
    tot = [cs[k][q - 1:q, :] if k % 2 == 0 else cs[k][0:1, :] for k in streams]
    cst = [cs[k].T for k in streams]
    dec = [jnp.exp(tot[k] - cs[k]) for k in streams]
    ecs = [jnp.exp(cs[k]) for k in streams]
    etot = [jnp.exp(tot[k]) for k in streams]
    ys = [[] for _ in streams]
    nt = (((1,), (1,)), ((), ()))
    tn = (((0,), (0,)), ((), ()))
    for g in range(2):
        gs = slice(g * SSD_STATE, (g + 1) * SSD_STATE)
        bg = [bm[k][:, gs].astype(BF16) for k in streams]
        cg = [cm[k][:, gs].astype(BF16) for k in streams]
        gmat = [lax.dot_general(cg[k], bg[k], nt, preferred_element_type=F32) for k in streams]
        for hh in range(SSD_HEADS // 2):
            h = 2 * g + hh
            hs = slice(h * SSD_HEAD_DIM, (h + 1) * SSD_HEAD_DIM)
            lmat = [jnp.exp(jnp.where(mask[k % 2], cs[k][:, h:h + 1] - cst[k][h:h + 1, :], -jnp.inf))
                    for k in streams]
            xdt = [x[k][:, hs] * dt[k][:, h:h + 1] for k in streams]
            y_diag = [_bdot(gmat[k] * lmat[k], xdt[k]) for k in streams]
            s_in = [s_scr[k, h] for k in streams]
            y_off = [lax.dot_general(cg[k], s_in[k].astype(BF16), nt, preferred_element_type=F32) * ecs[k][:, h:h + 1]
                     for k in streams]
            xd = [(xdt[k] * dec[k][:, h:h + 1]).astype(BF16) for k in streams]
            for k in streams:
                ys[k].append(y_diag[k] + y_off[k])
                s_scr[k, h] = etot[k][:, h:h + 1] * s_in[k] + lax.dot_general(xd[k], bg[k], tn,
                                                                               preferred_element_type=F32)
    for k in streams:
        if k % 2 == 0:
            yf_ref[k // 2] = jnp.concatenate(ys[k], axis=1) + x[k] * dvec_ref[...]
        else:
            yb_ref[k // 2] = jnp.concatenate(ys[k], axis=1)


def _ssd(xbc, dt, bias, a_neg, dvec):
    b, t, cw = xbc.shape
    q = SEQ_TILE
    nc = t // q
    xw = SSD_HEADS * SSD_HEAD_DIM
    fwd = lambda c: c
    bwd = lambda c: _seq_order(1, c, nc)
    nb = SSD_BATCH
    return pl.pallas_call(
        _ssd_body,
        grid=(b // nb, nc),
        in_specs=[
            pl.BlockSpec((nb, q, cw), lambda bb, c: (bb, fwd(c), 0)),
            pl.BlockSpec((nb, q, cw), lambda bb, c: (bb, bwd(c), 0)),
            pl.BlockSpec((nb, q, LANES), lambda bb, c: (bb, fwd(c), 0)),
            pl.BlockSpec((nb, q, LANES), lambda bb, c: (bb, bwd(c), 1)),
            pl.BlockSpec((2, 1, LANES), lambda bb, c: (0, 0, 0)),
            pl.BlockSpec((2, 1, LANES), lambda bb, c: (0, 0, 0)),
            pl.BlockSpec((1, xw), lambda bb, c: (0, 0)),
        ],
        out_specs=[pl.BlockSpec((nb, q, xw), lambda bb, c: (bb, fwd(c), 0)),
                   pl.BlockSpec((nb, q, xw), lambda bb, c: (bb, bwd(c), 0))],
        out_shape=[jax.ShapeDtypeStruct((b, t, xw), F32)] * 2,
        scratch_shapes=[pltpu.VMEM((2 * nb, SSD_HEADS, SSD_HEAD_DIM, SSD_STATE), F32)],
        compiler_params=_cparams(),
        name="ssd_scan",
    )(xbc, xbc, dt, dt, bias, a_neg, dvec)


def _rg_body(u_ref, w_ref, bias_ref, lam_ref, y_ref, a_s, v_s, o_s, h_s, *, pitch):
    d = pl.program_id(0)
    c = pl.program_id(1)
    nb, tl, width = u_ref.shape
    ng = width // LANES

    @pl.when(c == 0)
    def _():
        h_s[...] = jnp.zeros_like(h_s)

    sp = _softplus(-lam_ref[0])
    w = w_ref[0]
    bias = bias_ref[0]
    for b in range(nb):
        ub = u_ref[b]
        pre = jnp.dot(ub.astype(BF16), w, preferred_element_type=F32) + bias
        r = _sigmoid(pre[:, :width])
        ig = _sigmoid(pre[:, width:])
        a = jnp.exp((-RG_C) * r * sp)
        v = jnp.sqrt(1.0 - a * a) * ig * ub
        for j in range(ng):
            a_s[j, pl.ds(b * pitch, tl), :] = a[:, j * LANES:(j + 1) * LANES]
            v_s[j, pl.ds(b * pitch, tl), :] = v[:, j * LANES:(j + 1) * LANES]

    def step(t, hs):
        te = jnp.where(d == 0, t, tl - 1 - t)
        out = []
        for j in range(ng):
            at = a_s[j, pl.ds(te, nb, stride=pitch), :]
            vt = v_s[j, pl.ds(te, nb, stride=pitch), :]
            hj = at * hs[j] + vt
            o_s[j, pl.ds(te, nb, stride=pitch), :] = hj
            out.append(hj)
        return tuple(out)

    h0 = tuple(h_s[:, j * LANES:(j + 1) * LANES] for j in range(ng))
    hf = lax.fori_loop(0, tl, step, h0, unroll=8)
    for j in range(ng):
        h_s[:, j * LANES:(j + 1) * LANES] = hf[j]
    for b in range(nb):
        y_ref[0, b] = jnp.concatenate([o_s[j, pl.ds(b * pitch, tl), :] for j in range(ng)], axis=1)


def _rglru(u, wg, bias, lam):
    b, t, width = u.shape
    assert b == SUBLANES, "the recurrence keeps one sample per sublane"
    tl = SEQ_TILE
    nc = t // tl
    pitch = tl + SUBLANES
    ng = width // LANES
    slab = pltpu.VMEM((ng, b * pitch, LANES), F32)
    return pl.pallas_call(
        functools.partial(_rg_body, pitch=pitch),
        grid=(2, nc),
        in_specs=[
            pl.BlockSpec((b, tl, width), lambda d, c: (0, _seq_order(d, c, nc), 0)),
            pl.BlockSpec((1, width, 2 * width), lambda d, c: (d, 0, 0)),
            pl.BlockSpec((1, 1, 2 * width), lambda d, c: (d, 0, 0)),
            pl.BlockSpec((1, 1, width), lambda d, c: (d, 0, 0)),
        ],
        out_specs=pl.BlockSpec((1, b, tl, width), lambda d, c: (d, 0, _seq_order(d, c, nc), 0)),
        out_shape=jax.ShapeDtypeStruct((2, b, t, width), F32),
        scratch_shapes=[slab, slab, slab, pltpu.VMEM((b, width), F32)],
        compiler_params=_cparams(),
        name="rglru_scan",
    )(u, wg, bias, lam)


def _rms(x, w):
    return x * lax.rsqrt(jnp.mean(x * x, axis=-1, keepdims=True) + EPS) * w


def _layer_norm(t, w, b):
    mu = jnp.mean(t, axis=-1, keepdims=True)
    tc = t - mu
    var = jnp.mean(tc * tc, axis=-1, keepdims=True)
    return tc * lax.rsqrt(var + EPS) * w + b


def _gelu_tanh(x):
    return 0.5 * x * (1.0 + jnp.tanh(math.sqrt(2.0 / math.pi) * (x + 0.044715 * (x * x * x))))


def _pack_bf16_pair(lo, hi):
    lb = pltpu.bitcast(lo.astype(BF16).astype(F32), jnp.uint32)
    hb = pltpu.bitcast(hi.astype(BF16).astype(F32), jnp.uint32)
    return (lb >> 16) | (hb & jnp.uint32(0xFFFF0000))


def _merge_body(*refs, has_ctx):
    if has_ctx:
        actx_ref, refs = refs[0], refs[1:]
    (alat_ref, ysf_ref, ysb_ref, z_ref, hr_ref, g_ref, x_ref, g1_ref, sc2_ref, sh2_ref, aw_ref, sw_ref, rw_ref,
     wo_ref, lnw_ref, lnb_ref, wr_ref, x1_ref, hp_ref, aff_ref, afft_ref, cat_scr, proj_scr) = refs
    tl = x_ref.shape[1]
    nh = MERGE_ROW_SPLIT
    rows = [slice(j * tl // nh, (j + 1) * tl // nh) for j in range(nh)]
    for rs in rows:
        a = alat_ref[0, :, rs]
        if has_ctx:
            a = jnp.where(pl.program_id(1) == 0, actx_ref[0, :, rs], a)
        an = _rms(a.T, aw_ref[...])
        sn = _rms((ysf_ref[0, rs] + ysb_ref[0, rs]) * _silu(z_ref[0, rs]), sw_ref[...])
        rn = _rms((hr_ref[0, 0, rs] + hr_ref[1, 0, rs]) * _gelu_tanh(g_ref[0, rs]), rw_ref[...])
        cat_scr[rs] = jnp.concatenate([an, sn, rn], axis=1).astype(BF16)
    for rs in rows:
        proj_scr[rs] = jnp.dot(cat_scr[rs], wo_ref[...], preferred_element_type=F32)
    for rs in rows:
        x1 = _layer_norm(DEEPNORM_ALPHA * x_ref[0, rs] + g1_ref[0] * proj_scr[rs], lnw_ref[...], lnb_ref[...])
        x1_ref[0, rs] = x1
        h2 = x1 * (1.0 + sc2_ref[0]) + sh2_ref[0]
        half = h2.shape[1] // 2
        hp_ref[0, rs] = _pack_bf16_pair(h2[:, :half], h2[:, half:])
        logits = jnp.dot(h2.astype(BF16), wr_ref[...], preferred_element_type=F32)
        lane = lax.broadcasted_iota(I32, logits.shape, 1)
        logits = jnp.where(lane < N_EXPERTS, logits, -jnp.inf)
        e = jnp.exp(logits - jnp.max(logits, axis=-1, keepdims=True))
        aff = e / jnp.sum(e, axis=-1, keepdims=True)
        aff_ref[0, rs] = aff
        afft_ref[0, :, rs] = aff.T[:N_EXPERTS, :]


def _merge(attn_ctx, attn_lat, ys, z, hr, g, xt, modl, aw, sw, rw, wo, lnw, lnb, wr, ctx_row, off):
    b, t, d = xt.shape
    tl = SEQ_TILE
    nt = t // tl - off
    has_ctx = off == 0
    aw_ = attn_lat.shape[1]
    seq = lambda w: pl.BlockSpec((1, tl, w), lambda bb, i: (bb, i + off, 0))
    pair = lambda w: pl.BlockSpec((2, 1, tl, w), lambda bb, i: (0, bb, i + off, 0))
    full = lambda shape: pl.BlockSpec(shape, lambda bb, i: (0,) * len(shape))
    out = lambda w: pl.BlockSpec((1, tl, w), lambda bb, i: (bb, i, 0))
    lat_off = 1 - off
    in_specs = [
        pl.BlockSpec((1, aw_, tl), lambda bb, i: (bb, 0, jnp.maximum(i - lat_off, 0))),
        seq(ys[0].shape[-1]), seq(ys[1].shape[-1]), seq(z.shape[-1]), pair(hr.shape[-1]), seq(g.shape[-1]), seq(d),
        _mod_spec(2, d, ctx_row, off), _mod_spec(4, d, ctx_row, off), _mod_spec(3, d, ctx_row, off),
        full(aw.shape), full(sw.shape), full(rw.shape), full(wo.shape), full(lnw.shape), full(lnb.shape),
        full(wr.shape),
    ]
    args = [attn_lat, ys[0], ys[1], z, hr, g, xt, modl, modl, modl, aw, sw, rw, wo, lnw, lnb, wr]
    if has_ctx:
        in_specs = [pl.BlockSpec((1, aw_, tl), lambda bb, i: (bb, 0, 0))] + in_specs
        args = [attn_ctx] + args
    rows = nt * tl
    return pl.pallas_call(
        functools.partial(_merge_body, has_ctx=has_ctx),
        grid=(b, nt),
        in_specs=in_specs,
        out_specs=[out(d), out(d // 2), out(LANES), pl.BlockSpec((1, N_EXPERTS, tl), lambda bb, i: (bb, 0, i))],
        out_shape=[
            jax.ShapeDtypeStruct((b, rows, d), F32),
            jax.ShapeDtypeStruct((b, rows, d // 2), jnp.uint32),
            jax.ShapeDtypeStruct((b, rows, LANES), F32),
            jax.ShapeDtypeStruct((b, N_EXPERTS, rows), F32),
        ],
        scratch_shapes=[pltpu.VMEM((tl, wo.shape[0]), BF16), pltpu.VMEM((tl, d), F32)],
        compiler_params=_cparams(),
        name="merge_outproj_ln1_router",
    )(*args)


def _topk_body(aff_ref, idx_ref, cum_scr, *, cap):
    aff = aff_ref[0]
    ne, n = aff.shape
    bits = pltpu.bitcast(aff, I32)

    def search(i, thr):
        cand = thr | lax.shift_left(jnp.int32(1), 30 - i)
        cnt = jnp.sum(jnp.where(bits >= cand, 1.0, 0.0), axis=1, keepdims=True)
        return jnp.where(cnt >= cap, cand, thr)

    thr = lax.fori_loop(0, 31, search, jnp.zeros((ne, 1), I32))
    gt = jnp.where(bits > thr, 1.0, 0.0)
    eq = jnp.where(bits == thr, 1.0, 0.0)
    need = cap - jnp.sum(gt, axis=1, keepdims=True)
    nblk = n // LANES
    r_ = lax.broadcasted_iota(I32, (LANES, LANES), 0)
    c_ = lax.broadcasted_iota(I32, (LANES, LANES), 1)
    upper = jnp.where(r_ <= c_, 1.0, 0.0).astype(BF16)

    def prefix(blocks):
        outs = []
        off = jnp.zeros((ne, 1), F32)
        for mk in blocks:
            w = jnp.dot(mk.astype(BF16), upper, preferred_element_type=F32) + off
            outs.append(w)
            off = w[:, LANES - 1:LANES]
        return outs

    blk = lambda a, k: a[:, k * LANES:(k + 1) * LANES]
    tie_rank = prefix([blk(eq, k) for k in range(nblk)])
    cum = prefix([jnp.maximum(blk(gt, k), blk(eq, k) * jnp.where(tie_rank[k] <= need, 1.0, 0.0))
                  for k in range(nblk)])
    kpad = cum_scr.shape[0] // ne
    for k in range(kpad):
        cum_scr[k * ne:(k + 1) * ne, :] = cum[k] if k < nblk else jnp.zeros((ne, LANES), F32)
    width = -(-cap // LANES) * LANES
    slot = lax.broadcasted_iota(I32, (LANES, width), 1).astype(F32)
    blk_id = lax.broadcasted_iota(I32, (LANES, width), 0).astype(F32)
    blk_col = lax.broadcasted_iota(I32, (LANES, 1), 0)
    tn = (((0,), (0,)), ((), ()))

    def per_expert(e, carry):
        cum_e = cum_scr[pl.ds(e, kpad, stride=ne), :]
        if kpad < LANES:
            cum_e = jnp.concatenate([cum_e, jnp.zeros((LANES - kpad, LANES), F32)], axis=0)
        ends = jnp.where(blk_col < nblk, cum_e[:, LANES - 1:LANES], float(2 * n))
        full = jnp.sum(jnp.where(ends <= slot, 1.0, 0.0), axis=0, keepdims=True)
        pick = jnp.where(blk_id == full, 1.0, 0.0).astype(BF16)
        hi = jnp.floor(cum_e * (1.0 / 32.0))
        lo = cum_e - 32.0 * hi
        straddle = (32.0 * lax.dot_general(hi.astype(BF16), pick, tn, preferred_element_type=F32)
                    + lax.dot_general(lo.astype(BF16), pick, tn, preferred_element_type=F32))
        inside = jnp.sum(jnp.where(straddle <= slot, 1.0, 0.0), axis=0, keepdims=True)
        idx_ref[0, pl.ds(e, 1), :] = (float(LANES) * full + inside)[:, :cap].astype(I32)
        return carry

    lax.fori_loop(0, ne, per_expert, 0, unroll=4)


def _topk(aff_t, cap):
    b, e, n = aff_t.shape
    return pl.pallas_call(
        functools.partial(_topk_body, cap=cap),
        grid=(b,),
        in_specs=[pl.BlockSpec((1, e, n), lambda bb: (bb, 0, 0))],
        out_specs=pl.BlockSpec((1, e, cap), lambda bb: (bb, 0, 0)),
        out_shape=jax.ShapeDtypeStruct((b, e, cap), I32),
        scratch_shapes=[pltpu.VMEM((max(n // LANES, SUBLANES) * e, LANES), F32)],
        compiler_params=_cparams(),
        name="expert_choice_topk",
    )(aff_t)


def _moe_body(idxp_ref, idxc_ref, idxn_ref, hp_ref, aff_ref, wg_ref, wu_ref, wd_ref, out_ref,
              xs_scr, ag_scr, y_scr, xb_scr, hid_scr, *, cap):
    e = pl.program_id(1)
    f = pl.program_id(2)
    ne = pl.num_programs(1)
    nf = pl.num_programs(2)
    cur = e % 2
    oth = 1 - cur

    @pl.when((pl.program_id(0) == 0) & (e == 0) & (f == 0))
    def _():
        y_scr[...] = jnp.zeros_like(y_scr)

    @pl.when((e == 0) & (f == 0))
    def _():
        out_ref[...] = jnp.zeros_like(out_ref)

        def gather(j, carry):
            for u in range(SUBLANES):
                t = idxc_ref[0, 0, 0, j * SUBLANES + u]
                xs_scr[0, j, u:u + 1, :] = hp_ref[0, pl.ds(t, 1), :]
                ag_scr[0, j, u:u + 1, :] = aff_ref[0, pl.ds(t, 1), :]
            return carry

        lax.fori_loop(0, cap // SUBLANES, gather, 0)

    @pl.when(f == 0)
    def _():
        w = xs_scr[cur].reshape(cap, xs_scr.shape[-1])
        half = w.shape[1]
        xb_scr[:, :half] = pltpu.bitcast(w << 16, F32).astype(BF16)
        xb_scr[:, half:] = pltpu.bitcast(w & jnp.uint32(0xFFFF0000), F32).astype(BF16)

    lane = lax.broadcasted_iota(I32, (cap, LANES), 1)
    gate = jnp.sum(jnp.where(lane == e, ag_scr[cur].reshape(cap, LANES), 0.0), axis=1, keepdims=True)
    share = cap // nf
    base = f * share
    base_tile = f * (share // SUBLANES)
    has_prev = e > 0

    def row_copies(r0, r1):
        for g0 in range(r0, r1, MOE_ROW_GROUP):
            rows = range(g0, min(g0 + MOE_ROW_GROUP, r1))
            tps = [idxp_ref[0, 0, 0, base + r] for r in rows]
            sums = [out_ref[0, pl.ds(tp, 1), :]
                    + jnp.where(has_prev, y_scr[oth, base_tile + r // SUBLANES, r % SUBLANES:r % SUBLANES + 1, :], 0.0)
                    for tp, r in zip(tps, rows)]
            for tp, v in zip(tps, sums):
                out_ref[0, pl.ds(tp, 1), :] = v
            for r in rows:
                tn = idxn_ref[0, 0, 0, base + r]
                j, u = base_tile + r // SUBLANES, r % SUBLANES
                xs_scr[oth, j, u:u + 1, :] = hp_ref[0, pl.ds(tn, 1), :]
                ag_scr[oth, j, u:u + 1, :] = aff_ref[0, pl.ds(tn, 1), :]

    tf = wg_ref.shape[3]
    nsl = tf // MOE_FF_SLICE
    sl = lambda c: slice(c * MOE_FF_SLICE, (c + 1) * MOE_FF_SLICE)

    def hidden(c):
        xb = xb_scr[...]
        hg = jnp.dot(xb, wg_ref[0, 0, :, sl(c)], preferred_element_type=F32)
        hu = jnp.dot(xb, wu_ref[0, 0, :, sl(c)], preferred_element_type=F32)
        hid_scr[:, sl(c)] = (_silu(hg) * hu).astype(BF16)

    for c in range(nsl):
        hidden(c)
        row_copies(share * c // nsl, share * (c + 1) // nsl)
    yp = jnp.dot(hid_scr[...], wd_ref[0, 0], preferred_element_type=F32)
    y_old = y_scr[cur].reshape(cap, y_scr.shape[-1])
    y_new = (jnp.where(f == 0, 0.0, y_old) + yp) * jnp.where(f == nf - 1, gate, 1.0)
    y_scr[cur] = y_new.reshape(y_scr.shape[1:])

    @pl.when((e == ne - 1) & (f == nf - 1))
    def _():
        def scatter(j, carry):
            for u in range(SUBLANES):
                t = idxc_ref[0, 0, 0, j * SUBLANES + u]
                out_ref[0, pl.ds(t, 1), :] = out_ref[0, pl.ds(t, 1), :] + y_scr[cur, j, u:u + 1, :]
            return carry

        lax.fori_loop(0, cap // SUBLANES, scatter, 0)


def _moe(idx, hp, aff, wg, wu, wd, layer, tf):
    b, t, half = hp.shape
    d = 2 * half
    ne, ff = wg.shape[1], wg.shape[3]
    cap = idx.shape[-1]
    nf = ff // tf
    assert nf >= 2 and cap % nf == 0
    one = pl.Buffered(1)
    idx4 = idx.reshape(b, ne, 1, cap)
    smem = lambda shift: pl.BlockSpec((1, 1, 1, cap), lambda bb, e, f: (bb, jnp.clip(e + shift, 0, ne - 1), 0, 0),
                                      memory_space=pltpu.SMEM)
    return pl.pallas_call(
        functools.partial(_moe_body, cap=cap),
        grid=(b, ne, nf),
        in_specs=[
            smem(-1), smem(0), smem(1),
            pl.BlockSpec((1, t, half), lambda bb, e, f: (bb, 0, 0), pipeline_mode=one),
            pl.BlockSpec((1, t, LANES), lambda bb, e, f: (bb, 0, 0), pipeline_mode=one),
            pl.BlockSpec((1, 1, d, tf), lambda bb, e, f: (layer, e, 0, f)),
            pl.BlockSpec((1, 1, d, tf), lambda bb, e, f: (layer, e, 0, f)),
            pl.BlockSpec((1, 1, tf, d), lambda bb, e, f: (layer, e, f, 0)),
        ],
        out_specs=pl.BlockSpec((1, t, d), lambda bb, e, f: (bb, 0, 0), pipeline_mode=one),
        out_shape=jax.ShapeDtypeStruct((b, t, d), F32),
        scratch_shapes=[
            pltpu.VMEM((2, cap // SUBLANES, SUBLANES, half), jnp.uint32),
            pltpu.VMEM((2, cap // SUBLANES, SUBLANES, LANES), F32),
            pltpu.VMEM((2, cap // SUBLANES, SUBLANES, d), F32),
            pltpu.VMEM((cap, d), BF16),
            pltpu.VMEM((cap, tf), BF16),
        ],
        compiler_params=_cparams(VMEM_LIMIT_MOE),
        name="expert_ffn",
    )(idx4, idx4, idx4, hp, aff, wg, wu, wd)


def _ln2_body(x1_ref, moe_ref, g2_ref, w_ref, b_ref, o_ref):
    o_ref[0] = _layer_norm(DEEPNORM_ALPHA * x1_ref[0] + g2_ref[0] * moe_ref[0], w_ref[...], b_ref[...])


def _ln2(x1, moe, modl, w, bias, ctx_row, off):
    b, t, d = x1.shape
    tl = SEQ_TILE
    seq = pl.BlockSpec((1, tl, d), lambda bb, i: (bb, i, 0))
    full = pl.BlockSpec((1, d), lambda bb, i: (0, 0))
    return pl.pallas_call(
        _ln2_body,
        grid=(b, t // tl),
        in_specs=[seq, seq, _mod_spec(5, d, ctx_row, off), full, full],
        out_specs=seq,
        out_shape=jax.ShapeDtypeStruct((b, t, d), F32),
        compiler_params=_cparams(),
        name="ln2",
    )(x1, moe, modl, w, bias)


def _block_diag_ones(width, block):
    r = jnp.arange(width)[:, None] // block
    c = jnp.arange(width)[None, :] // block
    return (r == c).astype(BF16)


def _block_diag(w):
    k, d, e = w.shape
    eye = jnp.eye(k, dtype=w.dtype)
    return (eye[:, None, :, None] * w[:, :, None, :]).reshape(k * d, k * e)


def _rope_tables(n_ctx, n_lat):
    pos = jnp.arange(n_lat)
    row = (pos // GRID_W).astype(F32)
    col = (pos % GRID_W).astype(F32)
    inv_freq = ROPE_THETA ** (-jnp.arange(0, AXIS_DIM, 2, dtype=F32) / AXIS_DIM)
    ang_r = row[:, None] * inv_freq
    ang_c = col[:, None] * inv_freq
    cos_h = jnp.concatenate([jnp.cos(ang_r)] * 2 + [jnp.cos(ang_c)] * 2, axis=1)
    sin_h = jnp.concatenate([-jnp.sin(ang_r), jnp.sin(ang_r), -jnp.sin(ang_c), jnp.sin(ang_c)], axis=1)
    reps = LANES // HEAD_DIM
    cos_t = jnp.concatenate([jnp.ones((n_ctx, LANES), F32), jnp.tile(cos_h, (1, reps))], axis=0)
    sin_t = jnp.concatenate([jnp.zeros((n_ctx, LANES), F32), jnp.tile(sin_h, (1, reps))], axis=0)
    return cos_t, sin_t


def _layer(xt, modl, rope, p, experts, layer, last, n_ctx, ctx_row):
    b, t, d = xt.shape
    n_lat = t - n_ctx
    cos_t, sin_t = rope
    w = p["w_in"]
    pad = jnp.zeros((d, LANES - SSD_HEADS), F32)
    wcat = jnp.concatenate(
        [w[:, 0:768], w[:, 768:1280], w[:, 1544:1800], w[:, 1280:1536], w[:, 1800:2056],
         w[:, 1536:1540], pad, w[:, 1540:1544], pad], axis=1).astype(BF16)
    qw = jnp.tile(p["q_norm"], W_Q // HEAD_DIM)[None]
    kw = jnp.tile(p["k_norm"], W_K // HEAD_DIM)[None]
    conv_w = jnp.concatenate([p["ssd_conv_w"], p["rg_conv_w"]], axis=1)
    conv_b = jnp.concatenate([p["ssd_conv_b"], p["rg_conv_b"]])[None]
    qt, k, v, xbc, u, z, g, dt = _inproj(xt, modl, wcat, cos_t, sin_t, qw, kw,
                                         _block_diag_ones(W_Q, HEAD_DIM), _block_diag_ones(W_K, HEAD_DIM),
                                         conv_w, conv_b, p["ssd_conv_w"].shape[1], ctx_row)

    qt = qt.reshape(b, W_Q // HEAD_DIM, HEAD_DIM, t)
    kh = k.reshape(b, t, KV_HEADS, HEAD_DIM).transpose(0, 2, 1, 3)
    vt = v.reshape(b, t, KV_HEADS, HEAD_DIM).transpose(0, 2, 3, 1)
    ones_pad = jnp.zeros((b, KV_HEADS, BF16_ROWS, t), BF16).at[:, :, 0, :].set(1.0)
    vt = jnp.concatenate([vt, ones_pad], axis=2)
    attn_lat = _attention(qt[..., n_ctx:], kh, vt, n_lat, 0, t, ATTN_Q_TILE)
    attn_ctx = None if last else _attention(qt, kh, vt, n_ctx, 0, n_ctx, min(ATTN_Q_TILE, n_ctx))

    lane_pad = lambda a: jnp.pad(a, ((0, 0), (0, LANES - a.shape[1])))[:, None, :]
    ys = _ssd(xbc, dt, lane_pad(p["ssd_dt_bias"]), lane_pad(-jnp.exp(p["ssd_a_log"])),
              jnp.repeat(p["ssd_d"], SSD_HEAD_DIM)[None])
    wgate = jnp.stack([jnp.concatenate([_block_diag(p["rg_wa"][j]), _block_diag(p["rg_wx"][j])], axis=1)
                       for j in range(2)]).astype(BF16)
    bgate = jnp.concatenate([p["rg_ba"], p["rg_bx"]], axis=1)[:, None, :]
    hr = _rglru(u, wgate, bgate, p["rg_lambda"][:, None, :])

    off = 1 if last else 0
    wr = jnp.pad(p["w_router"], ((0, 0), (0, LANES - N_EXPERTS))).astype(BF16)
    x1, hp, aff, aff_t = _merge(attn_ctx, attn_lat, ys, z, hr, g, xt, modl,
                                p["attn_out_norm"][None], p["ssd_norm"][None], p["rg_out_norm"][None],
                                p["w_out"].astype(BF16), p["ln1_w"][None], p["ln1_b"][None], wr, ctx_row, off)

    if last:
        idx = _topk(aff_t, EC_CAPACITY * n_lat // N_EXPERTS)
    else:
        idx_lat = _topk(aff_t[:, :, n_ctx:], EC_CAPACITY * n_lat // N_EXPERTS) + n_ctx
        idx_ctx = _topk(aff_t[:, :, :n_ctx], EC_CAPACITY * n_ctx // N_EXPERTS)
        idx = jnp.concatenate([idx_lat, idx_ctx], axis=-1)
    moe = _moe(idx, hp, aff, *experts, layer, MOE_FF_TILE)
    return _ln2(x1, moe, modl, p["ln2_w"][None], p["ln2_b"][None], ctx_row, off)


def kernel(x, c, ctx, c_ctx, w_mod, b_mod, w_in, q_norm, k_norm, attn_out_norm, ssd_conv_w, ssd_conv_b, ssd_dt_bias, ssd_a_log, ssd_d, ssd_norm, rg_conv_w, rg_conv_b, rg_wa, rg_ba, rg_wx, rg_bx, rg_lambda, rg_out_norm, w_out, ln1_w, ln1_b, w_router, w_gate, w_up, w_down, ln2_w, ln2_b):
    b, n_lat, d = x.shape
    n_ctx = ctx.shape[1]
    assert n_ctx == SEQ_TILE and n_lat % 512 == 0 and b == SUBLANES
    params = dict(w_in=w_in, q_norm=q_norm, k_norm=k_norm, attn_out_norm=attn_out_norm, ssd_conv_w=ssd_conv_w,
                  ssd_conv_b=ssd_conv_b, ssd_dt_bias=ssd_dt_bias, ssd_a_log=ssd_a_log, ssd_d=ssd_d,
                  ssd_norm=ssd_norm, rg_conv_w=rg_conv_w, rg_conv_b=rg_conv_b, rg_wa=rg_wa, rg_ba=rg_ba,
                  rg_wx=rg_wx, rg_bx=rg_bx, rg_lambda=rg_lambda, rg_out_norm=rg_out_norm, w_out=w_out,
                  ln1_w=ln1_w, ln1_b=ln1_b, w_router=w_router, ln2_w=ln2_w, ln2_b=ln2_b)
    experts = (w_gate.astype(BF16), w_up.astype(BF16), w_down.astype(BF16))
    mod = _modulation(c, c_ctx, w_mod, b_mod)
    rope = _rope_tables(n_ctx, n_lat)
    xt = jnp.concatenate([ctx, x], axis=1)
    depth = w_mod.shape[0]
    for l in range(depth):
        p = {name: val[l] for name, val in params.items()}
        modl = mod[l].reshape(mod.shape[1], 1, mod.shape[2])
        xt = _layer(xt, modl, rope, p, experts, l, l == depth - 1, n_ctx, b)
    return xt
```

```python
import functools
import math

import jax
import jax.numpy as jnp
from jax import lax
from jax.experimental import pallas as pl
from jax.experimental.pallas import tpu as pltpu

F32 = jnp.float32
BF16 = jnp.bfloat16
I32 = jnp.int32

DEPTH = 2
GRID_W = 64
HEAD_DIM = 64
KV_HEADS = 2
Q_PER_KV = 4
AXIS_DIM = HEAD_DIM // 2
ROPE_THETA = 10000.0
SSD_HEADS = 4
SSD_HEAD_DIM = 64
SSD_STATE = 64
RG_BLOCKS = 4
RG_C = 8.0
CONV_W = 4
N_EXPERTS = 16
EC_CAPACITY = 2
EPS = 1e-6
DEEPNORM_ALPHA = (2 * DEPTH) ** 0.25

LANES = 128
SUBLANES = 8
BF16_ROWS = 16
SEQ_TILE = 256
ATTN_Q_TILE = 512
ATTN_KEY_CHUNK = 256
MERGE_ROW_SPLIT = 2
SSD_BATCH = 4
MOE_FF_TILE = 1024
MOE_FF_SLICE = 256
MOE_ROW_GROUP = 4
LOG2E = math.log2(math.e)
VMEM_LIMIT = 48 * 1024 * 1024
VMEM_LIMIT_MOE = 60 * 1024 * 1024


def _cparams(limit=VMEM_LIMIT):
    return pltpu.CompilerParams(vmem_limit_bytes=limit)


def _sigmoid(x):
    return 1.0 / (1.0 + jnp.exp(-x))


def _silu(x):
    return x * _sigmoid(x)


def _softplus(x):
    return jnp.maximum(x, 0.0) + jnp.log1p(jnp.exp(-jnp.abs(x)))


def _bdot(a, b):
    return jnp.dot(a.astype(BF16), b.astype(BF16), preferred_element_type=F32)


def _mod_body(c_ref, w_ref, b_ref, o_ref):
    c = c_ref[...]
    o_ref[0] = _bdot(_silu(c), w_ref[0]) + b_ref[0]


def _modulation(c, c_ctx, w_mod, b_mod):
    depth, d, n6 = w_mod.shape
    b = c.shape[0]
    rows = 2 * SUBLANES
    cc = jnp.zeros((rows, d), F32).at[:b].set(c).at[b].set(c_ctx)
    tn = 1536
    return pl.pallas_call(
        _mod_body,
        grid=(depth, n6 // tn),
        in_specs=[
            pl.BlockSpec((rows, d), lambda l, j: (0, 0)),
            pl.BlockSpec((1, d, tn), lambda l, j: (l, 0, j)),
            pl.BlockSpec((1, 1, tn), lambda l, j: (l, 0, j)),
        ],
        out_specs=pl.BlockSpec((1, rows, tn), lambda l, j: (l, 0, j)),
        out_shape=jax.ShapeDtypeStruct((depth, rows, n6), F32),
        compiler_params=_cparams(),
        name="adaln_mod",
    )(cc, w_mod, b_mod.reshape(depth, 1, n6))


def _mod_spec(j, d, ctx_row, off):
    if off == 0:
        return pl.BlockSpec((1, 1, d), lambda b, i: (jnp.where(i == 0, ctx_row, b), 0, j))
    return pl.BlockSpec((1, 1, d), lambda b, i: (b, 0, j))


W_Q, W_K, W_V, W_XU, W_Z, W_G, W_DT = 512, 128, 128, 768, 256, 256, 256
IN_PAD = W_Q + W_K + W_V + W_XU + W_Z + W_G + W_DT


def _rope(t, cos, sin_signed):
    rows = t.shape[0]
    lane = lax.broadcasted_iota(I32, (rows, LANES), 1)
    first = (lane % AXIS_DIM) < (AXIS_DIM // 2)
    outs = []
    for c in range(t.shape[1] // LANES):
        tc = t[:, c * LANES:(c + 1) * LANES]
        partner = jnp.where(first, pltpu.roll(tc, LANES - AXIS_DIM // 2, 1), pltpu.roll(tc, AXIS_DIM // 2, 1))
        outs.append(tc * cos + partner * sin_signed)
    return outs[0] if len(outs) == 1 else jnp.concatenate(outs, axis=1)


def _inproj_body(x_ref, xp_ref, xn_ref, sc_ref, sh_ref, w_ref, cos_ref, sin_ref, qw_ref, kw_ref, oq_ref, ok_ref,
                 cw_ref, cb_ref, q_ref, k_ref, v_ref, xbc_ref, u_ref, z_ref, g_ref, dt_ref, qkv_scr):
    i = pl.program_id(1)
    n = pl.num_programs(1)
    scale = 1.0 + sc_ref[0]
    shift = sh_ref[0]
    h = (x_ref[0] * scale + shift).astype(BF16)
    n_qkv = W_Q + W_K + W_V
    qkv_scr[...] = jnp.dot(h, w_ref[:, :n_qkv], preferred_element_type=F32)
    rest = jnp.dot(h, w_ref[:, n_qkv:], preferred_element_type=F32)
    halo = (jnp.concatenate([xp_ref[0], xn_ref[0]], axis=0) * scale + shift).astype(BF16)
    xu_halo = jnp.dot(halo, w_ref[:, n_qkv:n_qkv + W_XU], preferred_element_type=F32)
    xu = rest[:, :W_XU]
    z_ref[0] = rest[:, W_XU:W_XU + W_Z]
    g_ref[0] = rest[:, W_XU + W_Z:W_XU + W_Z + W_G]
    dt_ref[0] = rest[:, W_XU + W_Z + W_G:]

    tl = xu.shape[0]
    has_prev = i > 1
    has_next = (i > 0) & (i < n - 1)
    pm = jnp.where(has_prev, xu_halo[SUBLANES - 1:SUBLANES, :], 0.0)
    n0 = jnp.where(has_next, xu_halo[SUBLANES:SUBLANES + 1, :], 0.0)
    n1 = jnp.where(has_next, xu_halo[SUBLANES + 1:SUBLANES + 2, :], 0.0)
    row = lax.broadcasted_iota(I32, xu.shape, 0)
    xm1 = jnp.where(row == 0, pm, pltpu.roll(xu, 1, 0))
    xp1 = jnp.where(row == tl - 1, n0, pltpu.roll(xu, tl - 1, 0))
    xp2 = jnp.where(row == tl - 1, n1, jnp.where(row == tl - 2, n0, pltpu.roll(xu, tl - 2, 0)))
    cw = cw_ref[...]
    yc = xm1 * cw[0:1] + xu * cw[1:2] + xp1 * cw[2:3] + xp2 * cw[3:4] + cb_ref[...]
    n_xbc = xbc_ref.shape[-1]
    xbc_ref[0] = _silu(yc[:, :n_xbc])
    u_ref[0] = yc[:, n_xbc:]

    q = qkv_scr[:, :W_Q]
    k = qkv_scr[:, W_Q:W_Q + W_K]
    cos = cos_ref[...]
    sin = sin_ref[...]
    ssq = jnp.dot((q * q).astype(BF16), oq_ref[...], preferred_element_type=F32)
    qn = q * lax.rsqrt(ssq * (1.0 / HEAD_DIM) + EPS) * qw_ref[...]
    q_ref[0] = (_rope(qn, cos, sin) * (HEAD_DIM ** -0.5 * LOG2E)).T.astype(BF16)
    ssk = jnp.dot((k * k).astype(BF16), ok_ref[...], preferred_element_type=F32)
    kn = k * lax.rsqrt(ssk * (1.0 / HEAD_DIM) + EPS) * kw_ref[...]
    k_ref[0] = _rope(kn, cos, sin).astype(BF16)
    v_ref[0] = qkv_scr[:, W_Q + W_K:].astype(BF16)


def _inproj(xt, modl, wcat, cos_t, sin_t, qw, kw, ones_q, ones_k, conv_w, conv_b, n_xbc, ctx_row):
    b, t, d = xt.shape
    tl = SEQ_TILE
    grid = (b, t // tl)
    r = tl // SUBLANES
    nblk = t // SUBLANES
    full = lambda shape: pl.BlockSpec(shape, lambda bb, i: (0,) * len(shape))
    seq = lambda w: pl.BlockSpec((1, tl, w), lambda bb, i: (bb, i, 0))
    outs = [(W_K, BF16), (W_V, BF16), (n_xbc, F32), (W_XU - n_xbc, F32), (W_Z, F32), (W_G, F32), (W_DT, F32)]
    return pl.pallas_call(
        _inproj_body,
        grid=grid,
        in_specs=[
            seq(d),
            pl.BlockSpec((1, SUBLANES, d), lambda bb, i: (bb, jnp.maximum(i * r - 1, 0), 0)),
            pl.BlockSpec((1, SUBLANES, d), lambda bb, i: (bb, jnp.minimum((i + 1) * r, nblk - 1), 0)),
            _mod_spec(1, d, ctx_row, 0),
            _mod_spec(0, d, ctx_row, 0),
            full((d, IN_PAD)),
            pl.BlockSpec((tl, LANES), lambda bb, i: (i, 0)),
            pl.BlockSpec((tl, LANES), lambda bb, i: (i, 0)),
            full((1, W_Q)), full((1, W_K)), full((W_Q, W_Q)), full((W_K, W_K)),
            full((CONV_W, W_XU)), full((1, W_XU)),
        ],
        out_specs=[pl.BlockSpec((1, W_Q, tl), lambda bb, i: (bb, 0, i))] + [seq(w) for w, _ in outs],
        out_shape=[jax.ShapeDtypeStruct((b, W_Q, t), BF16)]
        + [jax.ShapeDtypeStruct((b, t, w), dt) for w, dt in outs],
        scratch_shapes=[pltpu.VMEM((tl, W_Q + W_K + W_V), F32)],
        compiler_params=_cparams(),
        name="in_proj",
    )(xt, xt, xt, modl, modl, wcat, cos_t, sin_t, qw, kw, ones_q, ones_k, conv_w, conv_b)


def _attn_body(qt_ref, k_ref, vt_ref, o_ref, s_buf, p_buf, acc_ref, *, n_keys, kc):
    nh, hd, tq = qt_ref.shape[1:]
    qt = jnp.concatenate([qt_ref[0, h] for h in range(nh)], axis=1)
    cols = nh * tq
    rem = n_keys % kc
    chunks = ([(0, rem)] if rem else []) + [(rem + kc * i, kc) for i in range(n_keys // kc)]
    n = len(chunks)

    def scores(c):
        s0, sz = chunks[c]
        s_buf[c % 2, :sz, :] = jnp.dot(k_ref[0, 0, s0:s0 + sz, :], qt, preferred_element_type=F32)

    def softmax(c, m):
        sz = chunks[c][1]
        s = s_buf[c % 2, :sz, :]
        m_new = jnp.maximum(m, jnp.max(s, axis=0, keepdims=True))
        p_buf[c % 2, :sz, :] = jnp.exp2(s - m_new).astype(BF16)
        return m_new, jnp.exp2(m - m_new)

    def weighted_values(c, alpha):
        s0, sz = chunks[c]
        acc_ref[...] = alpha * acc_ref[...] + jnp.dot(vt_ref[0, 0, :, s0:s0 + sz], p_buf[c % 2, :sz, :],
                                                      preferred_element_type=F32)

    acc_ref[...] = jnp.zeros_like(acc_ref)
    m = jnp.full((1, cols), -jnp.inf, F32)
    alpha = None
    scores(0)
    for j in range(n + 1):
        if j + 1 < n:
            scores(j + 1)
        prev_alpha = alpha
        if j < n:
            m, alpha = softmax(j, m)
        if j >= 1:
            weighted_values(j - 1, prev_alpha)
    o = acc_ref[:hd, :] * (1.0 / acc_ref[hd:hd + 1, :])
    for h in range(nh):
        o_ref[0, h * hd:(h + 1) * hd, :] = o[:, h * tq:(h + 1) * tq]


def _attention(qt, k, vt, n_q, q_off, n_keys, tq):
    b = qt.shape[0]
    assert q_off % tq == 0
    off = q_off // tq
    kc = min(ATTN_KEY_CHUNK, n_keys)
    cols = Q_PER_KV * tq
    vrows = vt.shape[2]
    return pl.pallas_call(
        functools.partial(_attn_body, n_keys=n_keys, kc=kc),
        grid=(b, KV_HEADS, n_q // tq),
        scratch_shapes=[pltpu.VMEM((2, kc, cols), F32), pltpu.VMEM((2, kc, cols), BF16),
                        pltpu.VMEM((vrows, cols), F32)],
        in_specs=[
            pl.BlockSpec((1, Q_PER_KV, HEAD_DIM, tq), lambda bb, g, i: (bb, g, 0, i + off)),
            pl.BlockSpec((1, 1, n_keys, HEAD_DIM), lambda bb, g, i: (bb, g, 0, 0)),
            pl.BlockSpec((1, 1, vrows, n_keys), lambda bb, g, i: (bb, g, 0, 0)),
        ],
        out_specs=pl.BlockSpec((1, Q_PER_KV * HEAD_DIM, tq), lambda bb, g, i: (bb, g, i)),
        out_shape=jax.ShapeDtypeStruct((b, KV_HEADS * Q_PER_KV * HEAD_DIM, n_q), F32),
        compiler_params=_cparams(),
        name="gqa_attention",
    )(qt, k, vt)


def _seq_order(d, c, nc):
    return jnp.where(c == 0, 0, jnp.where(d == 0, c, nc - c))


def _ssd_body(xf_ref, xb_ref, dtf_ref, dtb_ref, bias_ref, a_ref, dvec_ref, yf_ref, yb_ref, s_scr):
    @pl.when(pl.program_id(1) == 0)
    def _():
        s_scr[...] = jnp.zeros_like(s_scr)

    nb = xf_ref.shape[0]
    streams = range(2 * nb)
    xbc = [(xf_ref, xb_ref)[k % 2][k // 2] for k in streams]
    q = xbc[0].shape[0]
    xw = SSD_HEADS * SSD_HEAD_DIM
    gw = 2 * SSD_STATE
    x = [v[:, :xw] for v in xbc]
    bm = [v[:, xw:xw + gw] for v in xbc]
    cm = [v[:, xw + gw:xw + 2 * gw] for v in xbc]
    dt_raw = [(dtf_ref, dtb_ref)[k % 2][k // 2] for k in streams]
    dt = [_softplus(dt_raw[k] + bias_ref[k % 2]) for k in streams]
    da = [dt[k] * a_ref[k % 2] for k in streams]
    ii = lax.broadcasted_iota(I32, (q, q), 0)
    jj = lax.broadcasted_iota(I32, (q, q), 1)
    mask = [jj <= ii, jj >= ii]
    tm = [jnp.where(m_, 1.0, 0.0).astype(BF16) for m_ in mask]
    da_hi = [da[k].astype(BF16) for k in streams]
    da_lo = [(da[k] - da_hi[k].astype(F32)).astype(BF16) for k in streams]
    cs = [jnp.dot(tm[k % 2], da_hi[k], preferred_element_type=F32)
          + jnp.dot(tm[k % 2], da_lo[k], preferred_element_type=F32) for k in streams]
    tot = [cs[k][q - 1:q, :] if k % 2 == 0 else cs[k][0:1, :] for k in streams]
    cst = [cs[k].T for k in streams]
    dec = [jnp.exp(tot[k] - cs[k]) for k in streams]
    ecs = [jnp.exp(cs[k]) for k in streams]
    etot = [jnp.exp(tot[k]) for k in streams]
    ys = [[] for _ in streams]
    nt = (((1,), (1,)), ((), ()))
    tn = (((0,), (0,)), ((), ()))
    for g in range(2):
        gs = slice(g * SSD_STATE, (g + 1) * SSD_STATE)
        bg = [bm[k][:, gs].astype(BF16) for k in streams]
        cg = [cm[k][:, gs].astype(BF16) for k in streams]
        gmat = [lax.dot_general(cg[k], bg[k], nt, preferred_element_type=F32) for k in streams]
        for hh in range(SSD_HEADS // 2):
            h = 2 * g + hh
            hs = slice(h * SSD_HEAD_DIM, (h + 1) * SSD_HEAD_DIM)
            lmat = [jnp.exp(jnp.where(mask[k % 2], cs[k][:, h:h + 1] - cst[k][h:h + 1, :], -jnp.inf))
                    for k in streams]
            xdt = [x[k][:, hs] * dt[k][:, h:h + 1] for k in streams]
            y_diag = [_bdot(gmat[k] * lmat[k], xdt[k]) for k in streams]
            s_in = [s_scr[k, h] for k in streams]
            y_off = [lax.dot_general(cg[k], s_in[k].astype(BF16), nt, preferred_element_type=F32) * ecs[k][:, h:h + 1]
                     for k in streams]
            xd = [(xdt[k] * dec[k][:, h:h + 1]).astype(BF16) for k in streams]
            for k in streams:
                ys[k].append(y_diag[k] + y_off[k])
                s_scr[k, h] = etot[k][:, h:h + 1] * s_in[k] + lax.dot_general(xd[k], bg[k], tn,
                                                                               preferred_element_type=F32)
    for k in streams:
        if k % 2 == 0:
            yf_ref[k // 2] = jnp.concatenate(ys[k], axis=1) + x[k] * dvec_ref[...]
        else:
            yb_ref[k // 2] = jnp.concatenate(ys[k], axis=1)


def _ssd(xbc, dt, bias, a_neg, dvec):
    b, t, cw = xbc.shape
    q = SEQ_TILE
    nc = t // q
    xw = SSD_HEADS * SSD_HEAD_DIM
    fwd = lambda c: c
    bwd = lambda c: _seq_order(1, c, nc)
    nb = SSD_BATCH
    return pl.pallas_call(
        _ssd_body,
        grid=(b // nb, nc),
        in_specs=[
            pl.BlockSpec((nb, q, cw), lambda bb, c: (bb, fwd(c), 0)),
            pl.BlockSpec((nb, q, cw), lambda bb, c: (bb, bwd(c), 0)),
            pl.BlockSpec((nb, q, LANES), lambda bb, c: (bb, fwd(c), 0)),
            pl.BlockSpec((nb, q, LANES), lambda bb, c: (bb, bwd(c), 1)),
            pl.BlockSpec((2, 1, LANES), lambda bb, c: (0, 0, 0)),
            pl.BlockSpec((2, 1, LANES), lambda bb, c: (0, 0, 0)),
            pl.BlockSpec((1, xw), lambda bb, c: (0, 0)),
        ],
        out_specs=[pl.BlockSpec((nb, q, xw), lambda bb, c: (bb, fwd(c), 0)),
                   pl.BlockSpec((nb, q, xw), lambda bb, c: (bb, bwd(c), 0))],
        out_shape=[jax.ShapeDtypeStruct((b, t, xw), F32)] * 2,
        scratch_shapes=[pltpu.VMEM((2 * nb, SSD_HEADS, SSD_HEAD_DIM, SSD_STATE), F32)],
        compiler_params=_cparams(),
        name="ssd_scan",
    )(xbc, xbc, dt, dt, bias, a_neg, dvec)


def _rg_body(u_ref, w_ref, bias_ref, lam_ref, y_ref, a_s, v_s, o_s, h_s, *, pitch):
    d = pl.program_id(0)
    c = pl.program_id(1)
    nb, tl, width = u_ref.shape
    ng = width // LANES

    @pl.when(c == 0)
    def _():
        h_s[...] = jnp.zeros_like(h_s)

    sp = _softplus(-lam_ref[0])
    w = w_ref[0]
    bias = bias_ref[0]
    for b in range(nb):
        ub = u_ref[b]
        pre = jnp.dot(ub.astype(BF16), w, preferred_element_type=F32) + bias
        r = _sigmoid(pre[:, :width])
        ig = _sigmoid(pre[:, width:])
        a = jnp.exp((-RG_C) * r * sp)
        v = jnp.sqrt(1.0 - a * a) * ig * ub
        for j in range(ng):
            a_s[j, pl.ds(b * pitch, tl), :] = a[:, j * LANES:(j + 1) * LANES]
            v_s[j, pl.ds(b * pitch, tl), :] = v[:, j * LANES:(j + 1) * LANES]

    def step(t, hs):
        te = jnp.where(d == 0, t, tl - 1 - t)
        out = []
        for j in range(ng):
            at = a_s[j, pl.ds(te, nb, stride=pitch), :]
            vt = v_s[j, pl.ds(te, nb, stride=pitch), :]
            hj = at * hs[j] + vt
            o_s[j, pl.ds(te, nb, stride=pitch), :] = hj
            out.append(hj)
        return tuple(out)

    h0 = tuple(h_s[:, j * LANES:(j + 1) * LANES] for j in range(ng))
    hf = lax.fori_loop(0, tl, step, h0, unroll=8)
    for j in range(ng):
        h_s[:, j * LANES:(j + 1) * LANES] = hf[j]
    for b in range(nb):
        y_ref[0, b] = jnp.concatenate([o_s[j, pl.ds(b * pitch, tl), :] for j in range(ng)], axis=1)


def _rglru(u, wg, bias, lam):
    b, t, width = u.shape
    assert b == SUBLANES, "the recurrence keeps one sample per sublane"
    tl = SEQ_TILE
    nc = t // tl
    pitch = tl + SUBLANES
    ng = width // LANES
    slab = pltpu.VMEM((ng, b * pitch, LANES), F32)
    return pl.pallas_call(
        functools.partial(_rg_body, pitch=pitch),
        grid=(2, nc),
        in_specs=[
            pl.BlockSpec((b, tl, width), lambda d, c: (0, _seq_order(d, c, nc), 0)),
            pl.BlockSpec((1, width, 2 * width), lambda d, c: (d, 0, 0)),
            pl.BlockSpec((1, 1, 2 * width), lambda d, c: (d, 0, 0)),
            pl.BlockSpec((1, 1, width), lambda d, c: (d, 0, 0)),
        ],
        out_specs=pl.BlockSpec((1, b, tl, width), lambda d, c: (d, 0, _seq_order(d, c, nc), 0)),
        out_shape=jax.ShapeDtypeStruct((2, b, t, width), F32),
        scratch_shapes=[slab, slab, slab, pltpu.VMEM((b, width), F32)],
        compiler_params=_cparams(),
        name="rglru_scan",
    )(u, wg, bias, lam)


def _rms(x, w):
    return x * lax.rsqrt(jnp.mean(x * x, axis=-1, keepdims=True) + EPS) * w


def _layer_norm(t, w, b):
    mu = jnp.mean(t, axis=-1, keepdims=True)
    tc = t - mu
    var = jnp.mean(tc * tc, axis=-1, keepdims=True)
    return tc * lax.rsqrt(var + EPS) * w + b


def _gelu_tanh(x):
    return 0.5 * x * (1.0 + jnp.tanh(math.sqrt(2.0 / math.pi) * (x + 0.044715 * (x * x * x))))


def _pack_bf16_pair(lo, hi):
    lb = pltpu.bitcast(lo.astype(BF16).astype(F32), jnp.uint32)
    hb = pltpu.bitcast(hi.astype(BF16).astype(F32), jnp.uint32)
    return (lb >> 16) | (hb & jnp.uint32(0xFFFF0000))


def _merge_body(*refs, has_ctx):
    if has_ctx:
        actx_ref, refs = refs[0], refs[1:]
    (alat_ref, ysf_ref, ysb_ref, z_ref, hr_ref, g_ref, x_ref, g1_ref, sc2_ref, sh2_ref, aw_ref, sw_ref, rw_ref,
     wo_ref, lnw_ref, lnb_ref, wr_ref, x1_ref, hp_ref, aff_ref, afft_ref, cat_scr, proj_scr) = refs
    tl = x_ref.shape[1]
    nh = MERGE_ROW_SPLIT
    rows = [slice(j * tl // nh, (j + 1) * tl // nh) for j in range(nh)]
    for rs in rows:
        a = alat_ref[0, :, rs]
        if has_ctx:
            a = jnp.where(pl.program_id(1) == 0, actx_ref[0, :, rs], a)
        an = _rms(a.T, aw_ref[...])
        sn = _rms((ysf_ref[0, rs] + ysb_ref[0, rs]) * _silu(z_ref[0, rs]), sw_ref[...])
        rn = _rms((hr_ref[0, 0, rs] + hr_ref[1, 0, rs]) * _gelu_tanh(g_ref[0, rs]), rw_ref[...])
        cat_scr[rs] = jnp.concatenate([an, sn, rn], axis=1).astype(BF16)
    for rs in rows:
        proj_scr[rs] = jnp.dot(cat_scr[rs], wo_ref[...], preferred_element_type=F32)
    for rs in rows:
        x1 = _layer_norm(DEEPNORM_ALPHA * x_ref[0, rs] + g1_ref[0] * proj_scr[rs], lnw_ref[...], lnb_ref[...])
        x1_ref[0, rs] = x1
        h2 = x1 * (1.0 + sc2_ref[0]) + sh2_ref[0]
        half = h2.shape[1] // 2
        hp_ref[0, rs] = _pack_bf16_pair(h2[:, :half], h2[:, half:])
        logits = jnp.dot(h2.astype(BF16), wr_ref[...], preferred_element_type=F32)
        lane = lax.broadcasted_iota(I32, logits.shape, 1)
        logits = jnp.where(lane < N_EXPERTS, logits, -jnp.inf)
        e = jnp.exp(logits - jnp.max(logits, axis=-1, keepdims=True))
        aff = e / jnp.sum(e, axis=-1, keepdims=True)
        aff_ref[0, rs] = aff
        afft_ref[0, :, rs] = aff.T[:N_EXPERTS, :]


def _merge(attn_ctx, attn_lat, ys, z, hr, g, xt, modl, aw, sw, rw, wo, lnw, lnb, wr, ctx_row, off):
    b, t, d = xt.shape
    tl = SEQ_TILE
    nt = t // tl - off
    has_ctx = off == 0
    aw_ = attn_lat.shape[1]
    seq = lambda w: pl.BlockSpec((1, tl, w), lambda bb, i: (bb, i + off, 0))
    pair = lambda w: pl.BlockSpec((2, 1, tl, w), lambda bb, i: (0, bb, i + off, 0))
    full = lambda shape: pl.BlockSpec(shape, lambda bb, i: (0,) * len(shape))
    out = lambda w: pl.BlockSpec((1, tl, w), lambda bb, i: (bb, i, 0))
    lat_off = 1 - off
    in_specs = [
        pl.BlockSpec((1, aw_, tl), lambda bb, i: (bb, 0, jnp.maximum(i - lat_off, 0))),
        seq(ys[0].shape[-1]), seq(ys[1].shape[-1]), seq(z.shape[-1]), pair(hr.shape[-1]), seq(g.shape[-1]), seq(d),
        _mod_spec(2, d, ctx_row, off), _mod_spec(4, d, ctx_row, off), _mod_spec(3, d, ctx_row, off),
        full(aw.shape), full(sw.shape), full(rw.shape), full(wo.shape), full(lnw.shape), full(lnb.shape),
        full(wr.shape),
    ]
    args = [attn_lat, ys[0], ys[1], z, hr, g, xt, modl, modl, modl, aw, sw, rw, wo, lnw, lnb, wr]
    if has_ctx:
        in_specs = [pl.BlockSpec((1, aw_, tl), lambda bb, i: (bb, 0, 0))] + in_specs
        args = [attn_ctx] + args
    rows = nt * tl
    return pl.pallas_call(
        functools.partial(_merge_body, has_ctx=has_ctx),
        grid=(b, nt),
        in_specs=in_specs,
        out_specs=[out(d), out(d // 2), out(LANES), pl.BlockSpec((1, N_EXPERTS, tl), lambda bb, i: (bb, 0, i))],
        out_shape=[
            jax.ShapeDtypeStruct((b, rows, d), F32),
            jax.ShapeDtypeStruct((b, rows, d // 2), jnp.uint32),
            jax.ShapeDtypeStruct((b, rows, LANES), F32),
            jax.ShapeDtypeStruct((b, N_EXPERTS, rows), F32),
        ],
        scratch_shapes=[pltpu.VMEM((tl, wo.shape[0]), BF16), pltpu.VMEM((tl, d), F32)],
        compiler_params=_cparams(),
        name="merge_outproj_ln1_router",
    )(*args)


def _topk_body(aff_ref, idx_ref, cum_scr, *, cap):
    aff = aff_ref[0]
    ne, n = aff.shape
    bits = pltpu.bitcast(aff, I32)

    def search(i, thr):
        cand = thr | lax.shift_left(jnp.int32(1), 30 - i)
        cnt = jnp.sum(jnp.where(bits >= cand, 1.0, 0.0), axis=1, keepdims=True)
        return jnp.where(cnt >= cap, cand, thr)

    thr = lax.fori_loop(0, 31, search, jnp.zeros((ne, 1), I32))
    gt = jnp.where(bits > thr, 1.0, 0.0)
    eq = jnp.where(bits == thr, 1.0, 0.0)
    need = cap - jnp.sum(gt, axis=1, keepdims=True)
    nblk = n // LANES
    r_ = lax.broadcasted_iota(I32, (LANES, LANES), 0)
    c_ = lax.broadcasted_iota(I32, (LANES, LANES), 1)
    upper = jnp.where(r_ <= c_, 1.0, 0.0).astype(BF16)

    def prefix(blocks):
        outs = []
        off = jnp.zeros((ne, 1), F32)
        for mk in blocks:
            w = jnp.dot(mk.astype(BF16), upper, preferred_element_type=F32) + off
            outs.append(w)
            off = w[:, LANES - 1:LANES]
        return outs

    blk = lambda a, k: a[:, k * LANES:(k + 1) * LANES]
    tie_rank = prefix([blk(eq, k) for k in range(nblk)])
    cum = prefix([jnp.maximum(blk(gt, k), blk(eq, k) * jnp.where(tie_rank[k] <= need, 1.0, 0.0))
                  for k in range(nblk)])
    kpad = cum_scr.shape[0] // ne
    for k in range(kpad):
        cum_scr[k * ne:(k + 1) * ne, :] = cum[k] if k < nblk else jnp.zeros((ne, LANES), F32)
    width = -(-cap // LANES) * LANES
    slot = lax.broadcasted_iota(I32, (LANES, width), 1).astype(F32)
    blk_id = lax.broadcasted_iota(I32, (LANES, width), 0).astype(F32)
    blk_col = lax.broadcasted_iota(I32, (LANES, 1), 0)
    tn = (((0,), (0,)), ((), ()))

    def per_expert(e, carry):
        cum_e = cum_scr[pl.ds(e, kpad, stride=ne), :]
        if kpad < LANES:
            cum_e = jnp.concatenate([cum_e, jnp.zeros((LANES - kpad, LANES), F32)], axis=0)
        ends = jnp.where(blk_col < nblk, cum_e[:, LANES - 1:LANES], float(2 * n))
        full = jnp.sum(jnp.where(ends <= slot, 1.0, 0.0), axis=0, keepdims=True)
        pick = jnp.where(blk_id == full, 1.0, 0.0).astype(BF16)
        hi = jnp.floor(cum_e * (1.0 / 32.0))
        lo = cum_e - 32.0 * hi
        straddle = (32.0 * lax.dot_general(hi.astype(BF16), pick, tn, preferred_element_type=F32)
                    + lax.dot_general(lo.astype(BF16), pick, tn, preferred_element_type=F32))
        inside = jnp.sum(jnp.where(straddle <= slot, 1.0, 0.0), axis=0, keepdims=True)
        idx_ref[0, pl.ds(e, 1), :] = (float(LANES) * full + inside)[:, :cap].astype(I32)
        return carry

    lax.fori_loop(0, ne, per_expert, 0, unroll=4)


def _topk(aff_t, cap):
    b, e, n = aff_t.shape
    return pl.pallas_call(
        functools.partial(_topk_body, cap=cap),
        grid=(b,),
        in_specs=[pl.BlockSpec((1, e, n), lambda bb: (bb, 0, 0))],
        out_specs=pl.BlockSpec((1, e, cap), lambda bb: (bb, 0, 0)),
        out_shape=jax.ShapeDtypeStruct((b, e, cap), I32),
        scratch_shapes=[pltpu.VMEM((max(n // LANES, SUBLANES) * e, LANES), F32)],
        compiler_params=_cparams(),
        name="expert_choice_topk",
    )(aff_t)


def _moe_body(idxp_ref, idxc_ref, idxn_ref, hp_ref, aff_ref, wg_ref, wu_ref, wd_ref, out_ref,
              xs_scr, ag_scr, y_scr, xb_scr, hid_scr, *, cap):
    e = pl.program_id(1)
    f = pl.program_id(2)
    ne = pl.num_programs(1)
    nf = pl.num_programs(2)
    cur = e % 2
    oth = 1 - cur

    @pl.when((pl.program_id(0) == 0) & (e == 0) & (f == 0))
    def _():
        y_scr[...] = jnp.zeros_like(y_scr)

    @pl.when((e == 0) & (f == 0))
    def _():
        out_ref[...] = jnp.zeros_like(out_ref)

        def gather(j, carry):
            for u in range(SUBLANES):
                t = idxc_ref[0, 0, 0, j * SUBLANES + u]
                xs_scr[0, j, u:u + 1, :] = hp_ref[0, pl.ds(t, 1), :]
                ag_scr[0, j, u:u + 1, :] = aff_ref[0, pl.ds(t, 1), :]
            return carry

        lax.fori_loop(0, cap // SUBLANES, gather, 0)

    w = xs_scr[cur].reshape(cap, xs_scr.shape[-1])
    half = w.shape[1]
    xb_scr[:, :half] = pltpu.bitcast(w << 16, F32).astype(BF16)
    xb_scr[:, half:] = pltpu.bitcast(w & jnp.uint32(0xFFFF0000), F32).astype(BF16)
    lane = lax.broadcasted_iota(I32, (cap, LANES), 1)
    gate = jnp.sum(jnp.where(lane == e, ag_scr[cur].reshape(cap, LANES), 0.0), axis=1, keepdims=True)
    share = cap // nf
    base = f * share
    base_tile = f * (share // SUBLANES)
    has_prev = e > 0

    def row_copies(r0, r1):
        for g0 in range(r0, r1, MOE_ROW_GROUP):
            rows = range(g0, min(g0 + MOE_ROW_GROUP, r1))
            tps = [idxp_ref[0, 0, 0, base + r] for r in rows]
            sums = [out_ref[0, pl.ds(tp, 1), :]
                    + jnp.where(has_prev, y_scr[oth, base_tile + r // SUBLANES, r % SUBLANES:r % SUBLANES + 1, :], 0.0)
                    for tp, r in zip(tps, rows)]
            for tp, v in zip(tps, sums):
                out_ref[0, pl.ds(tp, 1), :] = v
            for r in rows:
                tn = idxn_ref[0, 0, 0, base + r]
                j, u = base_tile + r // SUBLANES, r % SUBLANES
                xs_scr[oth, j, u:u + 1, :] = hp_ref[0, pl.ds(tn, 1), :]
                ag_scr[oth, j, u:u + 1, :] = aff_ref[0, pl.ds(tn, 1), :]

    tf = wg_ref.shape[3]
    nsl = tf // MOE_FF_SLICE
    sl = lambda c: slice(c * MOE_FF_SLICE, (c + 1) * MOE_FF_SLICE)

    def hidden(c):
        xb = xb_scr[...]
        hg = jnp.dot(xb, wg_ref[0, 0, :, sl(c)], preferred_element_type=F32)
        hu = jnp.dot(xb, wu_ref[0, 0, :, sl(c)], preferred_element_type=F32)
        hid_scr[:, sl(c)] = (_silu(hg) * hu).astype(BF16)

    for c in range(nsl):
        hidden(c)
        row_copies(share * c // nsl, share * (c + 1) // nsl)
    yp = jnp.dot(hid_scr[...], wd_ref[0, 0], preferred_element_type=F32)
    y_old = y_scr[cur].reshape(cap, y_scr.shape[-1])
    y_new = (jnp.where(f == 0, 0.0, y_old) + yp) * jnp.where(f == nf - 1, gate, 1.0)
    y_scr[cur] = y_new.reshape(y_scr.shape[1:])

    @pl.when((e == ne - 1) & (f == nf - 1))
    def _():
        def scatter(j, carry):
            for u in range(SUBLANES):
                t = idxc_ref[0, 0, 0, j * SUBLANES + u]
                out_ref[0, pl.ds(t, 1), :] = out_ref[0, pl.ds(t, 1), :] + y_scr[cur, j, u:u + 1, :]
            return carry

        lax.fori_loop(0, cap // SUBLANES, scatter, 0)


def _moe(idx, hp, aff, wg, wu, wd, layer, tf):
    b, t, half = hp.shape
    d = 2 * half
    ne, ff = wg.shape[1], wg.shape[3]
    cap = idx.shape[-1]
    nf = ff // tf
    assert nf >= 2 and cap % nf == 0
    one = pl.Buffered(1)
    idx4 = idx.reshape(b, ne, 1, cap)
    smem = lambda shift: pl.BlockSpec((1, 1, 1, cap), lambda bb, e, f: (bb, jnp.clip(e + shift, 0, ne - 1), 0, 0),
                                      memory_space=pltpu.SMEM)
    return pl.pallas_call(
        functools.partial(_moe_body, cap=cap),
        grid=(b, ne, nf),
        in_specs=[
            smem(-1), smem(0), smem(1),
            pl.BlockSpec((1, t, half), lambda bb, e, f: (bb, 0, 0), pipeline_mode=one),
            pl.BlockSpec((1, t, LANES), lambda bb, e, f: (bb, 0, 0), pipeline_mode=one),
            pl.BlockSpec((1, 1, d, tf), lambda bb, e, f: (layer, e, 0, f)),
            pl.BlockSpec((1, 1, d, tf), lambda bb, e, f: (layer, e, 0, f)),
            pl.BlockSpec((1, 1, tf, d), lambda bb, e, f: (layer, e, f, 0)),
        ],
        out_specs=pl.BlockSpec((1, t, d), lambda bb, e, f: (bb, 0, 0), pipeline_mode=one),
        out_shape=jax.ShapeDtypeStruct((b, t, d), F32),
        scratch_shapes=[
            pltpu.VMEM((2, cap // SUBLANES, SUBLANES, half), jnp.uint32),
            pltpu.VMEM((2, cap // SUBLANES, SUBLANES, LANES), F32),
            pltpu.VMEM((2, cap // SUBLANES, SUBLANES, d), F32),
            pltpu.VMEM((cap, d), BF16),
            pltpu.VMEM((cap, tf), BF16),
        ],
        compiler_params=_cparams(VMEM_LIMIT_MOE),
        name="expert_ffn",
    )(idx4, idx4, idx4, hp, aff, wg, wu, wd)


def _ln2_body(x1_ref, moe_ref, g2_ref, w_ref, b_ref, o_ref):
    o_ref[0] = _layer_norm(DEEPNORM_ALPHA * x1_ref[0] + g2_ref[0] * moe_ref[0], w_ref[...], b_ref[...])


def _ln2(x1, moe, modl, w, bias, ctx_row, off):
    b, t, d = x1.shape
    tl = SEQ_TILE
    seq = pl.BlockSpec((1, tl, d), lambda bb, i: (bb, i, 0))
    full = pl.BlockSpec((1, d), lambda bb, i: (0, 0))
    return pl.pallas_call(
        _ln2_body,
        grid=(b, t // tl),
        in_specs=[seq, seq, _mod_spec(5, d, ctx_row, off), full, full],
        out_specs=seq,
        out_shape=jax.ShapeDtypeStruct((b, t, d), F32),
        compiler_params=_cparams(),
        name="ln2",
    )(x1, moe, modl, w, bias)


def _block_diag_ones(width, block):
    r = jnp.arange(width)[:, None] // block
    c = jnp.arange(width)[None, :] // block
    return (r == c).astype(BF16)


def _block_diag(w):
    k, d, e = w.shape
    eye = jnp.eye(k, dtype=w.dtype)
    return (eye[:, None, :, None] * w[:, :, None, :]).reshape(k * d, k * e)


def _rope_tables(n_ctx, n_lat):
    pos = jnp.arange(n_lat)
    row = (pos // GRID_W).astype(F32)
    col = (pos % GRID_W).astype(F32)
    inv_freq = ROPE_THETA ** (-jnp.arange(0, AXIS_DIM, 2, dtype=F32) / AXIS_DIM)
    ang_r = row[:, None] * inv_freq
    ang_c = col[:, None] * inv_freq
    cos_h = jnp.concatenate([jnp.cos(ang_r)] * 2 + [jnp.cos(ang_c)] * 2, axis=1)
    sin_h = jnp.concatenate([-jnp.sin(ang_r), jnp.sin(ang_r), -jnp.sin(ang_c), jnp.sin(ang_c)], axis=1)
    reps = LANES // HEAD_DIM
    cos_t = jnp.concatenate([jnp.ones((n_ctx, LANES), F32), jnp.tile(cos_h, (1, reps))], axis=0)
    sin_t = jnp.concatenate([jnp.zeros((n_ctx, LANES), F32), jnp.tile(sin_h, (1, reps))], axis=0)
    return cos_t, sin_t


def _layer(xt, modl, rope, p, experts, layer, last, n_ctx, ctx_row):
    b, t, d = xt.shape
    n_lat = t - n_ctx
    cos_t, sin_t = rope
    w = p["w_in"]
    pad = jnp.zeros((d, LANES - SSD_HEADS), F32)
    wcat = jnp.concatenate(
        [w[:, 0:768], w[:, 768:1280], w[:, 1544:1800], w[:, 1280:1536], w[:, 1800:2056],
         w[:, 1536:1540], pad, w[:, 1540:1544], pad], axis=1).astype(BF16)
    qw = jnp.tile(p["q_norm"], W_Q // HEAD_DIM)[None]
    kw = jnp.tile(p["k_norm"], W_K // HEAD_DIM)[None]
    conv_w = jnp.concatenate([p["ssd_conv_w"], p["rg_conv_w"]], axis=1)
    conv_b = jnp.concatenate([p["ssd_conv_b"], p["rg_conv_b"]])[None]
    qt, k, v, xbc, u, z, g, dt = _inproj(xt, modl, wcat, cos_t, sin_t, qw, kw,
                                         _block_diag_ones(W_Q, HEAD_DIM), _block_diag_ones(W_K, HEAD_DIM),
                                         conv_w, conv_b, p["ssd_conv_w"].shape[1], ctx_row)

    qt = qt.reshape(b, W_Q // HEAD_DIM, HEAD_DIM, t)
    kh = k.reshape(b, t, KV_HEADS, HEAD_DIM).transpose(0, 2, 1, 3)
    vt = v.reshape(b, t, KV_HEADS, HEAD_DIM).transpose(0, 2, 3, 1)
    ones_pad = jnp.zeros((b, KV_HEADS, BF16_ROWS, t), BF16).at[:, :, 0, :].set(1.0)
    vt = jnp.concatenate([vt, ones_pad], axis=2)
    attn_lat = _attention(qt[..., n_ctx:], kh, vt, n_lat, 0, t, ATTN_Q_TILE)
    attn_ctx = None if last else _attention(qt, kh, vt, n_ctx, 0, n_ctx, min(ATTN_Q_TILE, n_ctx))

    lane_pad = lambda a: jnp.pad(a, ((0, 0), (0, LANES - a.shape[1])))[:, None, :]
    ys = _ssd(xbc, dt, lane_pad(p["ssd_dt_bias"]), lane_pad(-jnp.exp(p["ssd_a_log"])),
              jnp.repeat(p["ssd_d"], SSD_HEAD_DIM)[None])
    wgate = jnp.stack([jnp.concatenate([_block_diag(p["rg_wa"][j]), _block_diag(p["rg_wx"][j])], axis=1)
                       for j in range(2)]).astype(BF16)
    bgate = jnp.concatenate([p["rg_ba"], p["rg_bx"]], axis=1)[:, None, :]
    hr = _rglru(u, wgate, bgate, p["rg_lambda"][:, None, :])

    off = 1 if last else 0
    wr = jnp.pad(p["w_router"], ((0, 0), (0, LANES - N_EXPERTS))).astype(BF16)
    x1, hp, aff, aff_t = _merge(attn_ctx, attn_lat, ys, z, hr, g, xt, modl,
                                p["attn_out_norm"][None], p["ssd_norm"][None], p["rg_out_norm"][None],
                                p["w_out"].astype(BF16), p["ln1_w"][None], p["ln1_b"][None], wr, ctx_row, off)

    if last:
        idx = _topk(aff_t, EC_CAPACITY * n_lat // N_EXPERTS)
    else:
        idx_lat = _topk(aff_t[:, :, n_ctx:], EC_CAPACITY * n_lat // N_EXPERTS) + n_ctx
        idx_ctx = _topk(aff_t[:, :, :n_ctx], EC_CAPACITY * n_ctx // N_EXPERTS)
        idx = jnp.concatenate([idx_lat, idx_ctx], axis=-1)
    moe = _moe(idx, hp, aff, *experts, layer, MOE_FF_TILE)
    return _ln2(x1, moe, modl, p["ln2_w"][None], p["ln2_b"][None], ctx_row, off)


def kernel(x, c, ctx, c_ctx, w_mod, b_mod, w_in, q_norm, k_norm, attn_out_norm, ssd_conv_w, ssd_conv_b, ssd_dt_bias, ssd_a_log, ssd_d, ssd_norm, rg_conv_w, rg_conv_b, rg_wa, rg_ba, rg_wx, rg_bx, rg_lambda, rg_out_norm, w_out, ln1_w, ln1_b, w_router, w_gate, w_up, w_down, ln2_w, ln2_b):
    b, n_lat, d = x.shape
    n_ctx = ctx.shape[1]
    assert n_ctx == SEQ_TILE and n_lat % 512 == 0 and b == SUBLANES
    params = dict(w_in=w_in, q_norm=q_norm, k_norm=k_norm, attn_out_norm=attn_out_norm, ssd_conv_w=ssd_conv_w,
                  ssd_conv_b=ssd_conv_b, ssd_dt_bias=ssd_dt_bias, ssd_a_log=ssd_a_log, ssd_d=ssd_d,
                  ssd_norm=ssd_norm, rg_conv_w=rg_conv_w, rg_conv_b=rg_conv_b, rg_wa=rg_wa, rg_ba=rg_ba,
                  rg_wx=rg_wx, rg_bx=rg_bx, rg_lambda=rg_lambda, rg_out_norm=rg_out_norm, w_out=w_out,
                  ln1_w=ln1_w, ln1_b=ln1_b, w_router=w_router, ln2_w=ln2_w, ln2_b=ln2_b)
    experts = (w_gate.astype(BF16), w_up.astype(BF16), w_down.astype(BF16))
    mod = _modulation(c, c_ctx, w_mod, b_mod)
    rope = _rope_tables(n_ctx, n_lat)
    xt = jnp.concatenate([ctx, x], axis=1)
    depth = w_mod.shape[0]
    for l in range(depth):
        p = {name: val[l] for name, val in params.items()}
        modl = mod[l].reshape(mod.shape[1], 1, mod.shape[2])
        xt = _layer(xt, modl, rope, p, experts, l, l == depth - 1, n_ctx, b)
    return xt
```

```python
import functools
import math

import jax
import jax.numpy as jnp
from jax import lax
from jax.experimental import pallas as pl
from jax.experimental.pallas import tpu as pltpu

F32 = jnp.float32
BF16 = jnp.bfloat16
I32 = jnp.int32

DEPTH = 2
GRID_W = 64
HEAD_DIM = 64
KV_HEADS = 2
Q_PER_KV = 4
AXIS_DIM = HEAD_DIM // 2
ROPE_THETA = 10000.0
SSD_HEADS = 4
SSD_HEAD_DIM = 64
SSD_STATE = 64
RG_BLOCKS = 4
RG_C = 8.0
CONV_W = 4
N_EXPERTS = 16
EC_CAPACITY = 2
EPS = 1e-6
DEEPNORM_ALPHA = (2 * DEPTH) ** 0.25

LANES = 128
SUBLANES = 8
BF16_ROWS = 16
SEQ_TILE = 256
ATTN_Q_TILE = 512
ATTN_KEY_CHUNK = 256
MERGE_ROW_SPLIT = 2
LN2_TILES = 4
SSD_BATCH = 4
MOE_FF_TILE = 1024
MOE_FF_SLICE = 256
MOE_ROW_GROUP = 4
LOG2E = math.log2(math.e)
VMEM_LIMIT = 48 * 1024 * 1024
VMEM_LIMIT_MOE = 60 * 1024 * 1024


def _cparams(limit=VMEM_LIMIT):
    return pltpu.CompilerParams(vmem_limit_bytes=limit)


def _sigmoid(x):
    return 1.0 / (1.0 + jnp.exp(-x))


def _silu(x):
    return x * _sigmoid(x)


def _softplus(x):
    return jnp.maximum(x, 0.0) + jnp.log1p(jnp.exp(-jnp.abs(x)))


def _bdot(a, b):
    return jnp.dot(a.astype(BF16), b.astype(BF16), preferred_element_type=F32)


def _mod_body(c_ref, w_ref, b_ref, o_ref):
    c = c_ref[...]
    o_ref[0] = _bdot(_silu(c), w_ref[0]) + b_ref[0]


def _modulation(c, c_ctx, w_mod, b_mod):
    depth, d, n6 = w_mod.shape
    b = c.shape[0]
    rows = 2 * SUBLANES
    cc = jnp.zeros((rows, d), F32).at[:b].set(c).at[b].set(c_ctx)
    tn = 1536
    return pl.pallas_call(
        _mod_body,
        grid=(depth, n6 // tn),
        in_specs=[
            pl.BlockSpec((rows, d), lambda l, j: (0, 0)),
            pl.BlockSpec((1, d, tn), lambda l, j: (l, 0, j)),
            pl.BlockSpec((1, 1, tn), lambda l, j: (l, 0, j)),
        ],
        out_specs=pl.BlockSpec((1, rows, tn), lambda l, j: (l, 0, j)),
        out_shape=jax.ShapeDtypeStruct((depth, rows, n6), F32),
        compiler_params=_cparams(),
        name="adaln_mod",
    )(cc, w_mod, b_mod.reshape(depth, 1, n6))


def _mod_spec(j, d, ctx_row, off):
    if off == 0:
        return pl.BlockSpec((1, 1, d), lambda b, i: (jnp.where(i == 0, ctx_row, b), 0, j))
    return pl.BlockSpec((1, 1, d), lambda b, i: (b, 0, j))


W_Q, W_K, W_V, W_XU, W_Z, W_G, W_DT = 512, 128, 128, 768, 256, 256, 256
IN_PAD = W_Q + W_K + W_V + W_XU + W_Z + W_G + W_DT


def _rope(t, cos, sin_signed):
    rows = t.shape[0]
    lane = lax.broadcasted_iota(I32, (rows, LANES), 1)
    first = (lane % AXIS_DIM) < (AXIS_DIM // 2)
    outs = []
    for c in range(t.shape[1] // LANES):
        tc = t[:, c * LANES:(c + 1) * LANES]
        partner = jnp.where(first, pltpu.roll(tc, LANES - AXIS_DIM // 2, 1), pltpu.roll(tc, AXIS_DIM // 2, 1))
        outs.append(tc * cos + partner * sin_signed)
    return outs[0] if len(outs) == 1 else jnp.concatenate(outs, axis=1)


def _inproj_body(x_ref, xp_ref, xn_ref, sc_ref, sh_ref, w_ref, cos_ref, sin_ref, qw_ref, kw_ref, oq_ref, ok_ref,
                 cw_ref, cb_ref, q_ref, k_ref, v_ref, xbc_ref, u_ref, z_ref, g_ref, dt_ref, qkv_scr):
    i = pl.program_id(1)
    n = pl.num_programs(1)
    scale = 1.0 + sc_ref[0]
    shift = sh_ref[0]
    h = (x_ref[0] * scale + shift).astype(BF16)
    n_qkv = W_Q + W_K + W_V
    qkv_scr[...] = jnp.dot(h, w_ref[:, :n_qkv], preferred_element_type=F32)
    rest = jnp.dot(h, w_ref[:, n_qkv:], preferred_element_type=F32)
    halo = (jnp.concatenate([xp_ref[0], xn_ref[0]], axis=0) * scale + shift).astype(BF16)
    xu_halo = jnp.dot(halo, w_ref[:, n_qkv:n_qkv + W_XU], preferred_element_type=F32)
    xu = rest[:, :W_XU]
    z_ref[0] = rest[:, W_XU:W_XU + W_Z]
    g_ref[0] = rest[:, W_XU + W_Z:W_XU + W_Z + W_G]
    dt_ref[0] = rest[:, W_XU + W_Z + W_G:]

    tl = xu.shape[0]
    has_prev = i > 1
    has_next = (i > 0) & (i < n - 1)
    pm = jnp.where(has_prev, xu_halo[SUBLANES - 1:SUBLANES, :], 0.0)
    n0 = jnp.where(has_next, xu_halo[SUBLANES:SUBLANES + 1, :], 0.0)
    n1 = jnp.where(has_next, xu_halo[SUBLANES + 1:SUBLANES + 2, :], 0.0)
    row = lax.broadcasted_iota(I32, xu.shape, 0)
    xm1 = jnp.where(row == 0, pm, pltpu.roll(xu, 1, 0))
    xp1 = jnp.where(row == tl - 1, n0, pltpu.roll(xu, tl - 1, 0))
    xp2 = jnp.where(row == tl - 1, n1, jnp.where(row == tl - 2, n0, pltpu.roll(xu, tl - 2, 0)))
    cw = cw_ref[...]
    yc = xm1 * cw[0:1] + xu * cw[1:2] + xp1 * cw[2:3] + xp2 * cw[3:4] + cb_ref[...]
    n_xbc = xbc_ref.shape[-1]
    xbc_ref[0] = _silu(yc[:, :n_xbc])
    u_ref[0] = yc[:, n_xbc:]

    q = qkv_scr[:, :W_Q]
    k = qkv_scr[:, W_Q:W_Q + W_K]
    cos = cos_ref[...]
    sin = sin_ref[...]
    ssq = jnp.dot((q * q).astype(BF16), oq_ref[...], preferred_element_type=F32)
    qn = q * lax.rsqrt(ssq * (1.0 / HEAD_DIM) + EPS) * qw_ref[...]
    q_ref[0] = (_rope(qn, cos, sin) * (HEAD_DIM ** -0.5 * LOG2E)).T.astype(BF16)
    ssk = jnp.dot((k * k).astype(BF16), ok_ref[...], preferred_element_type=F32)
    kn = k * lax.rsqrt(ssk * (1.0 / HEAD_DIM) + EPS) * kw_ref[...]
    k_ref[0] = _rope(kn, cos, sin).astype(BF16)
    v_ref[0] = qkv_scr[:, W_Q + W_K:].astype(BF16)


def _inproj(xt, modl, wcat, cos_t, sin_t, qw, kw, ones_q, ones_k, conv_w, conv_b, n_xbc, ctx_row):
    b, t, d = xt.shape
    tl = SEQ_TILE
    grid = (b, t // tl)
    r = tl // SUBLANES
    nblk = t // SUBLANES
    full = lambda shape: pl.BlockSpec(shape, lambda bb, i: (0,) * len(shape))
    seq = lambda w: pl.BlockSpec((1, tl, w), lambda bb, i: (bb, i, 0))
    outs = [(W_K, BF16), (W_V, BF16), (n_xbc, F32), (W_XU - n_xbc, F32), (W_Z, F32), (W_G, F32), (W_DT, F32)]
    return pl.pallas_call(
        _inproj_body,
        grid=grid,
        in_specs=[
            seq(d),
            pl.BlockSpec((1, SUBLANES, d), lambda bb, i: (bb, jnp.maximum(i * r - 1, 0), 0)),
            pl.BlockSpec((1, SUBLANES, d), lambda bb, i: (bb, jnp.minimum((i + 1) * r, nblk - 1), 0)),
            _mod_spec(1, d, ctx_row, 0),
            _mod_spec(0, d, ctx_row, 0),
            full((d, IN_PAD)),
            pl.BlockSpec((tl, LANES), lambda bb, i: (i, 0)),
            pl.BlockSpec((tl, LANES), lambda bb, i: (i, 0)),
            full((1, W_Q)), full((1, W_K)), full((W_Q, W_Q)), full((W_K, W_K)),
            full((CONV_W, W_XU)), full((1, W_XU)),
        ],
        out_specs=[pl.BlockSpec((1, W_Q, tl), lambda bb, i: (bb, 0, i))] + [seq(w) for w, _ in outs],
        out_shape=[jax.ShapeDtypeStruct((b, W_Q, t), BF16)]
        + [jax.ShapeDtypeStruct((b, t, w), dt) for w, dt in outs],
        scratch_shapes=[pltpu.VMEM((tl, W_Q + W_K + W_V), F32)],
        compiler_params=_cparams(),
        name="in_proj",
    )(xt, xt, xt, modl, modl, wcat, cos_t, sin_t, qw, kw, ones_q, ones_k, conv_w, conv_b)


def _attn_body(qt_ref, k_ref, vt_ref, o_ref, s_buf, p_buf, acc_ref, *, n_keys, kc):
    nh, hd, tq = qt_ref.shape[1:]
    qt = jnp.concatenate([qt_ref[0, h] for h in range(nh)], axis=1)
    cols = nh * tq
    rem = n_keys % kc
    chunks = ([(0, rem)] if rem else []) + [(rem + kc * i, kc) for i in range(n_keys // kc)]
    n = len(chunks)

    def scores(c):
        s0, sz = chunks[c]
        s_buf[c % 2, :sz, :] = jnp.dot(k_ref[0, 0, s0:s0 + sz, :], qt, preferred_element_type=F32)

    def softmax(c, m):
        sz = chunks[c][1]
        s = s_buf[c % 2, :sz, :]
        m_new = jnp.maximum(m, jnp.max(s, axis=0, keepdims=True))
        p_buf[c % 2, :sz, :] = jnp.exp2(s - m_new).astype(BF16)
        return m_new, jnp.exp2(m - m_new)

    def weighted_values(c, alpha):
        s0, sz = chunks[c]
        acc_ref[...] = alpha * acc_ref[...] + jnp.dot(vt_ref[0, 0, :, s0:s0 + sz], p_buf[c % 2, :sz, :],
                                                      preferred_element_type=F32)

    acc_ref[...] = jnp.zeros_like(acc_ref)
    m = jnp.full((1, cols), -jnp.inf, F32)
    alpha = None
    scores(0)
    for j in range(n + 1):
        if j + 1 < n:
            scores(j + 1)
        prev_alpha = alpha
        if j < n:
            m, alpha = softmax(j, m)
        if j >= 1:
            weighted_values(j - 1, prev_alpha)
    o = acc_ref[:hd, :] * (1.0 / acc_ref[hd:hd + 1, :])
    for h in range(nh):
        o_ref[0, h * hd:(h + 1) * hd, :] = o[:, h * tq:(h + 1) * tq]


def _attention(qt, k, vt, n_q, q_off, n_keys, tq):
    b = qt.shape[0]
    assert q_off % tq == 0
    off = q_off // tq
    kc = min(ATTN_KEY_CHUNK, n_keys)
    cols = Q_PER_KV * tq
    vrows = vt.shape[2]
    return pl.pallas_call(
        functools.partial(_attn_body, n_keys=n_keys, kc=kc),
        grid=(b, KV_HEADS, n_q // tq),
        scratch_shapes=[pltpu.VMEM((2, kc, cols), F32), pltpu.VMEM((2, kc, cols), BF16),
                        pltpu.VMEM((vrows, cols), F32)],
        in_specs=[
            pl.BlockSpec((1, Q_PER_KV, HEAD_DIM, tq), lambda bb, g, i: (bb, g, 0, i + off)),
            pl.BlockSpec((1, 1, n_keys, HEAD_DIM), lambda bb, g, i: (bb, g, 0, 0)),
            pl.BlockSpec((1, 1, vrows, n_keys), lambda bb, g, i: (bb, g, 0, 0)),
        ],
        out_specs=pl.BlockSpec((1, Q_PER_KV * HEAD_DIM, tq), lambda bb, g, i: (bb, g, i)),
        out_shape=jax.ShapeDtypeStruct((b, KV_HEADS * Q_PER_KV * HEAD_DIM, n_q), F32),
        compiler_params=_cparams(),
        name="gqa_attention",
    )(qt, k, vt)


def _seq_order(d, c, nc):
    return jnp.where(c == 0, 0, jnp.where(d == 0, c, nc - c))


def _ssd_body(xf_ref, xb_ref, dtf_ref, dtb_ref, bias_ref, a_ref, dvec_ref, yf_ref, yb_ref, s_scr):
    @pl.when(pl.program_id(1) == 0)
    def _():
        s_scr[...] = jnp.zeros_like(s_scr)

    nb = xf_ref.shape[0]
    streams = range(2 * nb)
    xbc = [(xf_ref, xb_ref)[k % 2][k // 2] for k in streams]
    q = xbc[0].shape[0]
    xw = SSD_HEADS * SSD_HEAD_DIM
    gw = 2 * SSD_STATE
    x = [v[:, :xw] for v in xbc]
    bm = [v[:, xw:xw + gw] for v in xbc]
    cm = [v[:, xw + gw:xw + 2 * gw] for v in xbc]
    dt_raw = [(dtf_ref, dtb_ref)[k % 2][k // 2] for k in streams]
    dt = [_softplus(dt_raw[k] + bias_ref[k % 2]) for k in streams]
    da = [dt[k] * a_ref[k % 2] for k in streams]
    ii = lax.broadcasted_iota(I32, (q, q), 0)
    jj = lax.broadcasted_iota(I32, (q, q), 1)
    mask = [jj <= ii, jj >= ii]
    tm = [jnp.where(m_, 1.0, 0.0).astype(BF16) for m_ in mask]
    da_hi = [da[k].astype(BF16) for k in streams]
    da_lo = [(da[k] - da_hi[k].astype(F32)).astype(BF16) for k in streams]
    cs = [jnp.dot(tm[k % 2], da_hi[k], preferred_element_type=F32)
          + jnp.dot(tm[k % 2], da_lo[k], preferred_element_type=F32) for k in streams]
    tot = [cs[k][q - 1:q, :] if k % 2 == 0 else cs[k][0:1, :] for k in streams]
    cst = [cs[k].T for k in streams]
    dec = [jnp.exp(tot[k] - cs[k]) for k in streams]
    ecs = [jnp.exp(cs[k]) for k in streams]
    etot = [jnp.exp(tot[k]) for k in streams]
    ys = [[] for _ in streams]
    nt = (((1,), (1,)), ((), ()))
    tn = (((0,), (0,)), ((), ()))
    for g in range(2):
        gs = slice(g * SSD_STATE, (g + 1) * SSD_STATE)
        bg = [bm[k][:, gs].astype(BF16) for k in streams]
        cg = [cm[k][:, gs].astype(BF16) for k in streams]
        gmat = [lax.dot_general(cg[k], bg[k], nt, preferred_element_type=F32) for k in streams]
        for hh in range(SSD_HEADS // 2):
            h = 2 * g + hh
            hs = slice(h * SSD_HEAD_DIM, (h + 1) * SSD_HEAD_DIM)
            lmat = [jnp.exp(jnp.where(mask[k % 2], cs[k][:, h:h + 1] - cst[k][h:h + 1, :], -jnp.inf))
                    for k in streams]
            xdt = [x[k][:, hs] * dt[k][:, h:h + 1] for k in streams]
            y_diag = [_bdot(gmat[k] * lmat[k], xdt[k]) for k in streams]
            s_in = [s_scr[k, h] for k in streams]
            y_off = [lax.dot_general(cg[k], s_in[k].astype(BF16), nt, preferred_element_type=F32) * ecs[k][:, h:h + 1]
                     for k in streams]
            xd = [(xdt[k] * dec[k][:, h:h + 1]).astype(BF16) for k in streams]
            for k in streams:
                ys[k].append(y_diag[k] + y_off[k])
                s_scr[k, h] = etot[k][:, h:h + 1] * s_in[k] + lax.dot_general(xd[k], bg[k], tn,
                                                                               preferred_element_type=F32)
    for k in streams:
        if k % 2 == 0:
            yf_ref[k // 2] = jnp.concatenate(ys[k], axis=1) + x[k] * dvec_ref[...]
        else:
            yb_ref[k // 2] = jnp.concatenate(ys[k], axis=1)


def _ssd(xbc, dt, bias, a_neg, dvec):
    b, t, cw = xbc.shape
    q = SEQ_TILE
    nc = t // q
    xw = SSD_HEADS * SSD_HEAD_DIM
    fwd = lambda c: c
    bwd = lambda c: _seq_order(1, c, nc)
    nb = SSD_BATCH
    return pl.pallas_call(
        _ssd_body,
        grid=(b // nb, nc),
        in_specs=[
            pl.BlockSpec((nb, q, cw), lambda bb, c: (bb, fwd(c), 0)),
            pl.BlockSpec((nb, q, cw), lambda bb, c: (bb, bwd(c), 0)),
            pl.BlockSpec((nb, q, LANES), lambda bb, c: (bb, fwd(c), 0)),
            pl.BlockSpec((nb, q, LANES), lambda bb, c: (bb, bwd(c), 1)),
            pl.BlockSpec((2, 1, LANES), lambda bb, c: (0, 0, 0)),
            pl.BlockSpec((2, 1, LANES), lambda bb, c: (0, 0, 0)),
            pl.BlockSpec((1, xw), lambda bb, c: (0, 0)),
        ],
        out_specs=[pl.BlockSpec((nb, q, xw), lambda bb, c: (bb, fwd(c), 0)),
                   pl.BlockSpec((nb, q, xw), lambda bb, c: (bb, bwd(c), 0))],
        out_shape=[jax.ShapeDtypeStruct((b, t, xw), F32)] * 2,
        scratch_shapes=[pltpu.VMEM((2 * nb, SSD_HEADS, SSD_HEAD_DIM, SSD_STATE), F32)],
        compiler_params=_cparams(),
        name="ssd_scan",
    )(xbc, xbc, dt, dt, bias, a_neg, dvec)


def _rg_body(u_ref, w_ref, bias_ref, lam_ref, y_ref, a_s, v_s, o_s, h_s, *, pitch):
    d = pl.program_id(0)
    c = pl.program_id(1)
    nb, tl, width = u_ref.shape
    ng = width // LANES

    @pl.when(c == 0)
    def _():
        h_s[...] = jnp.zeros_like(h_s)

    sp = _softplus(-lam_ref[0])
    w = w_ref[0]
    bias = bias_ref[0]
    for b in range(nb):
        ub = u_ref[b]
        pre = jnp.dot(ub.astype(BF16), w, preferred_element_type=F32) + bias
        r = _sigmoid(pre[:, :width])
        ig = _sigmoid(pre[:, width:])
        a = jnp.exp((-RG_C) * r * sp)
        v = jnp.sqrt(1.0 - a * a) * ig * ub
        for j in range(ng):
            a_s[j, pl.ds(b * pitch, tl), :] = a[:, j * LANES:(j + 1) * LANES]
            v_s[j, pl.ds(b * pitch, tl), :] = v[:, j * LANES:(j + 1) * LANES]

    def step(t, hs):
        te = jnp.where(d == 0, t, tl - 1 - t)
        out = []
        for j in range(ng):
            at = a_s[j, pl.ds(te, nb, stride=pitch), :]
            vt = v_s[j, pl.ds(te, nb, stride=pitch), :]
            hj = at * hs[j] + vt
            o_s[j, pl.ds(te, nb, stride=pitch), :] = hj
            out.append(hj)
        return tuple(out)

    h0 = tuple(h_s[:, j * LANES:(j + 1) * LANES] for j in range(ng))
    hf = lax.fori_loop(0, tl, step, h0, unroll=8)
    for j in range(ng):
        h_s[:, j * LANES:(j + 1) * LANES] = hf[j]
    for b in range(nb):
        y_ref[0, b] = jnp.concatenate([o_s[j, pl.ds(b * pitch, tl), :] for j in range(ng)], axis=1)


def _rglru(u, wg, bias, lam):
    b, t, width = u.shape
    assert b == SUBLANES, "the recurrence keeps one sample per sublane"
    tl = SEQ_TILE
    nc = t // tl
    pitch = tl + SUBLANES
    ng = width // LANES
    slab = pltpu.VMEM((ng, b * pitch, LANES), F32)
    return pl.pallas_call(
        functools.partial(_rg_body, pitch=pitch),
        grid=(2, nc),
        in_specs=[
            pl.BlockSpec((b, tl, width), lambda d, c: (0, _seq_order(d, c, nc), 0)),
            pl.BlockSpec((1, width, 2 * width), lambda d, c: (d, 0, 0)),
            pl.BlockSpec((1, 1, 2 * width), lambda d, c: (d, 0, 0)),
            pl.BlockSpec((1, 1, width), lambda d, c: (d, 0, 0)),
        ],
        out_specs=pl.BlockSpec((1, b, tl, width), lambda d, c: (d, 0, _seq_order(d, c, nc), 0)),
        out_shape=jax.ShapeDtypeStruct((2, b, t, width), F32),
        scratch_shapes=[slab, slab, slab, pltpu.VMEM((b, width), F32)],
        compiler_params=_cparams(),
        name="rglru_scan",
    )(u, wg, bias, lam)


def _rms(x, w):
    return x * lax.rsqrt(jnp.mean(x * x, axis=-1, keepdims=True) + EPS) * w


def _layer_norm(t, w, b):
    mu = jnp.mean(t, axis=-1, keepdims=True)
    tc = t - mu
    var = jnp.mean(tc * tc, axis=-1, keepdims=True)
    return tc * lax.rsqrt(var + EPS) * w + b


def _gelu_tanh(x):
    return 0.5 * x * (1.0 + jnp.tanh(math.sqrt(2.0 / math.pi) * (x + 0.044715 * (x * x * x))))


def _pack_bf16_pair(lo, hi):
    lb = pltpu.bitcast(lo.astype(BF16).astype(F32), jnp.uint32)
    hb = pltpu.bitcast(hi.astype(BF16).astype(F32), jnp.uint32)
    return (lb >> 16) | (hb & jnp.uint32(0xFFFF0000))


def _merge_body(*refs, has_ctx):
    if has_ctx:
        actx_ref, refs = refs[0], refs[1:]
    (alat_ref, ysf_ref, ysb_ref, z_ref, hr_ref, g_ref, x_ref, g1_ref, sc2_ref, sh2_ref, aw_ref, sw_ref, rw_ref,
     wo_ref, lnw_ref, lnb_ref, wr_ref, x1_ref, hp_ref, aff_ref, afft_ref, cat_scr, proj_scr) = refs
    tl = x_ref.shape[1]
    nh = MERGE_ROW_SPLIT
    rows = [slice(j * tl // nh, (j + 1) * tl // nh) for j in range(nh)]
    for rs in rows:
        a = alat_ref[0, :, rs]
        if has_ctx:
            a = jnp.where(pl.program_id(1) == 0, actx_ref[0, :, rs], a)
        an = _rms(a.T, aw_ref[...])
        sn = _rms((ysf_ref[0, rs] + ysb_ref[0, rs]) * _silu(z_ref[0, rs]), sw_ref[...])
        rn = _rms((hr_ref[0, 0, rs] + hr_ref[1, 0, rs]) * _gelu_tanh(g_ref[0, rs]), rw_ref[...])
        cat_scr[rs] = jnp.concatenate([an, sn, rn], axis=1).astype(BF16)
    for rs in rows:
        proj_scr[rs] = jnp.dot(cat_scr[rs], wo_ref[...], preferred_element_type=F32)
    for rs in rows:
        x1 = _layer_norm(DEEPNORM_ALPHA * x_ref[0, rs] + g1_ref[0] * proj_scr[rs], lnw_ref[...], lnb_ref[...])
        x1_ref[0, rs] = x1
        h2 = x1 * (1.0 + sc2_ref[0]) + sh2_ref[0]
        half = h2.shape[1] // 2
        hp_ref[0, rs] = _pack_bf16_pair(h2[:, :half], h2[:, half:])
        logits = jnp.dot(h2.astype(BF16), wr_ref[...], preferred_element_type=F32)
        lane = lax.broadcasted_iota(I32, logits.shape, 1)
        logits = jnp.where(lane < N_EXPERTS, logits, -jnp.inf)
        e = jnp.exp(logits - jnp.max(logits, axis=-1, keepdims=True))
        aff = e / jnp.sum(e, axis=-1, keepdims=True)
        aff_ref[0, rs] = aff
        afft_ref[0, :, rs] = aff.T[:N_EXPERTS, :]


def _merge(attn_ctx, attn_lat, ys, z, hr, g, xt, modl, aw, sw, rw, wo, lnw, lnb, wr, ctx_row, off):
    b, t, d = xt.shape
    tl = SEQ_TILE
    nt = t // tl - off
    has_ctx = off == 0
    aw_ = attn_lat.shape[1]
    seq = lambda w: pl.BlockSpec((1, tl, w), lambda bb, i: (bb, i + off, 0))
    pair = lambda w: pl.BlockSpec((2, 1, tl, w), lambda bb, i: (0, bb, i + off, 0))
    full = lambda shape: pl.BlockSpec(shape, lambda bb, i: (0,) * len(shape))
    out = lambda w: pl.BlockSpec((1, tl, w), lambda bb, i: (bb, i, 0))
    lat_off = 1 - off
    in_specs = [
        pl.BlockSpec((1, aw_, tl), lambda bb, i: (bb, 0, jnp.maximum(i - lat_off, 0))),
        seq(ys[0].shape[-1]), seq(ys[1].shape[-1]), seq(z.shape[-1]), pair(hr.shape[-1]), seq(g.shape[-1]), seq(d),
        _mod_spec(2, d, ctx_row, off), _mod_spec(4, d, ctx_row, off), _mod_spec(3, d, ctx_row, off),
        full(aw.shape), full(sw.shape), full(rw.shape), full(wo.shape), full(lnw.shape), full(lnb.shape),
        full(wr.shape),
    ]
    args = [attn_lat, ys[0], ys[1], z, hr, g, xt, modl, modl, modl, aw, sw, rw, wo, lnw, lnb, wr]
    if has_ctx:
        in_specs = [pl.BlockSpec((1, aw_, tl), lambda bb, i: (bb, 0, 0))] + in_specs
        args = [attn_ctx] + args
    rows = nt * tl
    return pl.pallas_call(
        functools.partial(_merge_body, has_ctx=has_ctx),
        grid=(b, nt),
        in_specs=in_specs,
        out_specs=[out(d), out(d // 2), out(LANES), pl.BlockSpec((1, N_EXPERTS, tl), lambda bb, i: (bb, 0, i))],
        out_shape=[
            jax.ShapeDtypeStruct((b, rows, d), F32),
            jax.ShapeDtypeStruct((b, rows, d // 2), jnp.uint32),
            jax.ShapeDtypeStruct((b, rows, LANES), F32),
            jax.ShapeDtypeStruct((b, N_EXPERTS, rows), F32),
        ],
        scratch_shapes=[pltpu.VMEM((tl, wo.shape[0]), BF16), pltpu.VMEM((tl, d), F32)],
        compiler_params=_cparams(),
        name="merge_outproj_ln1_router",
    )(*args)


def _topk_body(aff_ref, idx_ref, cum_scr, *, cap):
    aff = aff_ref[0]
    ne, n = aff.shape
    bits = pltpu.bitcast(aff, I32)

    def search(i, thr):
        cand = thr | lax.shift_left(jnp.int32(1), 30 - i)
        cnt = jnp.sum(jnp.where(bits >= cand, 1.0, 0.0), axis=1, keepdims=True)
        return jnp.where(cnt >= cap, cand, thr)

    thr = lax.fori_loop(0, 31, search, jnp.zeros((ne, 1), I32))
    gt = jnp.where(bits > thr, 1.0, 0.0)
    eq = jnp.where(bits == thr, 1.0, 0.0)
    need = cap - jnp.sum(gt, axis=1, keepdims=True)
    nblk = n // LANES
    r_ = lax.broadcasted_iota(I32, (LANES, LANES), 0)
    c_ = lax.broadcasted_iota(I32, (LANES, LANES), 1)
    upper = jnp.where(r_ <= c_, 1.0, 0.0).astype(BF16)

    def prefix(blocks):
        outs = []
        off = jnp.zeros((ne, 1), F32)
        for mk in blocks:
            w = jnp.dot(mk.astype(BF16), upper, preferred_element_type=F32) + off
            outs.append(w)
            off = w[:, LANES - 1:LANES]
        return outs

    blk = lambda a, k: a[:, k * LANES:(k + 1) * LANES]
    tie_rank = prefix([blk(eq, k) for k in range(nblk)])
    cum = prefix([jnp.maximum(blk(gt, k), blk(eq, k) * jnp.where(tie_rank[k] <= need, 1.0, 0.0))
                  for k in range(nblk)])
    kpad = cum_scr.shape[0] // ne
    for k in range(kpad):
        cum_scr[k * ne:(k + 1) * ne, :] = cum[k] if k < nblk else jnp.zeros((ne, LANES), F32)
    width = -(-cap // LANES) * LANES
    slot = lax.broadcasted_iota(I32, (LANES, width), 1).astype(F32)
    blk_id = lax.broadcasted_iota(I32, (LANES, width), 0).astype(F32)
    blk_col = lax.broadcasted_iota(I32, (LANES, 1), 0)
    tn = (((0,), (0,)), ((), ()))

    def per_expert(e, carry):
        cum_e = cum_scr[pl.ds(e, kpad, stride=ne), :]
        if kpad < LANES:
            cum_e = jnp.concatenate([cum_e, jnp.zeros((LANES - kpad, LANES), F32)], axis=0)
        ends = jnp.where(blk_col < nblk, cum_e[:, LANES - 1:LANES], float(2 * n))
        full = jnp.sum(jnp.where(ends <= slot, 1.0, 0.0), axis=0, keepdims=True)
        pick = jnp.where(blk_id == full, 1.0, 0.0).astype(BF16)
        hi = jnp.floor(cum_e * (1.0 / 32.0))
        lo = cum_e - 32.0 * hi
        straddle = (32.0 * lax.dot_general(hi.astype(BF16), pick, tn, preferred_element_type=F32)
                    + lax.dot_general(lo.astype(BF16), pick, tn, preferred_element_type=F32))
        inside = jnp.sum(jnp.where(straddle <= slot, 1.0, 0.0), axis=0, keepdims=True)
        idx_ref[0, pl.ds(e, 1), :] = (float(LANES) * full + inside)[:, :cap].astype(I32)
        return carry

    lax.fori_loop(0, ne, per_expert, 0, unroll=4)


def _topk(aff_t, cap):
    b, e, n = aff_t.shape
    return pl.pallas_call(
        functools.partial(_topk_body, cap=cap),
        grid=(b,),
        in_specs=[pl.BlockSpec((1, e, n), lambda bb: (bb, 0, 0))],
        out_specs=pl.BlockSpec((1, e, cap), lambda bb: (bb, 0, 0)),
        out_shape=jax.ShapeDtypeStruct((b, e, cap), I32),
        scratch_shapes=[pltpu.VMEM((max(n // LANES, SUBLANES) * e, LANES), F32)],
        compiler_params=_cparams(),
        name="expert_choice_topk",
    )(aff_t)


def _moe_body(idxp_ref, idxc_ref, idxn_ref, hp_ref, aff_ref, wg_ref, wu_ref, wd_ref, out_ref,
              xs_scr, ag_scr, y_scr, xb_scr, hid_scr, *, cap):
    e = pl.program_id(1)
    f = pl.program_id(2)
    ne = pl.num_programs(1)
    nf = pl.num_programs(2)
    cur = e % 2
    oth = 1 - cur

    @pl.when((pl.program_id(0) == 0) & (e == 0) & (f == 0))
    def _():
        y_scr[...] = jnp.zeros_like(y_scr)

    @pl.when((e == 0) & (f == 0))
    def _():
        out_ref[...] = jnp.zeros_like(out_ref)

        def gather(j, carry):
            for u in range(SUBLANES):
                t = idxc_ref[0, 0, 0, j * SUBLANES + u]
                xs_scr[0, j, u:u + 1, :] = hp_ref[0, pl.ds(t, 1), :]
                ag_scr[0, j, u:u + 1, :] = aff_ref[0, pl.ds(t, 1), :]
            return carry

        lax.fori_loop(0, cap // SUBLANES, gather, 0)

    w = xs_scr[cur].reshape(cap, xs_scr.shape[-1])
    half = w.shape[1]
    xb_scr[:, :half] = pltpu.bitcast(w << 16, F32).astype(BF16)
    xb_scr[:, half:] = pltpu.bitcast(w & jnp.uint32(0xFFFF0000), F32).astype(BF16)
    lane = lax.broadcasted_iota(I32, (cap, LANES), 1)
    gate = jnp.sum(jnp.where(lane == e, ag_scr[cur].reshape(cap, LANES), 0.0), axis=1, keepdims=True)
    share = cap // nf
    base = f * share
    base_tile = f * (share // SUBLANES)
    has_prev = e > 0

    def row_copies(r0, r1):
        for g0 in range(r0, r1, MOE_ROW_GROUP):
            rows = range(g0, min(g0 + MOE_ROW_GROUP, r1))
            tps = [idxp_ref[0, 0, 0, base + r] for r in rows]
            sums = [out_ref[0, pl.ds(tp, 1), :]
                    + jnp.where(has_prev, y_scr[oth, base_tile + r // SUBLANES, r % SUBLANES:r % SUBLANES + 1, :], 0.0)
                    for tp, r in zip(tps, rows)]
            for tp, v in zip(tps, sums):
                out_ref[0, pl.ds(tp, 1), :] = v
            for r in rows:
                tn = idxn_ref[0, 0, 0, base + r]
                j, u = base_tile + r // SUBLANES, r % SUBLANES
                xs_scr[oth, j, u:u + 1, :] = hp_ref[0, pl.ds(tn, 1), :]
                ag_scr[oth, j, u:u + 1, :] = aff_ref[0, pl.ds(tn, 1), :]

    tf = wg_ref.shape[3]
    nsl = tf // MOE_FF_SLICE
    sl = lambda c: slice(c * MOE_FF_SLICE, (c + 1) * MOE_FF_SLICE)

    def hidden(c):
        xb = xb_scr[...]
        hg = jnp.dot(xb, wg_ref[0, 0, :, sl(c)], preferred_element_type=F32)
        hu = jnp.dot(xb, wu_ref[0, 0, :, sl(c)], preferred_element_type=F32)
        hid_scr[:, sl(c)] = (_silu(hg) * hu).astype(BF16)

    for c in range(nsl):
        hidden(c)
        row_copies(share * c // nsl, share * (c + 1) // nsl)
    yp = jnp.dot(hid_scr[...], wd_ref[0, 0], preferred_element_type=F32)
    y_old = y_scr[cur].reshape(cap, y_scr.shape[-1])
    y_new = (jnp.where(f == 0, 0.0, y_old) + yp) * jnp.where(f == nf - 1, gate, 1.0)
    y_scr[cur] = y_new.reshape(y_scr.shape[1:])

    @pl.when((e == ne - 1) & (f == nf - 1))
    def _():
        def scatter(j, carry):
            for u in range(SUBLANES):
                t = idxc_ref[0, 0, 0, j * SUBLANES + u]
                out_ref[0, pl.ds(t, 1), :] = out_ref[0, pl.ds(t, 1), :] + y_scr[cur, j, u:u + 1, :]
            return carry

        lax.fori_loop(0, cap // SUBLANES, scatter, 0)


def _moe(idx, hp, aff, wg, wu, wd, layer, tf):
    b, t, half = hp.shape
    d = 2 * half
    ne, ff = wg.shape[1], wg.shape[3]
    cap = idx.shape[-1]
    nf = ff // tf
    assert nf >= 2 and cap % nf == 0
    one = pl.Buffered(1)
    idx4 = idx.reshape(b, ne, 1, cap)
    smem = lambda shift: pl.BlockSpec((1, 1, 1, cap), lambda bb, e, f: (bb, jnp.clip(e + shift, 0, ne - 1), 0, 0),
                                      memory_space=pltpu.SMEM)
    return pl.pallas_call(
        functools.partial(_moe_body, cap=cap),
        grid=(b, ne, nf),
        in_specs=[
            smem(-1), smem(0), smem(1),
            pl.BlockSpec((1, t, half), lambda bb, e, f: (bb, 0, 0), pipeline_mode=one),
            pl.BlockSpec((1, t, LANES), lambda bb, e, f: (bb, 0, 0), pipeline_mode=one),
            pl.BlockSpec((1, 1, d, tf), lambda bb, e, f: (layer, e, 0, f)),
            pl.BlockSpec((1, 1, d, tf), lambda bb, e, f: (layer, e, 0, f)),
            pl.BlockSpec((1, 1, tf, d), lambda bb, e, f: (layer, e, f, 0)),
        ],
        out_specs=pl.BlockSpec((1, t, d), lambda bb, e, f: (bb, 0, 0), pipeline_mode=one),
        out_shape=jax.ShapeDtypeStruct((b, t, d), F32),
        scratch_shapes=[
            pltpu.VMEM((2, cap // SUBLANES, SUBLANES, half), jnp.uint32),
            pltpu.VMEM((2, cap // SUBLANES, SUBLANES, LANES), F32),
            pltpu.VMEM((2, cap // SUBLANES, SUBLANES, d), F32),
            pltpu.VMEM((cap, d), BF16),
            pltpu.VMEM((cap, tf), BF16),
        ],
        compiler_params=_cparams(VMEM_LIMIT_MOE),
        name="expert_ffn",
    )(idx4, idx4, idx4, hp, aff, wg, wu, wd)


def _ln2_body(x1_ref, moe_ref, g2_ref, g2c_ref, w_ref, b_ref, o_ref, *, n_ctx):
    g2 = g2_ref[0]
    if n_ctx:
        tl = x1_ref.shape[1]
        row = lax.broadcasted_iota(I32, (tl, 1), 0) + pl.program_id(1) * tl
        g2 = jnp.where(row < n_ctx, g2c_ref[0], g2)
    o_ref[0] = _layer_norm(DEEPNORM_ALPHA * x1_ref[0] + g2 * moe_ref[0], w_ref[...], b_ref[...])


def _ln2(x1, moe, modl, w, bias, ctx_row, n_ctx):
    b, t, d = x1.shape
    tl = t // LN2_TILES
    assert t % LN2_TILES == 0 and tl % SUBLANES == 0
    seq = pl.BlockSpec((1, tl, d), lambda bb, i: (bb, i, 0))
    full = pl.BlockSpec((1, d), lambda bb, i: (0, 0))
    return pl.pallas_call(
        functools.partial(_ln2_body, n_ctx=n_ctx),
        grid=(b, LN2_TILES),
        in_specs=[seq, seq,
                  pl.BlockSpec((1, 1, d), lambda bb, i: (bb, 0, 5)),
                  pl.BlockSpec((1, 1, d), lambda bb, i: (ctx_row, 0, 5)),
                  full, full],
        out_specs=seq,
        out_shape=jax.ShapeDtypeStruct((b, t, d), F32),
        compiler_params=_cparams(),
        name="ln2",
    )(x1, moe, modl, modl, w, bias)


def _block_diag_ones(width, block):
    r = jnp.arange(width)[:, None] // block
    c = jnp.arange(width)[None, :] // block
    return (r == c).astype(BF16)


def _block_diag(w):
    k, d, e = w.shape
    eye = jnp.eye(k, dtype=w.dtype)
    return (eye[:, None, :, None] * w[:, :, None, :]).reshape(k * d, k * e)


def _rope_tables(n_ctx, n_lat):
    pos = jnp.arange(n_lat)
    row = (pos // GRID_W).astype(F32)
    col = (pos % GRID_W).astype(F32)
    inv_freq = ROPE_THETA ** (-jnp.arange(0, AXIS_DIM, 2, dtype=F32) / AXIS_DIM)
    ang_r = row[:, None] * inv_freq
    ang_c = col[:, None] * inv_freq
    cos_h = jnp.concatenate([jnp.cos(ang_r)] * 2 + [jnp.cos(ang_c)] * 2, axis=1)
    sin_h = jnp.concatenate([-jnp.sin(ang_r), jnp.sin(ang_r), -jnp.sin(ang_c), jnp.sin(ang_c)], axis=1)
    reps = LANES // HEAD_DIM
    cos_t = jnp.concatenate([jnp.ones((n_ctx, LANES), F32), jnp.tile(cos_h, (1, reps))], axis=0)
    sin_t = jnp.concatenate([jnp.zeros((n_ctx, LANES), F32), jnp.tile(sin_h, (1, reps))], axis=0)
    return cos_t, sin_t


def _layer(xt, modl, rope, p, experts, layer, last, n_ctx, ctx_row):
    b, t, d = xt.shape
    n_lat = t - n_ctx
    cos_t, sin_t = rope
    w = p["w_in"]
    pad = jnp.zeros((d, LANES - SSD_HEADS), F32)
    wcat = jnp.concatenate(
        [w[:, 0:768], w[:, 768:1280], w[:, 1544:1800], w[:, 1280:1536], w[:, 1800:2056],
         w[:, 1536:1540], pad, w[:, 1540:1544], pad], axis=1).astype(BF16)
    qw = jnp.tile(p["q_norm"], W_Q // HEAD_DIM)[None]
    kw = jnp.tile(p["k_norm"], W_K // HEAD_DIM)[None]
    conv_w = jnp.concatenate([p["ssd_conv_w"], p["rg_conv_w"]], axis=1)
    conv_b = jnp.concatenate([p["ssd_conv_b"], p["rg_conv_b"]])[None]
    qt, k, v, xbc, u, z, g, dt = _inproj(xt, modl, wcat, cos_t, sin_t, qw, kw,
                                         _block_diag_ones(W_Q, HEAD_DIM), _block_diag_ones(W_K, HEAD_DIM),
                                         conv_w, conv_b, p["ssd_conv_w"].shape[1], ctx_row)

    qt = qt.reshape(b, W_Q // HEAD_DIM, HEAD_DIM, t)
    kh = k.reshape(b, t, KV_HEADS, HEAD_DIM).transpose(0, 2, 1, 3)
    vt = v.reshape(b, t, KV_HEADS, HEAD_DIM).transpose(0, 2, 3, 1)
    ones_pad = jnp.zeros((b, KV_HEADS, BF16_ROWS, t), BF16).at[:, :, 0, :].set(1.0)
    vt = jnp.concatenate([vt, ones_pad], axis=2)
    attn_lat = _attention(qt[..., n_ctx:], kh, vt, n_lat, 0, t, ATTN_Q_TILE)
    attn_ctx = None if last else _attention(qt, kh, vt, n_ctx, 0, n_ctx, min(ATTN_Q_TILE, n_ctx))

    lane_pad = lambda a: jnp.pad(a, ((0, 0), (0, LANES - a.shape[1])))[:, None, :]
    ys = _ssd(xbc, dt, lane_pad(p["ssd_dt_bias"]), lane_pad(-jnp.exp(p["ssd_a_log"])),
              jnp.repeat(p["ssd_d"], SSD_HEAD_DIM)[None])
    wgate = jnp.stack([jnp.concatenate([_block_diag(p["rg_wa"][j]), _block_diag(p["rg_wx"][j])], axis=1)
                       for j in range(2)]).astype(BF16)
    bgate = jnp.concatenate([p["rg_ba"], p["rg_bx"]], axis=1)[:, None, :]
    hr = _rglru(u, wgate, bgate, p["rg_lambda"][:, None, :])

    off = 1 if last else 0
    wr = jnp.pad(p["w_router"], ((0, 0), (0, LANES - N_EXPERTS))).astype(BF16)
    x1, hp, aff, aff_t = _merge(attn_ctx, attn_lat, ys, z, hr, g, xt, modl,
                                p["attn_out_norm"][None], p["ssd_norm"][None], p["rg_out_norm"][None],
                                p["w_out"].astype(BF16), p["ln1_w"][None], p["ln1_b"][None], wr, ctx_row, off)

    if last:
        idx = _topk(aff_t, EC_CAPACITY * n_lat // N_EXPERTS)
    else:
        idx_lat = _topk(aff_t[:, :, n_ctx:], EC_CAPACITY * n_lat // N_EXPERTS) + n_ctx
        idx_ctx = _topk(aff_t[:, :, :n_ctx], EC_CAPACITY * n_ctx // N_EXPERTS)
        idx = jnp.concatenate([idx_lat, idx_ctx], axis=-1)
    moe = _moe(idx, hp, aff, *experts, layer, MOE_FF_TILE)
    return _ln2(x1, moe, modl, p["ln2_w"][None], p["ln2_b"][None], ctx_row, 0 if last else n_ctx)


def kernel(x, c, ctx, c_ctx, w_mod, b_mod, w_in, q_norm, k_norm, attn_out_norm, ssd_conv_w, ssd_conv_b, ssd_dt_bias, ssd_a_log, ssd_d, ssd_norm, rg_conv_w, rg_conv_b, rg_wa, rg_ba, rg_wx, rg_bx, rg_lambda, rg_out_norm, w_out, ln1_w, ln1_b, w_router, w_gate, w_up, w_down, ln2_w, ln2_b):
    b, n_lat, d = x.shape
    n_ctx = ctx.shape[1]
    assert n_ctx == SEQ_TILE and n_lat % 512 == 0 and b == SUBLANES
    params = dict(w_in=w_in, q_norm=q_norm, k_norm=k_norm, attn_out_norm=attn_out_norm, ssd_conv_w=ssd_conv_w,
                  ssd_conv_b=ssd_conv_b, ssd_dt_bias=ssd_dt_bias, ssd_a_log=ssd_a_log, ssd_d=ssd_d,
                  ssd_norm=ssd_norm, rg_conv_w=rg_conv_w, rg_conv_b=rg_conv_b, rg_wa=rg_wa, rg_ba=rg_ba,
                  rg_wx=rg_wx, rg_bx=rg_bx, rg_lambda=rg_lambda, rg_out_norm=rg_out_norm, w_out=w_out,
                  ln1_w=ln1_w, ln1_b=ln1_b, w_router=w_router, ln2_w=ln2_w, ln2_b=ln2_b)
    experts = (w_gate.astype(BF16), w_up.astype(BF16), w_down.astype(BF16))
    mod = _modulation(c, c_ctx, w_mod, b_mod)
    rope = _rope_tables(n_ctx, n_lat)
    xt = jnp.concatenate([ctx, x], axis=1)
    depth = w_mod.shape[0]
    for l in range(depth):
        p = {name: val[l] for name, val in params.items()}
        modl = mod[l].reshape(mod.shape[1], 1, mod.shape[2])
        xt = _layer(xt, modl, rope, p, experts, l, l == depth - 1, n_ctx, b)
    return xt
```

```python
import functools
import math

import jax
import jax.numpy as jnp
from jax import lax
from jax.experimental import pallas as pl
from jax.experimental.pallas import tpu as pltpu

F32 = jnp.float32
BF16 = jnp.bfloat16
I32 = jnp.int32

DEPTH = 2
GRID_W = 64
HEAD_DIM = 64
KV_HEADS = 2
Q_PER_KV = 4
AXIS_DIM = HEAD_DIM // 2
ROPE_THETA = 10000.0
SSD_HEADS = 4
SSD_HEAD_DIM = 64
SSD_STATE = 64
RG_BLOCKS = 4
RG_C = 8.0
CONV_W = 4
N_EXPERTS = 16
EC_CAPACITY = 2
EPS = 1e-6
DEEPNORM_ALPHA = (2 * DEPTH) ** 0.25

LANES = 128
SUBLANES = 8
BF16_ROWS = 16
SEQ_TILE = 256
ATTN_Q_TILE = 512
ATTN_KEY_CHUNK = 256
MERGE_ROW_SPLIT = 2
LN2_TILES = 4
SSD_BATCH = 4
MOE_FF_TILE = 1024
MOE_FF_SLICE = 256
MOE_ROW_GROUP = 4
LOG2E = math.log2(math.e)
VMEM_LIMIT = 48 * 1024 * 1024
VMEM_LIMIT_MOE = 60 * 1024 * 1024


def _cparams(limit=VMEM_LIMIT):
    return pltpu.CompilerParams(vmem_limit_bytes=limit)


def _sigmoid(x):
    return 1.0 / (1.0 + jnp.exp(-x))


def _silu(x):
    return x * _sigmoid(x)


def _softplus(x):
    return jnp.maximum(x, 0.0) + jnp.log1p(jnp.exp(-jnp.abs(x)))


def _bdot(a, b):
    return jnp.dot(a.astype(BF16), b.astype(BF16), preferred_element_type=F32)


def _mod_body(c_ref, w_ref, b_ref, o_ref):
    c = c_ref[...]
    o_ref[0] = _bdot(_silu(c), w_ref[0]) + b_ref[0]


def _modulation(c, c_ctx, w_mod, b_mod):
    depth, d, n6 = w_mod.shape
    b = c.shape[0]
    rows = 2 * SUBLANES
    cc = jnp.zeros((rows, d), F32).at[:b].set(c).at[b].set(c_ctx)
    tn = 1536
    return pl.pallas_call(
        _mod_body,
        grid=(depth, n6 // tn),
        in_specs=[
            pl.BlockSpec((rows, d), lambda l, j: (0, 0)),
            pl.BlockSpec((1, d, tn), lambda l, j: (l, 0, j)),
            pl.BlockSpec((1, 1, tn), lambda l, j: (l, 0, j)),
        ],
        out_specs=pl.BlockSpec((1, rows, tn), lambda l, j: (l, 0, j)),
        out_shape=jax.ShapeDtypeStruct((depth, rows, n6), F32),
        compiler_params=_cparams(),
        name="adaln_mod",
    )(cc, w_mod, b_mod.reshape(depth, 1, n6))


def _mod_spec(j, d, ctx_row, off):
    if off == 0:
        return pl.BlockSpec((1, 1, d), lambda b, i: (jnp.where(i == 0, ctx_row, b), 0, j))
    return pl.BlockSpec((1, 1, d), lambda b, i: (b, 0, j))


W_Q, W_K, W_V, W_XU, W_Z, W_G, W_DT = 512, 128, 128, 768, 256, 256, 256
IN_PAD = W_Q + W_K + W_V + W_XU + W_Z + W_G + W_DT


def _rope(t, cos, sin_signed):
    rows = t.shape[0]
    lane = lax.broadcasted_iota(I32, (rows, LANES), 1)
    first = (lane % AXIS_DIM) < (AXIS_DIM // 2)
    outs = []
    for c in range(t.shape[1] // LANES):
        tc = t[:, c * LANES:(c + 1) * LANES]
        partner = jnp.where(first, pltpu.roll(tc, LANES - AXIS_DIM // 2, 1), pltpu.roll(tc, AXIS_DIM // 2, 1))
        outs.append(tc * cos + partner * sin_signed)
    return outs[0] if len(outs) == 1 else jnp.concatenate(outs, axis=1)


def _inproj_body(x_ref, xp_ref, xn_ref, sc_ref, sh_ref, w_ref, cos_ref, sin_ref, qw_ref, kw_ref, oq_ref, ok_ref,
                 cw_ref, cb_ref, q_ref, k_ref, v_ref, xbc_ref, u_ref, z_ref, g_ref, dt_ref, qkv_scr):
    i = pl.program_id(1)
    n = pl.num_programs(1)
    scale = 1.0 + sc_ref[0]
    shift = sh_ref[0]
    h = (x_ref[0] * scale + shift).astype(BF16)
    n_qkv = W_Q + W_K + W_V
    qkv_scr[...] = jnp.dot(h, w_ref[:, :n_qkv], preferred_element_type=F32)
    rest = jnp.dot(h, w_ref[:, n_qkv:], preferred_element_type=F32)
    halo = (jnp.concatenate([xp_ref[0], xn_ref[0]], axis=0) * scale + shift).astype(BF16)
    xu_halo = jnp.dot(halo, w_ref[:, n_qkv:n_qkv + W_XU], preferred_element_type=F32)
    xu = rest[:, :W_XU]
    z_ref[0] = rest[:, W_XU:W_XU + W_Z]
    g_ref[0] = rest[:, W_XU + W_Z:W_XU + W_Z + W_G]
    dt_ref[0] = rest[:, W_XU + W_Z + W_G:]

    tl = xu.shape[0]
    has_prev = i > 1
    has_next = (i > 0) & (i < n - 1)
    pm = jnp.where(has_prev, xu_halo[SUBLANES - 1:SUBLANES, :], 0.0)
    n0 = jnp.where(has_next, xu_halo[SUBLANES:SUBLANES + 1, :], 0.0)
    n1 = jnp.where(has_next, xu_halo[SUBLANES + 1:SUBLANES + 2, :], 0.0)
    row = lax.broadcasted_iota(I32, xu.shape, 0)
    xm1 = jnp.where(row == 0, pm, pltpu.roll(xu, 1, 0))
    xp1 = jnp.where(row == tl - 1, n0, pltpu.roll(xu, tl - 1, 0))
    xp2 = jnp.where(row == tl - 1, n1, jnp.where(row == tl - 2, n0, pltpu.roll(xu, tl - 2, 0)))
    cw = cw_ref[...]
    yc = xm1 * cw[0:1] + xu * cw[1:2] + xp1 * cw[2:3] + xp2 * cw[3:4] + cb_ref[...]
    n_xbc = xbc_ref.shape[-1]
    xbc_ref[0] = _silu(yc[:, :n_xbc])
    u_ref[0] = yc[:, n_xbc:]

    q = qkv_scr[:, :W_Q]
    k = qkv_scr[:, W_Q:W_Q + W_K]
    cos = cos_ref[...]
    sin = sin_ref[...]
    ssq = jnp.dot((q * q).astype(BF16), oq_ref[...], preferred_element_type=F32)
    qn = q * lax.rsqrt(ssq * (1.0 / HEAD_DIM) + EPS) * qw_ref[...]
    q_ref[0] = (_rope(qn, cos, sin) * (HEAD_DIM ** -0.5 * LOG2E)).T.astype(BF16)
    ssk = jnp.dot((k * k).astype(BF16), ok_ref[...], preferred_element_type=F32)
    kn = k * lax.rsqrt(ssk * (1.0 / HEAD_DIM) + EPS) * kw_ref[...]
    k_ref[0] = _rope(kn, cos, sin).astype(BF16)
    v_ref[0] = qkv_scr[:, W_Q + W_K:].astype(BF16)


def _inproj(xt, modl, wcat, cos_t, sin_t, qw, kw, ones_q, ones_k, conv_w, conv_b, n_xbc, ctx_row):
    b, t, d = xt.shape
    tl = SEQ_TILE
    grid = (b, t // tl)
    r = tl // SUBLANES
    nblk = t // SUBLANES
    full = lambda shape: pl.BlockSpec(shape, lambda bb, i: (0,) * len(shape))
    seq = lambda w: pl.BlockSpec((1, tl, w), lambda bb, i: (bb, i, 0))
    outs = [(W_K, BF16), (W_V, BF16), (n_xbc, F32), (W_XU - n_xbc, F32), (W_Z, F32), (W_G, F32), (W_DT, F32)]
    return pl.pallas_call(
        _inproj_body,
        grid=grid,
        in_specs=[
            seq(d),
            pl.BlockSpec((1, SUBLANES, d), lambda bb, i: (bb, jnp.maximum(i * r - 1, 0), 0)),
            pl.BlockSpec((1, SUBLANES, d), lambda bb, i: (bb, jnp.minimum((i + 1) * r, nblk - 1), 0)),
            _mod_spec(1, d, ctx_row, 0),
            _mod_spec(0, d, ctx_row, 0),
            full((d, IN_PAD)),
            pl.BlockSpec((tl, LANES), lambda bb, i: (i, 0)),
            pl.BlockSpec((tl, LANES), lambda bb, i: (i, 0)),
            full((1, W_Q)), full((1, W_K)), full((W_Q, W_Q)), full((W_K, W_K)),
            full((CONV_W, W_XU)), full((1, W_XU)),
        ],
        out_specs=[pl.BlockSpec((1, W_Q, tl), lambda bb, i: (bb, 0, i))] + [seq(w) for w, _ in outs],
        out_shape=[jax.ShapeDtypeStruct((b, W_Q, t), BF16)]
        + [jax.ShapeDtypeStruct((b, t, w), dt) for w, dt in outs],
        scratch_shapes=[pltpu.VMEM((tl, W_Q + W_K + W_V), F32)],
        compiler_params=_cparams(),
        name="in_proj",
    )(xt, xt, xt, modl, modl, wcat, cos_t, sin_t, qw, kw, ones_q, ones_k, conv_w, conv_b)


def _attn_body(qt_ref, k_ref, vt_ref, o_ref, s_buf, p_buf, acc_ref, *, n_keys, kc):
    nh, hd, tq = qt_ref.shape[1:]
    qt = jnp.concatenate([qt_ref[0, h] for h in range(nh)], axis=1)
    cols = nh * tq
    rem = n_keys % kc
    chunks = ([(0, rem)] if rem else []) + [(rem + kc * i, kc) for i in range(n_keys // kc)]
    n = len(chunks)

    def scores(c):
        s0, sz = chunks[c]
        s_buf[c % 2, :sz, :] = jnp.dot(k_ref[0, 0, s0:s0 + sz, :], qt, preferred_element_type=F32)

    def softmax(c, m):
        sz = chunks[c][1]
        s = s_buf[c % 2, :sz, :]
        m_new = jnp.maximum(m, jnp.max(s, axis=0, keepdims=True))
        p_buf[c % 2, :sz, :] = jnp.exp2(s - m_new).astype(BF16)
        return m_new, jnp.exp2(m - m_new)

    def weighted_values(c, alpha):
        s0, sz = chunks[c]
        acc_ref[...] = alpha * acc_ref[...] + jnp.dot(vt_ref[0, 0, :, s0:s0 + sz], p_buf[c % 2, :sz, :],
                                                      preferred_element_type=F32)

    acc_ref[...] = jnp.zeros_like(acc_ref)
    m = jnp.full((1, cols), -jnp.inf, F32)
    alpha = None
    scores(0)
    for j in range(n + 1):
        if j + 1 < n:
            scores(j + 1)
        prev_alpha = alpha
        if j < n:
            m, alpha = softmax(j, m)
        if j >= 1:
            weighted_values(j - 1, prev_alpha)
    o = acc_ref[:hd, :] * (1.0 / acc_ref[hd:hd + 1, :])
    for h in range(nh):
        o_ref[0, h * hd:(h + 1) * hd, :] = o[:, h * tq:(h + 1) * tq]


def _attention(qt, k, vt, n_q, q_off, n_keys, tq):
    b = qt.shape[0]
    assert q_off % tq == 0
    off = q_off // tq
    kc = min(ATTN_KEY_CHUNK, n_keys)
    cols = Q_PER_KV * tq
    vrows = vt.shape[2]
    return pl.pallas_call(
        functools.partial(_attn_body, n_keys=n_keys, kc=kc),
        grid=(b, KV_HEADS, n_q // tq),
        scratch_shapes=[pltpu.VMEM((2, kc, cols), F32), pltpu.VMEM((2, kc, cols), BF16),
                        pltpu.VMEM((vrows, cols), F32)],
        in_specs=[
            pl.BlockSpec((1, Q_PER_KV, HEAD_DIM, tq), lambda bb, g, i: (bb, g, 0, i + off)),
            pl.BlockSpec((1, 1, n_keys, HEAD_DIM), lambda bb, g, i: (bb, g, 0, 0)),
            pl.BlockSpec((1, 1, vrows, n_keys), lambda bb, g, i: (bb, g, 0, 0)),
        ],
        out_specs=pl.BlockSpec((1, Q_PER_KV * HEAD_DIM, tq), lambda bb, g, i: (bb, g, i)),
        out_shape=jax.ShapeDtypeStruct((b, KV_HEADS * Q_PER_KV * HEAD_DIM, n_q), F32),
        compiler_params=_cparams(),
        name="gqa_attention",
    )(qt, k, vt)


def _seq_order(d, c, nc):
    return jnp.where(c == 0, 0, jnp.where(d == 0, c, nc - c))


def _ssd_body(xf_ref, xb_ref, dtf_ref, dtb_ref, bias_ref, a_ref, dvec_ref, yf_ref, yb_ref, s_scr):
    @pl.when(pl.program_id(1) == 0)
    def _():
        s_scr[...] = jnp.zeros_like(s_scr)

    nb = xf_ref.shape[0]
    streams = range(2 * nb)
    xbc = [(xf_ref, xb_ref)[k % 2][k // 2] for k in streams]
    q = xbc[0].shape[0]
    xw = SSD_HEADS * SSD_HEAD_DIM
    gw = 2 * SSD_STATE
    x = [v[:, :xw] for v in xbc]
    bm = [v[:, xw:xw + gw] for v in xbc]
    cm = [v[:, xw + gw:xw + 2 * gw] for v in xbc]
    dt_raw = [(dtf_ref, dtb_ref)[k % 2][k // 2] for k in streams]
    dt = [_softplus(dt_raw[k] + bias_ref[k % 2]) for k in streams]
    da = [dt[k] * a_ref[k % 2] for k in streams]
    ii = lax.broadcasted_iota(I32, (q, q), 0)
    jj = lax.broadcasted_iota(I32, (q, q), 1)
    mask = [jj <= ii, jj >= ii]
    tm = [jnp.where(m_, 1.0, 0.0).astype(BF16) for m_ in mask]
    da_hi = [da[k].astype(BF16) for k in streams]
    da_lo = [(da[k] - da_hi[k].astype(F32)).astype(BF16) for k in streams]
    cs = [jnp.dot(tm[k % 2], da_hi[k], preferred_element_type=F32)
          + jnp.dot(tm[k % 2], da_lo[k], preferred_element_type=F32) for k in streams]
    tot = [cs[k][q - 1:q, :] if k % 2 == 0 else cs[k][0:1, :] for k in streams]
    cst = [cs[k].T for k in streams]
    dec = [jnp.exp(tot[k] - cs[k]) for k in streams]
    ecs = [jnp.exp(cs[k]) for k in streams]
    etot = [jnp.exp(tot[k]) for k in streams]
    dt_t = [dt[k].T for k in streams]
    w_t = [(dt[k] * dec[k]).T for k in streams]
    x_t = [x[k].T for k in streams]
    ys = [[] for _ in streams]
    nt = (((1,), (1,)), ((), ()))
    tn = (((0,), (0,)), ((), ()))
    for g in range(2):
        gs = slice(g * SSD_STATE, (g + 1) * SSD_STATE)
        bg = [bm[k][:, gs].astype(BF16) for k in streams]
        cg = [cm[k][:, gs].astype(BF16) for k in streams]
        gmat = [lax.dot_general(cg[k], bg[k], nt, preferred_element_type=F32) for k in streams]
        for hh in range(SSD_HEADS // 2):
            h = 2 * g + hh
            hs = slice(h * SSD_HEAD_DIM, (h + 1) * SSD_HEAD_DIM)
            lmat = [jnp.exp(jnp.where(mask[k % 2], cs[k][:, h:h + 1] - cst[k][h:h + 1, :], -jnp.inf))
                    for k in streams]
            y_diag = [_bdot(gmat[k] * lmat[k] * dt_t[k][h:h + 1, :], x[k][:, hs]) for k in streams]
            s_in = [s_scr[k, h] for k in streams]
            y_off = [lax.dot_general(cg[k], s_in[k].astype(BF16), nt, preferred_element_type=F32) * ecs[k][:, h:h + 1]
                     for k in streams]
            xd_t = [(x_t[k][hs, :] * w_t[k][h:h + 1, :]).astype(BF16) for k in streams]
            for k in streams:
                ys[k].append(y_diag[k] + y_off[k])
                s_scr[k, h] = etot[k][:, h:h + 1] * s_in[k] + jnp.dot(xd_t[k], bg[k], preferred_element_type=F32)
    for k in streams:
        if k % 2 == 0:
            yf_ref[k // 2] = jnp.concatenate(ys[k], axis=1) + x[k] * dvec_ref[...]
        else:
            yb_ref[k // 2] = jnp.concatenate(ys[k], axis=1)


def _ssd(xbc, dt, bias, a_neg, dvec):
    b, t, cw = xbc.shape
    q = SEQ_TILE
    nc = t // q
    xw = SSD_HEADS * SSD_HEAD_DIM
    fwd = lambda c: c
    bwd = lambda c: _seq_order(1, c, nc)
    nb = SSD_BATCH
    return pl.pallas_call(
        _ssd_body,
        grid=(b // nb, nc),
        in_specs=[
            pl.BlockSpec((nb, q, cw), lambda bb, c: (bb, fwd(c), 0)),
            pl.BlockSpec((nb, q, cw), lambda bb, c: (bb, bwd(c), 0)),
            pl.BlockSpec((nb, q, LANES), lambda bb, c: (bb, fwd(c), 0)),
            pl.BlockSpec((nb, q, LANES), lambda bb, c: (bb, bwd(c), 1)),
            pl.BlockSpec((2, 1, LANES), lambda bb, c: (0, 0, 0)),
            pl.BlockSpec((2, 1, LANES), lambda bb, c: (0, 0, 0)),
            pl.BlockSpec((1, xw), lambda bb, c: (0, 0)),
        ],
        out_specs=[pl.BlockSpec((nb, q, xw), lambda bb, c: (bb, fwd(c), 0)),
                   pl.BlockSpec((nb, q, xw), lambda bb, c: (bb, bwd(c), 0))],
        out_shape=[jax.ShapeDtypeStruct((b, t, xw), F32)] * 2,
        scratch_shapes=[pltpu.VMEM((2 * nb, SSD_HEADS, SSD_HEAD_DIM, SSD_STATE), F32)],
        compiler_params=_cparams(),
        name="ssd_scan",
    )(xbc, xbc, dt, dt, bias, a_neg, dvec)


def _rg_body(u_ref, w_ref, bias_ref, lam_ref, y_ref, a_s, v_s, o_s, h_s, *, pitch):
    d = pl.program_id(0)
    c = pl.program_id(1)
    nb, tl, width = u_ref.shape
    ng = width // LANES

    @pl.when(c == 0)
    def _():
        h_s[...] = jnp.zeros_like(h_s)

    sp = _softplus(-lam_ref[0])
    w = w_ref[0]
    bias = bias_ref[0]
    for b in range(nb):
        ub = u_ref[b]
        pre = jnp.dot(ub.astype(BF16), w, preferred_element_type=F32) + bias
        r = _sigmoid(pre[:, :width])
        ig = _sigmoid(pre[:, width:])
        a = jnp.exp((-RG_C) * r * sp)
        v = jnp.sqrt(1.0 - a * a) * ig * ub
        for j in range(ng):
            a_s[j, pl.ds(b * pitch, tl), :] = a[:, j * LANES:(j + 1) * LANES]
            v_s[j, pl.ds(b * pitch, tl), :] = v[:, j * LANES:(j + 1) * LANES]

    def step(t, hs):
        te = jnp.where(d == 0, t, tl - 1 - t)
        out = []
        for j in range(ng):
            at = a_s[j, pl.ds(te, nb, stride=pitch), :]
            vt = v_s[j, pl.ds(te, nb, stride=pitch), :]
            hj = at * hs[j] + vt
            o_s[j, pl.ds(te, nb, stride=pitch), :] = hj
            out.append(hj)
        return tuple(out)

    h0 = tuple(h_s[:, j * LANES:(j + 1) * LANES] for j in range(ng))
    hf = lax.fori_loop(0, tl, step, h0, unroll=8)
    for j in range(ng):
        h_s[:, j * LANES:(j + 1) * LANES] = hf[j]
    for b in range(nb):
        y_ref[0, b] = jnp.concatenate([o_s[j, pl.ds(b * pitch, tl), :] for j in range(ng)], axis=1)


def _rglru(u, wg, bias, lam):
    b, t, width = u.shape
    assert b == SUBLANES, "the recurrence keeps one sample per sublane"
    tl = SEQ_TILE
    nc = t // tl
    pitch = tl + SUBLANES
    ng = width // LANES
    slab = pltpu.VMEM((ng, b * pitch, LANES), F32)
    return pl.pallas_call(
        functools.partial(_rg_body, pitch=pitch),
        grid=(2, nc),
        in_specs=[
            pl.BlockSpec((b, tl, width), lambda d, c: (0, _seq_order(d, c, nc), 0)),
            pl.BlockSpec((1, width, 2 * width), lambda d, c: (d, 0, 0)),
            pl.BlockSpec((1, 1, 2 * width), lambda d, c: (d, 0, 0)),
            pl.BlockSpec((1, 1, width), lambda d, c: (d, 0, 0)),
        ],
        out_specs=pl.BlockSpec((1, b, tl, width), lambda d, c: (d, 0, _seq_order(d, c, nc), 0)),
        out_shape=jax.ShapeDtypeStruct((2, b, t, width), F32),
        scratch_shapes=[slab, slab, slab, pltpu.VMEM((b, width), F32)],
        compiler_params=_cparams(),
        name="rglru_scan",
    )(u, wg, bias, lam)


def _rms(x, w):
    return x * lax.rsqrt(jnp.mean(x * x, axis=-1, keepdims=True) + EPS) * w


def _layer_norm(t, w, b):
    mu = jnp.mean(t, axis=-1, keepdims=True)
    tc = t - mu
    var = jnp.mean(tc * tc, axis=-1, keepdims=True)
    return tc * lax.rsqrt(var + EPS) * w + b


def _gelu_tanh(x):
    return 0.5 * x * (1.0 + jnp.tanh(math.sqrt(2.0 / math.pi) * (x + 0.044715 * (x * x * x))))


def _pack_bf16_pair(lo, hi):
    lb = pltpu.bitcast(lo.astype(BF16).astype(F32), jnp.uint32)
    hb = pltpu.bitcast(hi.astype(BF16).astype(F32), jnp.uint32)
    return (lb >> 16) | (hb & jnp.uint32(0xFFFF0000))


def _merge_body(*refs, has_ctx):
    if has_ctx:
        actx_ref, refs = refs[0], refs[1:]
    (alat_ref, ysf_ref, ysb_ref, z_ref, hr_ref, g_ref, x_ref, g1_ref, sc2_ref, sh2_ref, aw_ref, sw_ref, rw_ref,
     wo_ref, lnw_ref, lnb_ref, wr_ref, x1_ref, hp_ref, aff_ref, afft_ref, cat_scr, proj_scr) = refs
    tl = x_ref.shape[1]
    nh = MERGE_ROW_SPLIT
    rows = [slice(j * tl // nh, (j + 1) * tl // nh) for j in range(nh)]
    for rs in rows:
        a = alat_ref[0, :, rs]
        if has_ctx:
            a = jnp.where(pl.program_id(1) == 0, actx_ref[0, :, rs], a)
        an = _rms(a.T, aw_ref[...])
        sn = _rms((ysf_ref[0, rs] + ysb_ref[0, rs]) * _silu(z_ref[0, rs]), sw_ref[...])
        rn = _rms((hr_ref[0, 0, rs] + hr_ref[1, 0, rs]) * _gelu_tanh(g_ref[0, rs]), rw_ref[...])
        cat_scr[rs] = jnp.concatenate([an, sn, rn], axis=1).astype(BF16)
    for rs in rows:
        proj_scr[rs] = jnp.dot(cat_scr[rs], wo_ref[...], preferred_element_type=F32)
    for rs in rows:
        x1 = _layer_norm(DEEPNORM_ALPHA * x_ref[0, rs] + g1_ref[0] * proj_scr[rs], lnw_ref[...], lnb_ref[...])
        x1_ref[0, rs] = x1
        h2 = x1 * (1.0 + sc2_ref[0]) + sh2_ref[0]
        half = h2.shape[1] // 2
        hp_ref[0, rs] = _pack_bf16_pair(h2[:, :half], h2[:, half:])
        logits = jnp.dot(h2.astype(BF16), wr_ref[...], preferred_element_type=F32)
        lane = lax.broadcasted_iota(I32, logits.shape, 1)
        logits = jnp.where(lane < N_EXPERTS, logits, -jnp.inf)
        e = jnp.exp(logits - jnp.max(logits, axis=-1, keepdims=True))
        aff = e / jnp.sum(e, axis=-1, keepdims=True)
        aff_ref[0, rs] = aff
        afft_ref[0, :, rs] = aff.T[:N_EXPERTS, :]


def _merge(attn_ctx, attn_lat, ys, z, hr, g, xt, modl, aw, sw, rw, wo, lnw, lnb, wr, ctx_row, off):
    b, t, d = xt.shape
    tl = SEQ_TILE
    nt = t // tl - off
    has_ctx = off == 0
    aw_ = attn_lat.shape[1]
    seq = lambda w: pl.BlockSpec((1, tl, w), lambda bb, i: (bb, i + off, 0))
    pair = lambda w: pl.BlockSpec((2, 1, tl, w), lambda bb, i: (0, bb, i + off, 0))
    full = lambda shape: pl.BlockSpec(shape, lambda bb, i: (0,) * len(shape))
    out = lambda w: pl.BlockSpec((1, tl, w), lambda bb, i: (bb, i, 0))
    lat_off = 1 - off
    in_specs = [
        pl.BlockSpec((1, aw_, tl), lambda bb, i: (bb, 0, jnp.maximum(i - lat_off, 0))),
        seq(ys[0].shape[-1]), seq(ys[1].shape[-1]), seq(z.shape[-1]), pair(hr.shape[-1]), seq(g.shape[-1]), seq(d),
        _mod_spec(2, d, ctx_row, off), _mod_spec(4, d, ctx_row, off), _mod_spec(3, d, ctx_row, off),
        full(aw.shape), full(sw.shape), full(rw.shape), full(wo.shape), full(lnw.shape), full(lnb.shape),
        full(wr.shape),
    ]
    args = [attn_lat, ys[0], ys[1], z, hr, g, xt, modl, modl, modl, aw, sw, rw, wo, lnw, lnb, wr]
    if has_ctx:
        in_specs = [pl.BlockSpec((1, aw_, tl), lambda bb, i: (bb, 0, 0))] + in_specs
        args = [attn_ctx] + args
    rows = nt * tl
    return pl.pallas_call(
        functools.partial(_merge_body, has_ctx=has_ctx),
        grid=(b, nt),
        in_specs=in_specs,
        out_specs=[out(d), out(d // 2), out(LANES), pl.BlockSpec((1, N_EXPERTS, tl), lambda bb, i: (bb, 0, i))],
        out_shape=[
            jax.ShapeDtypeStruct((b, rows, d), F32),
            jax.ShapeDtypeStruct((b, rows, d // 2), jnp.uint32),
            jax.ShapeDtypeStruct((b, rows, LANES), F32),
            jax.ShapeDtypeStruct((b, N_EXPERTS, rows), F32),
        ],
        scratch_shapes=[pltpu.VMEM((tl, wo.shape[0]), BF16), pltpu.VMEM((tl, d), F32)],
        compiler_params=_cparams(),
        name="merge_outproj_ln1_router",
    )(*args)


def _topk_body(aff_ref, idx_ref, cum_scr, *, cap):
    aff = aff_ref[0]
    ne, n = aff.shape
    bits = pltpu.bitcast(aff, I32)

    def search(i, thr):
        cand = thr | lax.shift_left(jnp.int32(1), 30 - i)
        cnt = jnp.sum(jnp.where(bits >= cand, 1.0, 0.0), axis=1, keepdims=True)
        return jnp.where(cnt >= cap, cand, thr)

    thr = lax.fori_loop(0, 31, search, jnp.zeros((ne, 1), I32))
    gt = jnp.where(bits > thr, 1.0, 0.0)
    eq = jnp.where(bits == thr, 1.0, 0.0)
    need = cap - jnp.sum(gt, axis=1, keepdims=True)
    nblk = n // LANES
    r_ = lax.broadcasted_iota(I32, (LANES, LANES), 0)
    c_ = lax.broadcasted_iota(I32, (LANES, LANES), 1)
    upper = jnp.where(r_ <= c_, 1.0, 0.0).astype(BF16)

    def prefix(blocks):
        outs = []
        off = jnp.zeros((ne, 1), F32)
        for mk in blocks:
            w = jnp.dot(mk.astype(BF16), upper, preferred_element_type=F32) + off
            outs.append(w)
            off = w[:, LANES - 1:LANES]
        return outs

    blk = lambda a, k: a[:, k * LANES:(k + 1) * LANES]
    tie_rank = prefix([blk(eq, k) for k in range(nblk)])
    cum = prefix([jnp.maximum(blk(gt, k), blk(eq, k) * jnp.where(tie_rank[k] <= need, 1.0, 0.0))
                  for k in range(nblk)])
    kpad = cum_scr.shape[0] // ne
    for k in range(kpad):
        cum_scr[k * ne:(k + 1) * ne, :] = cum[k] if k < nblk else jnp.zeros((ne, LANES), F32)
    width = -(-cap // LANES) * LANES
    slot = lax.broadcasted_iota(I32, (LANES, width), 1).astype(F32)
    blk_id = lax.broadcasted_iota(I32, (LANES, width), 0).astype(F32)
    blk_col = lax.broadcasted_iota(I32, (LANES, 1), 0)
    tn = (((0,), (0,)), ((), ()))

    def per_expert(e, carry):
        cum_e = cum_scr[pl.ds(e, kpad, stride=ne), :]
        if kpad < LANES:
            cum_e = jnp.concatenate([cum_e, jnp.zeros((LANES - kpad, LANES), F32)], axis=0)
        ends = jnp.where(blk_col < nblk, cum_e[:, LANES - 1:LANES], float(2 * n))
        full = jnp.sum(jnp.where(ends <= slot, 1.0, 0.0), axis=0, keepdims=True)
        pick = jnp.where(blk_id == full, 1.0, 0.0).astype(BF16)
        hi = jnp.floor(cum_e * (1.0 / 32.0))
        lo = cum_e - 32.0 * hi
        straddle = (32.0 * lax.dot_general(hi.astype(BF16), pick, tn, preferred_element_type=F32)
                    + lax.dot_general(lo.astype(BF16), pick, tn, preferred_element_type=F32))
        inside = jnp.sum(jnp.where(straddle <= slot, 1.0, 0.0), axis=0, keepdims=True)
        idx_ref[0, pl.ds(e, 1), :] = (float(LANES) * full + inside)[:, :cap].astype(I32)
        return carry

    lax.fori_loop(0, ne, per_expert, 0, unroll=4)


def _topk(aff_t, cap):
    b, e, n = aff_t.shape
    return pl.pallas_call(
        functools.partial(_topk_body, cap=cap),
        grid=(b,),
        in_specs=[pl.BlockSpec((1, e, n), lambda bb: (bb, 0, 0))],
        out_specs=pl.BlockSpec((1, e, cap), lambda bb: (bb, 0, 0)),
        out_shape=jax.ShapeDtypeStruct((b, e, cap), I32),
        scratch_shapes=[pltpu.VMEM((max(n // LANES, SUBLANES) * e, LANES), F32)],
        compiler_params=_cparams(),
        name="expert_choice_topk",
    )(aff_t)


def _moe_body(idxp_ref, idxc_ref, idxn_ref, hp_ref, aff_ref, wg_ref, wu_ref, wd_ref, out_ref,
              xs_scr, ag_scr, y_scr, xb_scr, hid_scr, *, cap):
    e = pl.program_id(1)
    f = pl.program_id(2)
    ne = pl.num_programs(1)
    nf = pl.num_programs(2)
    cur = e % 2
    oth = 1 - cur

    @pl.when((pl.program_id(0) == 0) & (e == 0) & (f == 0))
    def _():
        y_scr[...] = jnp.zeros_like(y_scr)

    @pl.when((e == 0) & (f == 0))
    def _():
        out_ref[...] = jnp.zeros_like(out_ref)

        def gather(j, carry):
            for u in range(SUBLANES):
                t = idxc_ref[0, 0, 0, j * SUBLANES + u]
                xs_scr[0, j, u:u + 1, :] = hp_ref[0, pl.ds(t, 1), :]
                ag_scr[0, j, u:u + 1, :] = aff_ref[0, pl.ds(t, 1), :]
            return carry

        lax.fori_loop(0, cap // SUBLANES, gather, 0)

    w = xs_scr[cur].reshape(cap, xs_scr.shape[-1])
    half = w.shape[1]
    xb_scr[:, :half] = pltpu.bitcast(w << 16, F32).astype(BF16)
    xb_scr[:, half:] = pltpu.bitcast(w & jnp.uint32(0xFFFF0000), F32).astype(BF16)
    lane = lax.broadcasted_iota(I32, (cap, LANES), 1)
    gate = jnp.sum(jnp.where(lane == e, ag_scr[cur].reshape(cap, LANES), 0.0), axis=1, keepdims=True)
    share = cap // nf
    base = f * share
    base_tile = f * (share // SUBLANES)
    has_prev = e > 0

    def row_copies(r0, r1):
        for g0 in range(r0, r1, MOE_ROW_GROUP):
            rows = range(g0, min(g0 + MOE_ROW_GROUP, r1))
            tps = [idxp_ref[0, 0, 0, base + r] for r in rows]
            sums = [out_ref[0, pl.ds(tp, 1), :]
                    + jnp.where(has_prev, y_scr[oth, base_tile + r // SUBLANES, r % SUBLANES:r % SUBLANES + 1, :], 0.0)
                    for tp, r in zip(tps, rows)]
            for tp, v in zip(tps, sums):
                out_ref[0, pl.ds(tp, 1), :] = v
            for r in rows:
                tn = idxn_ref[0, 0, 0, base + r]
                j, u = base_tile + r // SUBLANES, r % SUBLANES
                xs_scr[oth, j, u:u + 1, :] = hp_ref[0, pl.ds(tn, 1), :]
                ag_scr[oth, j, u:u + 1, :] = aff_ref[0, pl.ds(tn, 1), :]

    tf = wg_ref.shape[3]
    nsl = tf // MOE_FF_SLICE
    sl = lambda c: slice(c * MOE_FF_SLICE, (c + 1) * MOE_FF_SLICE)

    def hidden(c):
        xb = xb_scr[...]
        hg = jnp.dot(xb, wg_ref[0, 0, :, sl(c)], preferred_element_type=F32)
        hu = jnp.dot(xb, wu_ref[0, 0, :, sl(c)], preferred_element_type=F32)
        hid_scr[:, sl(c)] = (_silu(hg) * hu).astype(BF16)

    for c in range(nsl):
        hidden(c)
        row_copies(share * c // nsl, share * (c + 1) // nsl)
    yp = jnp.dot(hid_scr[...], wd_ref[0, 0], preferred_element_type=F32)
    y_old = y_scr[cur].reshape(cap, y_scr.shape[-1])
    y_new = (jnp.where(f == 0, 0.0, y_old) + yp) * jnp.where(f == nf - 1, gate, 1.0)
    y_scr[cur] = y_new.reshape(y_scr.shape[1:])

    @pl.when((e == ne - 1) & (f == nf - 1))
    def _():
        def scatter(j, carry):
            for u in range(SUBLANES):
                t = idxc_ref[0, 0, 0, j * SUBLANES + u]
                out_ref[0, pl.ds(t, 1), :] = out_ref[0, pl.ds(t, 1), :] + y_scr[cur, j, u:u + 1, :]
            return carry

        lax.fori_loop(0, cap // SUBLANES, scatter, 0)


def _moe(idx, hp, aff, wg, wu, wd, layer, tf):
    b, t, half = hp.shape
    d = 2 * half
    ne, ff = wg.shape[1], wg.shape[3]
    cap = idx.shape[-1]
    nf = ff // tf
    assert nf >= 2 and cap % nf == 0
    one = pl.Buffered(1)
    idx4 = idx.reshape(b, ne, 1, cap)
    smem = lambda shift: pl.BlockSpec((1, 1, 1, cap), lambda bb, e, f: (bb, jnp.clip(e + shift, 0, ne - 1), 0, 0),
                                      memory_space=pltpu.SMEM)
    return pl.pallas_call(
        functools.partial(_moe_body, cap=cap),
        grid=(b, ne, nf),
        in_specs=[
            smem(-1), smem(0), smem(1),
            pl.BlockSpec((1, t, half), lambda bb, e, f: (bb, 0, 0), pipeline_mode=one),
            pl.BlockSpec((1, t, LANES), lambda bb, e, f: (bb, 0, 0), pipeline_mode=one),
            pl.BlockSpec((1, 1, d, tf), lambda bb, e, f: (layer, e, 0, f)),
            pl.BlockSpec((1, 1, d, tf), lambda bb, e, f: (layer, e, 0, f)),
            pl.BlockSpec((1, 1, tf, d), lambda bb, e, f: (layer, e, f, 0)),
        ],
        out_specs=pl.BlockSpec((1, t, d), lambda bb, e, f: (bb, 0, 0), pipeline_mode=one),
        out_shape=jax.ShapeDtypeStruct((b, t, d), F32),
        scratch_shapes=[
            pltpu.VMEM((2, cap // SUBLANES, SUBLANES, half), jnp.uint32),
            pltpu.VMEM((2, cap // SUBLANES, SUBLANES, LANES), F32),
            pltpu.VMEM((2, cap // SUBLANES, SUBLANES, d), F32),
            pltpu.VMEM((cap, d), BF16),
            pltpu.VMEM((cap, tf), BF16),
        ],
        compiler_params=_cparams(VMEM_LIMIT_MOE),
        name="expert_ffn",
    )(idx4, idx4, idx4, hp, aff, wg, wu, wd)


def _ln2_body(x1_ref, moe_ref, g2_ref, g2c_ref, w_ref, b_ref, o_ref, *, n_ctx):
    g2 = g2_ref[0]
    if n_ctx:
        tl = x1_ref.shape[1]
        row = lax.broadcasted_iota(I32, (tl, 1), 0) + pl.program_id(1) * tl
        g2 = jnp.where(row < n_ctx, g2c_ref[0], g2)
    o_ref[0] = _layer_norm(DEEPNORM_ALPHA * x1_ref[0] + g2 * moe_ref[0], w_ref[...], b_ref[...])


def _ln2(x1, moe, modl, w, bias, ctx_row, n_ctx):
    b, t, d = x1.shape
    tl = t // LN2_TILES
    assert t % LN2_TILES == 0 and tl % SUBLANES == 0
    seq = pl.BlockSpec((1, tl, d), lambda bb, i: (bb, i, 0))
    full = pl.BlockSpec((1, d), lambda bb, i: (0, 0))
    return pl.pallas_call(
        functools.partial(_ln2_body, n_ctx=n_ctx),
        grid=(b, LN2_TILES),
        in_specs=[seq, seq,
                  pl.BlockSpec((1, 1, d), lambda bb, i: (bb, 0, 5)),
                  pl.BlockSpec((1, 1, d), lambda bb, i: (ctx_row, 0, 5)),
                  full, full],
        out_specs=seq,
        out_shape=jax.ShapeDtypeStruct((b, t, d), F32),
        compiler_params=_cparams(),
        name="ln2",
    )(x1, moe, modl, modl, w, bias)


def _block_diag_ones(width, block):
    r = jnp.arange(width)[:, None] // block
    c = jnp.arange(width)[None, :] // block
    return (r == c).astype(BF16)


def _block_diag(w):
    k, d, e = w.shape
    eye = jnp.eye(k, dtype=w.dtype)
    return (eye[:, None, :, None] * w[:, :, None, :]).reshape(k * d, k * e)


def _rope_tables(n_ctx, n_lat):
    pos = jnp.arange(n_lat)
    row = (pos // GRID_W).astype(F32)
    col = (pos % GRID_W).astype(F32)
    inv_freq = ROPE_THETA ** (-jnp.arange(0, AXIS_DIM, 2, dtype=F32) / AXIS_DIM)
    ang_r = row[:, None] * inv_freq
    ang_c = col[:, None] * inv_freq
    cos_h = jnp.concatenate([jnp.cos(ang_r)] * 2 + [jnp.cos(ang_c)] * 2, axis=1)
    sin_h = jnp.concatenate([-jnp.sin(ang_r), jnp.sin(ang_r), -jnp.sin(ang_c), jnp.sin(ang_c)], axis=1)
    reps = LANES // HEAD_DIM
    cos_t = jnp.concatenate([jnp.ones((n_ctx, LANES), F32), jnp.tile(cos_h, (1, reps))], axis=0)
    sin_t = jnp.concatenate([jnp.zeros((n_ctx, LANES), F32), jnp.tile(sin_h, (1, reps))], axis=0)
    return cos_t, sin_t


def _layer(xt, modl, rope, p, experts, layer, last, n_ctx, ctx_row):
    b, t, d = xt.shape
    n_lat = t - n_ctx
    cos_t, sin_t = rope
    w = p["w_in"]
    pad = jnp.zeros((d, LANES - SSD_HEADS), F32)
    wcat = jnp.concatenate(
        [w[:, 0:768], w[:, 768:1280], w[:, 1544:1800], w[:, 1280:1536], w[:, 1800:2056],
         w[:, 1536:1540], pad, w[:, 1540:1544], pad], axis=1).astype(BF16)
    qw = jnp.tile(p["q_norm"], W_Q // HEAD_DIM)[None]
    kw = jnp.tile(p["k_norm"], W_K // HEAD_DIM)[None]
    conv_w = jnp.concatenate([p["ssd_conv_w"], p["rg_conv_w"]], axis=1)
    conv_b = jnp.concatenate([p["ssd_conv_b"], p["rg_conv_b"]])[None]
    qt, k, v, xbc, u, z, g, dt = _inproj(xt, modl, wcat, cos_t, sin_t, qw, kw,
                                         _block_diag_ones(W_Q, HEAD_DIM), _block_diag_ones(W_K, HEAD_DIM),
                                         conv_w, conv_b, p["ssd_conv_w"].shape[1], ctx_row)

    qt = qt.reshape(b, W_Q // HEAD_DIM, HEAD_DIM, t)
    kh = k.reshape(b, t, KV_HEADS, HEAD_DIM).transpose(0, 2, 1, 3)
    vt = v.reshape(b, t, KV_HEADS, HEAD_DIM).transpose(0, 2, 3, 1)
    ones_pad = jnp.zeros((b, KV_HEADS, BF16_ROWS, t), BF16).at[:, :, 0, :].set(1.0)
    vt = jnp.concatenate([vt, ones_pad], axis=2)
    attn_lat = _attention(qt[..., n_ctx:], kh, vt, n_lat, 0, t, ATTN_Q_TILE)
    attn_ctx = None if last else _attention(qt, kh, vt, n_ctx, 0, n_ctx, min(ATTN_Q_TILE, n_ctx))

    lane_pad = lambda a: jnp.pad(a, ((0, 0), (0, LANES - a.shape[1])))[:, None, :]
    ys = _ssd(xbc, dt, lane_pad(p["ssd_dt_bias"]), lane_pad(-jnp.exp(p["ssd_a_log"])),
              jnp.repeat(p["ssd_d"], SSD_HEAD_DIM)[None])
    wgate = jnp.stack([jnp.concatenate([_block_diag(p["rg_wa"][j]), _block_diag(p["rg_wx"][j])], axis=1)
                       for j in range(2)]).astype(BF16)
    bgate = jnp.concatenate([p["rg_ba"], p["rg_bx"]], axis=1)[:, None, :]
    hr = _rglru(u, wgate, bgate, p["rg_lambda"][:, None, :])

    off = 1 if last else 0
    wr = jnp.pad(p["w_router"], ((0, 0), (0, LANES - N_EXPERTS))).astype(BF16)
    x1, hp, aff, aff_t = _merge(attn_ctx, attn_lat, ys, z, hr, g, xt, modl,
                                p["attn_out_norm"][None], p["ssd_norm"][None], p["rg_out_norm"][None],
                                p["w_out"].astype(BF16), p["ln1_w"][None], p["ln1_b"][None], wr, ctx_row, off)

    if last:
        idx = _topk(aff_t, EC_CAPACITY * n_lat // N_EXPERTS)
    else:
        idx_lat = _topk(aff_t[:, :, n_ctx:], EC_CAPACITY * n_lat // N_EXPERTS) + n_ctx
        idx_ctx = _topk(aff_t[:, :, :n_ctx], EC_CAPACITY * n_ctx // N_EXPERTS)
        idx = jnp.concatenate([idx_lat, idx_ctx], axis=-1)
    moe = _moe(idx, hp, aff, *experts, layer, MOE_FF_TILE)
    return _ln2(x1, moe, modl, p["ln2_w"][None], p["ln2_b"][None], ctx_row, 0 if last else n_ctx)


def kernel(x, c, ctx, c_ctx, w_mod, b_mod, w_in, q_norm, k_norm, attn_out_norm, ssd_conv_w, ssd_conv_b, ssd_dt_bias, ssd_a_log, ssd_d, ssd_norm, rg_conv_w, rg_conv_b, rg_wa, rg_ba, rg_wx, rg_bx, rg_lambda, rg_out_norm, w_out, ln1_w, ln1_b, w_router, w_gate, w_up, w_down, ln2_w, ln2_b):
    b, n_lat, d = x.shape
    n_ctx = ctx.shape[1]
    assert n_ctx == SEQ_TILE and n_lat % 512 == 0 and b == SUBLANES
    params = dict(w_in=w_in, q_norm=q_norm, k_norm=k_norm, attn_out_norm=attn_out_norm, ssd_conv_w=ssd_conv_w,
                  ssd_conv_b=ssd_conv_b, ssd_dt_bias=ssd_dt_bias, ssd_a_log=ssd_a_log, ssd_d=ssd_d,
                  ssd_norm=ssd_norm, rg_conv_w=rg_conv_w, rg_conv_b=rg_conv_b, rg_wa=rg_wa, rg_ba=rg_ba,
                  rg_wx=rg_wx, rg_bx=rg_bx, rg_lambda=rg_lambda, rg_out_norm=rg_out_norm, w_out=w_out,
                  ln1_w=ln1_w, ln1_b=ln1_b, w_router=w_router, ln2_w=ln2_w, ln2_b=ln2_b)
    experts = (w_gate.astype(BF16), w_up.astype(BF16), w_down.astype(BF16))
    mod = _modulation(c, c_ctx, w_mod, b_mod)
    rope = _rope_tables(n_ctx, n_lat)
    xt = jnp.concatenate([ctx, x], axis=1)
    depth = w_mod.shape[0]
    for l in range(depth):
        p = {name: val[l] for name, val in params.items()}
        modl = mod[l].reshape(mod.shape[1], 1, mod.shape[2])
        xt = _layer(xt, modl, rope, p, experts, l, l == depth - 1, n_ctx, b)
    return xt
```

```python
import functools
import math

import jax
import jax.numpy as jnp
from jax import lax
from jax.experimental import pallas as pl
from jax.experimental.pallas import tpu as pltpu

F32 = jnp.float32
BF16 = jnp.bfloat16
I32 = jnp.int32

DEPTH = 2
GRID_W = 64
HEAD_DIM = 64
KV_HEADS = 2
Q_PER_KV = 4
AXIS_DIM = HEAD_DIM // 2
ROPE_THETA = 10000.0
SSD_HEADS = 4
SSD_HEAD_DIM = 64
SSD_STATE = 64
RG_BLOCKS = 4
RG_C = 8.0
CONV_W = 4
N_EXPERTS = 16
EC_CAPACITY = 2
EPS = 1e-6
DEEPNORM_ALPHA = (2 * DEPTH) ** 0.25

LANES = 128
SUBLANES = 8
BF16_ROWS = 16
SEQ_TILE = 256
ATTN_Q_TILE = 1024
ATTN_KEY_CHUNK = 256
MERGE_ROW_SPLIT = 2
LN2_TILES = 4
SSD_BATCH = 4
MOE_FF_TILE = 1024
MOE_FF_SLICE = 256
MOE_ROW_GROUP = 4
LOG2E = math.log2(math.e)
VMEM_LIMIT = 48 * 1024 * 1024
VMEM_LIMIT_MOE = 60 * 1024 * 1024


def _cparams(limit=VMEM_LIMIT):
    return pltpu.CompilerParams(vmem_limit_bytes=limit)


def _sigmoid(x):
    return 1.0 / (1.0 + jnp.exp(-x))


def _silu(x):
    return x * _sigmoid(x)


def _softplus(x):
    return jnp.maximum(x, 0.0) + jnp.log1p(jnp.exp(-jnp.abs(x)))


def _bdot(a, b):
    return jnp.dot(a.astype(BF16), b.astype(BF16), preferred_element_type=F32)


def _mod_body(c_ref, w_ref, b_ref, o_ref):
    c = c_ref[...]
    o_ref[0] = _bdot(_silu(c), w_ref[0]) + b_ref[0]


def _modulation(c, c_ctx, w_mod, b_mod):
    depth, d, n6 = w_mod.shape
    b = c.shape[0]
    rows = 2 * SUBLANES
    cc = jnp.zeros((rows, d), F32).at[:b].set(c).at[b].set(c_ctx)
    tn = 1536
    return pl.pallas_call(
        _mod_body,
        grid=(depth, n6 // tn),
        in_specs=[
            pl.BlockSpec((rows, d), lambda l, j: (0, 0)),
            pl.BlockSpec((1, d, tn), lambda l, j: (l, 0, j)),
            pl.BlockSpec((1, 1, tn), lambda l, j: (l, 0, j)),
        ],
        out_specs=pl.BlockSpec((1, rows, tn), lambda l, j: (l, 0, j)),
        out_shape=jax.ShapeDtypeStruct((depth, rows, n6), F32),
        compiler_params=_cparams(),
        name="adaln_mod",
    )(cc, w_mod, b_mod.reshape(depth, 1, n6))


def _mod_spec(j, d, ctx_row, off):
    if off == 0:
        return pl.BlockSpec((1, 1, d), lambda b, i: (jnp.where(i == 0, ctx_row, b), 0, j))
    return pl.BlockSpec((1, 1, d), lambda b, i: (b, 0, j))


W_Q, W_K, W_V, W_XU, W_Z, W_G, W_DT = 512, 128, 128, 768, 256, 256, 256
IN_PAD = W_Q + W_K + W_V + W_XU + W_Z + W_G + W_DT


def _rope(t, cos, sin_signed):
    rows = t.shape[0]
    lane = lax.broadcasted_iota(I32, (rows, LANES), 1)
    first = (lane % AXIS_DIM) < (AXIS_DIM // 2)
    outs = []
    for c in range(t.shape[1] // LANES):
        tc = t[:, c * LANES:(c + 1) * LANES]
        partner = jnp.where(first, pltpu.roll(tc, LANES - AXIS_DIM // 2, 1), pltpu.roll(tc, AXIS_DIM // 2, 1))
        outs.append(tc * cos + partner * sin_signed)
    return outs[0] if len(outs) == 1 else jnp.concatenate(outs, axis=1)


def _inproj_body(x_ref, xp_ref, xn_ref, sc_ref, sh_ref, w_ref, cos_ref, sin_ref, qw_ref, kw_ref, oq_ref, ok_ref,
                 cw_ref, cb_ref, q_ref, k_ref, v_ref, xbc_ref, u_ref, z_ref, g_ref, dt_ref, qkv_scr):
    i = pl.program_id(1)
    n = pl.num_programs(1)
    scale = 1.0 + sc_ref[0]
    shift = sh_ref[0]
    h = (x_ref[0] * scale + shift).astype(BF16)
    n_qkv = W_Q + W_K + W_V
    qkv_scr[...] = jnp.dot(h, w_ref[:, :n_qkv], preferred_element_type=F32)
    rest = jnp.dot(h, w_ref[:, n_qkv:], preferred_element_type=F32)
    halo = (jnp.concatenate([xp_ref[0], xn_ref[0]], axis=0) * scale + shift).astype(BF16)
    xu_halo = jnp.dot(halo, w_ref[:, n_qkv:n_qkv + W_XU], preferred_element_type=F32)
    xu = rest[:, :W_XU]
    z_ref[0] = rest[:, W_XU:W_XU + W_Z]
    g_ref[0] = rest[:, W_XU + W_Z:W_XU + W_Z + W_G]
    dt_ref[0] = rest[:, W_XU + W_Z + W_G:]

    tl = xu.shape[0]
    has_prev = i > 1
    has_next = (i > 0) & (i < n - 1)
    pm = jnp.where(has_prev, xu_halo[SUBLANES - 1:SUBLANES, :], 0.0)
    n0 = jnp.where(has_next, xu_halo[SUBLANES:SUBLANES + 1, :], 0.0)
    n1 = jnp.where(has_next, xu_halo[SUBLANES + 1:SUBLANES + 2, :], 0.0)
    row = lax.broadcasted_iota(I32, xu.shape, 0)
    xm1 = jnp.where(row == 0, pm, pltpu.roll(xu, 1, 0))
    xp1 = jnp.where(row == tl - 1, n0, pltpu.roll(xu, tl - 1, 0))
    xp2 = jnp.where(row == tl - 1, n1, jnp.where(row == tl - 2, n0, pltpu.roll(xu, tl - 2, 0)))
    cw = cw_ref[...]
    yc = xm1 * cw[0:1] + xu * cw[1:2] + xp1 * cw[2:3] + xp2 * cw[3:4] + cb_ref[...]
    n_xbc = xbc_ref.shape[-1]
    xbc_ref[0] = _silu(yc[:, :n_xbc])
    u_ref[0] = yc[:, n_xbc:]

    q = qkv_scr[:, :W_Q]
    k = qkv_scr[:, W_Q:W_Q + W_K]
    cos = cos_ref[...]
    sin = sin_ref[...]
    ssq = jnp.dot((q * q).astype(BF16), oq_ref[...], preferred_element_type=F32)
    qn = q * lax.rsqrt(ssq * (1.0 / HEAD_DIM) + EPS) * qw_ref[...]
    q_ref[0] = (_rope(qn, cos, sin) * (HEAD_DIM ** -0.5 * LOG2E)).T.astype(BF16)
    ssk = jnp.dot((k * k).astype(BF16), ok_ref[...], preferred_element_type=F32)
    kn = k * lax.rsqrt(ssk * (1.0 / HEAD_DIM) + EPS) * kw_ref[...]
    k_ref[0] = _rope(kn, cos, sin).astype(BF16)
    v_ref[0] = qkv_scr[:, W_Q + W_K:].astype(BF16)


def _inproj(xt, modl, wcat, cos_t, sin_t, qw, kw, ones_q, ones_k, conv_w, conv_b, n_xbc, ctx_row):
    b, t, d = xt.shape
    tl = SEQ_TILE
    grid = (b, t // tl)
    r = tl // SUBLANES
    nblk = t // SUBLANES
    full = lambda shape: pl.BlockSpec(shape, lambda bb, i: (0,) * len(shape))
    seq = lambda w: pl.BlockSpec((1, tl, w), lambda bb, i: (bb, i, 0))
    outs = [(W_K, BF16), (W_V, BF16), (n_xbc, F32), (W_XU - n_xbc, F32), (W_Z, F32), (W_G, F32), (W_DT, F32)]
    return pl.pallas_call(
        _inproj_body,
        grid=grid,
        in_specs=[
            seq(d),
            pl.BlockSpec((1, SUBLANES, d), lambda bb, i: (bb, jnp.maximum(i * r - 1, 0), 0)),
            pl.BlockSpec((1, SUBLANES, d), lambda bb, i: (bb, jnp.minimum((i + 1) * r, nblk - 1), 0)),
            _mod_spec(1, d, ctx_row, 0),
            _mod_spec(0, d, ctx_row, 0),
            full((d, IN_PAD)),
            pl.BlockSpec((tl, LANES), lambda bb, i: (i, 0)),
            pl.BlockSpec((tl, LANES), lambda bb, i: (i, 0)),
            full((1, W_Q)), full((1, W_K)), full((W_Q, W_Q)), full((W_K, W_K)),
            full((CONV_W, W_XU)), full((1, W_XU)),
        ],
        out_specs=[pl.BlockSpec((1, W_Q, tl), lambda bb, i: (bb, 0, i))] + [seq(w) for w, _ in outs],
        out_shape=[jax.ShapeDtypeStruct((b, W_Q, t), BF16)]
        + [jax.ShapeDtypeStruct((b, t, w), dt) for w, dt in outs],
        scratch_shapes=[pltpu.VMEM((tl, W_Q + W_K + W_V), F32)],
        compiler_params=_cparams(),
        name="in_proj",
    )(xt, xt, xt, modl, modl, wcat, cos_t, sin_t, qw, kw, ones_q, ones_k, conv_w, conv_b)


def _attn_body(qt_ref, k_ref, vt_ref, o_ref, s_buf, p_buf, acc_ref, *, n_keys, kc):
    nh, hd, tq = qt_ref.shape[1:]
    qt = jnp.concatenate([qt_ref[0, h] for h in range(nh)], axis=1)
    cols = nh * tq
    rem = n_keys % kc
    chunks = ([(0, rem)] if rem else []) + [(rem + kc * i, kc) for i in range(n_keys // kc)]
    n = len(chunks)

    def scores(c):
        s0, sz = chunks[c]
        s_buf[c % 2, :sz, :] = jnp.dot(k_ref[0, 0, s0:s0 + sz, :], qt, preferred_element_type=F32)

    def softmax(c, m):
        sz = chunks[c][1]
        s = s_buf[c % 2, :sz, :]
        m_new = jnp.maximum(m, jnp.max(s, axis=0, keepdims=True))
        p_buf[c % 2, :sz, :] = jnp.exp2(s - m_new).astype(BF16)
        return m_new, jnp.exp2(m - m_new)

    def weighted_values(c, alpha):
        s0, sz = chunks[c]
        acc_ref[...] = alpha * acc_ref[...] + jnp.dot(vt_ref[0, 0, :, s0:s0 + sz], p_buf[c % 2, :sz, :],
                                                      preferred_element_type=F32)

    acc_ref[...] = jnp.zeros_like(acc_ref)
    m = jnp.full((1, cols), -jnp.inf, F32)
    alpha = None
    scores(0)
    for j in range(n + 1):
        if j + 1 < n:
            scores(j + 1)
        prev_alpha = alpha
        if j < n:
            m, alpha = softmax(j, m)
        if j >= 1:
            weighted_values(j - 1, prev_alpha)
    o = acc_ref[:hd, :] * (1.0 / acc_ref[hd:hd + 1, :])
    for h in range(nh):
        o_ref[0, h * hd:(h + 1) * hd, :] = o[:, h * tq:(h + 1) * tq]


def _attention(qt, k, vt, n_q, q_off, n_keys, tq):
    b = qt.shape[0]
    assert q_off % tq == 0
    off = q_off // tq
    kc = min(ATTN_KEY_CHUNK, n_keys)
    cols = Q_PER_KV * tq
    vrows = vt.shape[2]
    return pl.pallas_call(
        functools.partial(_attn_body, n_keys=n_keys, kc=kc),
        grid=(b, KV_HEADS, n_q // tq),
        scratch_shapes=[pltpu.VMEM((2, kc, cols), F32), pltpu.VMEM((2, kc, cols), BF16),
                        pltpu.VMEM((vrows, cols), F32)],
        in_specs=[
            pl.BlockSpec((1, Q_PER_KV, HEAD_DIM, tq), lambda bb, g, i: (bb, g, 0, i + off)),
            pl.BlockSpec((1, 1, n_keys, HEAD_DIM), lambda bb, g, i: (bb, g, 0, 0)),
            pl.BlockSpec((1, 1, vrows, n_keys), lambda bb, g, i: (bb, g, 0, 0)),
        ],
        out_specs=pl.BlockSpec((1, Q_PER_KV * HEAD_DIM, tq), lambda bb, g, i: (bb, g, i)),
        out_shape=jax.ShapeDtypeStruct((b, KV_HEADS * Q_PER_KV * HEAD_DIM, n_q), F32),
        compiler_params=_cparams(),
        name="gqa_attention",
    )(qt, k, vt)


def _seq_order(d, c, nc):
    return jnp.where(c == 0, 0, jnp.where(d == 0, c, nc - c))


def _ssd_body(xf_ref, xb_ref, dtf_ref, dtb_ref, bias_ref, a_ref, dvec_ref, yf_ref, yb_ref, s_scr):
    @pl.when(pl.program_id(1) == 0)
    def _():
        s_scr[...] = jnp.zeros_like(s_scr)

    nb = xf_ref.shape[0]
    streams = range(2 * nb)
    xbc = [(xf_ref, xb_ref)[k % 2][k // 2] for k in streams]
    q = xbc[0].shape[0]
    xw = SSD_HEADS * SSD_HEAD_DIM
    gw = 2 * SSD_STATE
    x = [v[:, :xw] for v in xbc]
    bm = [v[:, xw:xw + gw] for v in xbc]
    cm = [v[:, xw + gw:xw + 2 * gw] for v in xbc]
    dt_raw = [(dtf_ref, dtb_ref)[k % 2][k // 2] for k in streams]
    dt = [_softplus(dt_raw[k] + bias_ref[k % 2]) for k in streams]
    da = [dt[k] * a_ref[k % 2] for k in streams]
    ii = lax.broadcasted_iota(I32, (q, q), 0)
    jj = lax.broadcasted_iota(I32, (q, q), 1)
    mask = [jj <= ii, jj >= ii]
    tm = [jnp.where(m_, 1.0, 0.0).astype(BF16) for m_ in mask]
    da_hi = [da[k].astype(BF16) for k in streams]
    da_lo = [(da[k] - da_hi[k].astype(F32)).astype(BF16) for k in streams]
    cs = [jnp.dot(tm[k % 2], da_hi[k], preferred_element_type=F32)
          + jnp.dot(tm[k % 2], da_lo[k], preferred_element_type=F32) for k in streams]
    tot = [cs[k][q - 1:q, :] if k % 2 == 0 else cs[k][0:1, :] for k in streams]
    cst = [cs[k].T for k in streams]
    dec = [jnp.exp(tot[k] - cs[k]) for k in streams]
    ecs = [jnp.exp(cs[k]) for k in streams]
    etot = [jnp.exp(tot[k]) for k in streams]
    dt_t = [dt[k].T for k in streams]
    w_t = [(dt[k] * dec[k]).T for k in streams]
    x_t = [x[k].T for k in streams]
    ys = [[] for _ in streams]
    nt = (((1,), (1,)), ((), ()))
    tn = (((0,), (0,)), ((), ()))
    for g in range(2):
        gs = slice(g * SSD_STATE, (g + 1) * SSD_STATE)
        bg = [bm[k][:, gs].astype(BF16) for k in streams]
        cg = [cm[k][:, gs].astype(BF16) for k in streams]
        gmat = [lax.dot_general(cg[k], bg[k], nt, preferred_element_type=F32) for k in streams]
        for hh in range(SSD_HEADS // 2):
            h = 2 * g + hh
            hs = slice(h * SSD_HEAD_DIM, (h + 1) * SSD_HEAD_DIM)
            lmat = [jnp.exp(jnp.where(mask[k % 2], cs[k][:, h:h + 1] - cst[k][h:h + 1, :], -jnp.inf))
                    for k in streams]
            y_diag = [_bdot(gmat[k] * lmat[k] * dt_t[k][h:h + 1, :], x[k][:, hs]) for k in streams]
            s_in = [s_scr[k, h] for k in streams]
            y_off = [lax.dot_general(cg[k], s_in[k].astype(BF16), nt, preferred_element_type=F32) * ecs[k][:, h:h + 1]
                     for k in streams]
            xd_t = [(x_t[k][hs, :] * w_t[k][h:h + 1, :]).astype(BF16) for k in streams]
            for k in streams:
                ys[k].append(y_diag[k] + y_off[k])
                s_scr[k, h] = etot[k][:, h:h + 1] * s_in[k] + jnp.dot(xd_t[k], bg[k], preferred_element_type=F32)
    for k in streams:
        if k % 2 == 0:
            yf_ref[k // 2] = jnp.concatenate(ys[k], axis=1) + x[k] * dvec_ref[...]
        else:
            yb_ref[k // 2] = jnp.concatenate(ys[k], axis=1)


def _ssd(xbc, dt, bias, a_neg, dvec):
    b, t, cw = xbc.shape
    q = SEQ_TILE
    nc = t // q
    xw = SSD_HEADS * SSD_HEAD_DIM
    fwd = lambda c: c
    bwd = lambda c: _seq_order(1, c, nc)
    nb = SSD_BATCH
    return pl.pallas_call(
        _ssd_body,
        grid=(b // nb, nc),
        in_specs=[
            pl.BlockSpec((nb, q, cw), lambda bb, c: (bb, fwd(c), 0)),
            pl.BlockSpec((nb, q, cw), lambda bb, c: (bb, bwd(c), 0)),
            pl.BlockSpec((nb, q, LANES), lambda bb, c: (bb, fwd(c), 0)),
            pl.BlockSpec((nb, q, LANES), lambda bb, c: (bb, bwd(c), 1)),
            pl.BlockSpec((2, 1, LANES), lambda bb, c: (0, 0, 0)),
            pl.BlockSpec((2, 1, LANES), lambda bb, c: (0, 0, 0)),
            pl.BlockSpec((1, xw), lambda bb, c: (0, 0)),
        ],
        out_specs=[pl.BlockSpec((nb, q, xw), lambda bb, c: (bb, fwd(c), 0)),
                   pl.BlockSpec((nb, q, xw), lambda bb, c: (bb, bwd(c), 0))],
        out_shape=[jax.ShapeDtypeStruct((b, t, xw), F32)] * 2,
        scratch_shapes=[pltpu.VMEM((2 * nb, SSD_HEADS, SSD_HEAD_DIM, SSD_STATE), F32)],
        compiler_params=_cparams(),
        name="ssd_scan",
    )(xbc, xbc, dt, dt, bias, a_neg, dvec)


def _rg_body(u_ref, w_ref, bias_ref, lam_ref, y_ref, a_s, v_s, o_s, h_s, *, pitch):
    d = pl.program_id(0)
    c = pl.program_id(1)
    nb, tl, width = u_ref.shape
    ng = width // LANES

    @pl.when(c == 0)
    def _():
        h_s[...] = jnp.zeros_like(h_s)

    sp = _softplus(-lam_ref[0])
    w = w_ref[0]
    bias = bias_ref[0]
    for b in range(nb):
        ub = u_ref[b]
        pre = jnp.dot(ub.astype(BF16), w, preferred_element_type=F32) + bias
        r = _sigmoid(pre[:, :width])
        ig = _sigmoid(pre[:, width:])
        a = jnp.exp((-RG_C) * r * sp)
        v = jnp.sqrt(1.0 - a * a) * ig * ub
        for j in range(ng):
            a_s[j, pl.ds(b * pitch, tl), :] = a[:, j * LANES:(j + 1) * LANES]
            v_s[j, pl.ds(b * pitch, tl), :] = v[:, j * LANES:(j + 1) * LANES]

    def step(t, hs):
        te = jnp.where(d == 0, t, tl - 1 - t)
        out = []
        for j in range(ng):
            at = a_s[j, pl.ds(te, nb, stride=pitch), :]
            vt = v_s[j, pl.ds(te, nb, stride=pitch), :]
            hj = at * hs[j] + vt
            o_s[j, pl.ds(te, nb, stride=pitch), :] = hj
            out.append(hj)
        return tuple(out)

    h0 = tuple(h_s[:, j * LANES:(j + 1) * LANES] for j in range(ng))
    hf = lax.fori_loop(0, tl, step, h0, unroll=8)
    for j in range(ng):
        h_s[:, j * LANES:(j + 1) * LANES] = hf[j]
    for b in range(nb):
        y_ref[0, b] = jnp.concatenate([o_s[j, pl.ds(b * pitch, tl), :] for j in range(ng)], axis=1)


def _rglru(u, wg, bias, lam):
    b, t, width = u.shape
    assert b == SUBLANES, "the recurrence keeps one sample per sublane"
    tl = SEQ_TILE
    nc = t // tl
    pitch = tl + SUBLANES
    ng = width // LANES
    slab = pltpu.VMEM((ng, b * pitch, LANES), F32)
    return pl.pallas_call(
        functools.partial(_rg_body, pitch=pitch),
        grid=(2, nc),
        in_specs=[
            pl.BlockSpec((b, tl, width), lambda d, c: (0, _seq_order(d, c, nc), 0)),
            pl.BlockSpec((1, width, 2 * width), lambda d, c: (d, 0, 0)),
            pl.BlockSpec((1, 1, 2 * width), lambda d, c: (d, 0, 0)),
            pl.BlockSpec((1, 1, width), lambda d, c: (d, 0, 0)),
        ],
        out_specs=pl.BlockSpec((1, b, tl, width), lambda d, c: (d, 0, _seq_order(d, c, nc), 0)),
        out_shape=jax.ShapeDtypeStruct((2, b, t, width), F32),
        scratch_shapes=[slab, slab, slab, pltpu.VMEM((b, width), F32)],
        compiler_params=_cparams(),
        name="rglru_scan",
    )(u, wg, bias, lam)


def _rms(x, w):
    return x * lax.rsqrt(jnp.mean(x * x, axis=-1, keepdims=True) + EPS) * w


def _layer_norm(t, w, b):
    mu = jnp.mean(t, axis=-1, keepdims=True)
    tc = t - mu
    var = jnp.mean(tc * tc, axis=-1, keepdims=True)
    return tc * lax.rsqrt(var + EPS) * w + b


def _gelu_tanh(x):
    return 0.5 * x * (1.0 + jnp.tanh(math.sqrt(2.0 / math.pi) * (x + 0.044715 * (x * x * x))))


def _pack_bf16_pair(lo, hi):
    lb = pltpu.bitcast(lo.astype(BF16).astype(F32), jnp.uint32)
    hb = pltpu.bitcast(hi.astype(BF16).astype(F32), jnp.uint32)
    return (lb >> 16) | (hb & jnp.uint32(0xFFFF0000))


def _merge_body(*refs, has_ctx):
    if has_ctx:
        actx_ref, refs = refs[0], refs[1:]
    (alat_ref, ysf_ref, ysb_ref, z_ref, hr_ref, g_ref, x_ref, g1_ref, sc2_ref, sh2_ref, aw_ref, sw_ref, rw_ref,
     wo_ref, lnw_ref, lnb_ref, wr_ref, x1_ref, hp_ref, aff_ref, afft_ref, cat_scr, proj_scr) = refs
    tl = x_ref.shape[1]
    nh = MERGE_ROW_SPLIT
    rows = [slice(j * tl // nh, (j + 1) * tl // nh) for j in range(nh)]
    for rs in rows:
        a = alat_ref[0, :, rs]
        if has_ctx:
            a = jnp.where(pl.program_id(1) == 0, actx_ref[0, :, rs], a)
        an = _rms(a.T, aw_ref[...])
        sn = _rms((ysf_ref[0, rs] + ysb_ref[0, rs]) * _silu(z_ref[0, rs]), sw_ref[...])
        rn = _rms((hr_ref[0, 0, rs] + hr_ref[1, 0, rs]) * _gelu_tanh(g_ref[0, rs]), rw_ref[...])
        cat_scr[rs] = jnp.concatenate([an, sn, rn], axis=1).astype(BF16)
    for rs in rows:
        proj_scr[rs] = jnp.dot(cat_scr[rs], wo_ref[...], preferred_element_type=F32)
    for rs in rows:
        x1 = _layer_norm(DEEPNORM_ALPHA * x_ref[0, rs] + g1_ref[0] * proj_scr[rs], lnw_ref[...], lnb_ref[...])
        x1_ref[0, rs] = x1
        h2 = x1 * (1.0 + sc2_ref[0]) + sh2_ref[0]
        half = h2.shape[1] // 2
        hp_ref[0, rs] = _pack_bf16_pair(h2[:, :half], h2[:, half:])
        logits = jnp.dot(h2.astype(BF16), wr_ref[...], preferred_element_type=F32)
        lane = lax.broadcasted_iota(I32, logits.shape, 1)
        logits = jnp.where(lane < N_EXPERTS, logits, -jnp.inf)
        e = jnp.exp(logits - jnp.max(logits, axis=-1, keepdims=True))
        aff = e / jnp.sum(e, axis=-1, keepdims=True)
        aff_ref[0, rs] = aff
        afft_ref[0, :, rs] = aff.T[:N_EXPERTS, :]


def _merge(attn_ctx, attn_lat, ys, z, hr, g, xt, modl, aw, sw, rw, wo, lnw, lnb, wr, ctx_row, off):
    b, t, d = xt.shape
    tl = SEQ_TILE
    nt = t // tl - off
    has_ctx = off == 0
    aw_ = attn_lat.shape[1]
    seq = lambda w: pl.BlockSpec((1, tl, w), lambda bb, i: (bb, i + off, 0))
    pair = lambda w: pl.BlockSpec((2, 1, tl, w), lambda bb, i: (0, bb, i + off, 0))
    full = lambda shape: pl.BlockSpec(shape, lambda bb, i: (0,) * len(shape))
    out = lambda w: pl.BlockSpec((1, tl, w), lambda bb, i: (bb, i, 0))
    lat_off = 1 - off
    in_specs = [
        pl.BlockSpec((1, aw_, tl), lambda bb, i: (bb, 0, jnp.maximum(i - lat_off, 0))),
        seq(ys[0].shape[-1]), seq(ys[1].shape[-1]), seq(z.shape[-1]), pair(hr.shape[-1]), seq(g.shape[-1]), seq(d),
        _mod_spec(2, d, ctx_row, off), _mod_spec(4, d, ctx_row, off), _mod_spec(3, d, ctx_row, off),
        full(aw.shape), full(sw.shape), full(rw.shape), full(wo.shape), full(lnw.shape), full(lnb.shape),
        full(wr.shape),
    ]
    args = [attn_lat, ys[0], ys[1], z, hr, g, xt, modl, modl, modl, aw, sw, rw, wo, lnw, lnb, wr]
    if has_ctx:
        in_specs = [pl.BlockSpec((1, aw_, tl), lambda bb, i: (bb, 0, 0))] + in_specs
        args = [attn_ctx] + args
    rows = nt * tl
    return pl.pallas_call(
        functools.partial(_merge_body, has_ctx=has_ctx),
        grid=(b, nt),
        in_specs=in_specs,
        out_specs=[out(d), out(d // 2), out(LANES), pl.BlockSpec((1, N_EXPERTS, tl), lambda bb, i: (bb, 0, i))],
        out_shape=[
            jax.ShapeDtypeStruct((b, rows, d), F32),
            jax.ShapeDtypeStruct((b, rows, d // 2), jnp.uint32),
            jax.ShapeDtypeStruct((b, rows, LANES), F32),
            jax.ShapeDtypeStruct((b, N_EXPERTS, rows), F32),
        ],
        scratch_shapes=[pltpu.VMEM((tl, wo.shape[0]), BF16), pltpu.VMEM((tl, d), F32)],
        compiler_params=_cparams(),
        name="merge_outproj_ln1_router",
    )(*args)


def _topk_body(aff_ref, idx_ref, cum_scr, *, cap):
    aff = aff_ref[0]
    ne, n = aff.shape
    bits = pltpu.bitcast(aff, I32)

    def search(i, thr):
        cand = thr | lax.shift_left(jnp.int32(1), 30 - i)
        cnt = jnp.sum(jnp.where(bits >= cand, 1.0, 0.0), axis=1, keepdims=True)
        return jnp.where(cnt >= cap, cand, thr)

    thr = lax.fori_loop(0, 31, search, jnp.zeros((ne, 1), I32))
    gt = jnp.where(bits > thr, 1.0, 0.0)
    eq = jnp.where(bits == thr, 1.0, 0.0)
    need = cap - jnp.sum(gt, axis=1, keepdims=True)
    nblk = n // LANES
    r_ = lax.broadcasted_iota(I32, (LANES, LANES), 0)
    c_ = lax.broadcasted_iota(I32, (LANES, LANES), 1)
    upper = jnp.where(r_ <= c_, 1.0, 0.0).astype(BF16)

    def prefix(blocks):
        outs = []
        off = jnp.zeros((ne, 1), F32)
        for mk in blocks:
            w = jnp.dot(mk.astype(BF16), upper, preferred_element_type=F32) + off
            outs.append(w)
            off = w[:, LANES - 1:LANES]
        return outs

    blk = lambda a, k: a[:, k * LANES:(k + 1) * LANES]
    tie_rank = prefix([blk(eq, k) for k in range(nblk)])
    cum = prefix([jnp.maximum(blk(gt, k), blk(eq, k) * jnp.where(tie_rank[k] <= need, 1.0, 0.0))
                  for k in range(nblk)])
    kpad = cum_scr.shape[0] // ne
    for k in range(kpad):
        cum_scr[k * ne:(k + 1) * ne, :] = cum[k] if k < nblk else jnp.zeros((ne, LANES), F32)
    width = -(-cap // LANES) * LANES
    slot = lax.broadcasted_iota(I32, (LANES, width), 1).astype(F32)
    blk_id = lax.broadcasted_iota(I32, (LANES, width), 0).astype(F32)
    blk_col = lax.broadcasted_iota(I32, (LANES, 1), 0)
    tn = (((0,), (0,)), ((), ()))

    def per_expert(e, carry):
        cum_e = cum_scr[pl.ds(e, kpad, stride=ne), :]
        if kpad < LANES:
            cum_e = jnp.concatenate([cum_e, jnp.zeros((LANES - kpad, LANES), F32)], axis=0)
        ends = jnp.where(blk_col < nblk, cum_e[:, LANES - 1:LANES], float(2 * n))
        full = jnp.sum(jnp.where(ends <= slot, 1.0, 0.0), axis=0, keepdims=True)
        pick = jnp.where(blk_id == full, 1.0, 0.0).astype(BF16)
        hi = jnp.floor(cum_e * (1.0 / 32.0))
        lo = cum_e - 32.0 * hi
        straddle = (32.0 * lax.dot_general(hi.astype(BF16), pick, tn, preferred_element_type=F32)
                    + lax.dot_general(lo.astype(BF16), pick, tn, preferred_element_type=F32))
        inside = jnp.sum(jnp.where(straddle <= slot, 1.0, 0.0), axis=0, keepdims=True)
        idx_ref[0, pl.ds(e, 1), :] = (float(LANES) * full + inside)[:, :cap].astype(I32)
        return carry

    lax.fori_loop(0, ne, per_expert, 0, unroll=4)


def _topk(aff_t, cap):
    b, e, n = aff_t.shape
    return pl.pallas_call(
        functools.partial(_topk_body, cap=cap),
        grid=(b,),
        in_specs=[pl.BlockSpec((1, e, n), lambda bb: (bb, 0, 0))],
        out_specs=pl.BlockSpec((1, e, cap), lambda bb: (bb, 0, 0)),
        out_shape=jax.ShapeDtypeStruct((b, e, cap), I32),
        scratch_shapes=[pltpu.VMEM((max(n // LANES, SUBLANES) * e, LANES), F32)],
        compiler_params=_cparams(),
        name="expert_choice_topk",
    )(aff_t)


def _moe_body(idxp_ref, idxc_ref, idxn_ref, hp_ref, aff_ref, wg_ref, wu_ref, wd_ref, out_ref,
              xs_scr, ag_scr, y_scr, xb_scr, hid_scr, *, cap):
    e = pl.program_id(1)
    f = pl.program_id(2)
    ne = pl.num_programs(1)
    nf = pl.num_programs(2)
    cur = e % 2
    oth = 1 - cur

    @pl.when((pl.program_id(0) == 0) & (e == 0) & (f == 0))
    def _():
        y_scr[...] = jnp.zeros_like(y_scr)

    @pl.when((e == 0) & (f == 0))
    def _():
        out_ref[...] = jnp.zeros_like(out_ref)

        def gather(j, carry):
            for u in range(SUBLANES):
                t = idxc_ref[0, 0, 0, j * SUBLANES + u]
                xs_scr[0, j, u:u + 1, :] = hp_ref[0, pl.ds(t, 1), :]
                ag_scr[0, j, u:u + 1, :] = aff_ref[0, pl.ds(t, 1), :]
            return carry

        lax.fori_loop(0, cap // SUBLANES, gather, 0)

    w = xs_scr[cur].reshape(cap, xs_scr.shape[-1])
    half = w.shape[1]
    xb_scr[:, :half] = pltpu.bitcast(w << 16, F32).astype(BF16)
    xb_scr[:, half:] = pltpu.bitcast(w & jnp.uint32(0xFFFF0000), F32).astype(BF16)
    lane = lax.broadcasted_iota(I32, (cap, LANES), 1)
    gate = jnp.sum(jnp.where(lane == e, ag_scr[cur].reshape(cap, LANES), 0.0), axis=1, keepdims=True)
    share = cap // nf
    base = f * share
    base_tile = f * (share // SUBLANES)
    has_prev = e > 0

    def row_copies(r0, r1):
        for g0 in range(r0, r1, MOE_ROW_GROUP):
            rows = range(g0, min(g0 + MOE_ROW_GROUP, r1))
            tps = [idxp_ref[0, 0, 0, base + r] for r in rows]
            sums = [out_ref[0, pl.ds(tp, 1), :]
                    + jnp.where(has_prev, y_scr[oth, base_tile + r // SUBLANES, r % SUBLANES:r % SUBLANES + 1, :], 0.0)
                    for tp, r in zip(tps, rows)]
            for tp, v in zip(tps, sums):
                out_ref[0, pl.ds(tp, 1), :] = v
            for r in rows:
                tn = idxn_ref[0, 0, 0, base + r]
                j, u = base_tile + r // SUBLANES, r % SUBLANES
                xs_scr[oth, j, u:u + 1, :] = hp_ref[0, pl.ds(tn, 1), :]
                ag_scr[oth, j, u:u + 1, :] = aff_ref[0, pl.ds(tn, 1), :]

    tf = wg_ref.shape[3]
    nsl = tf // MOE_FF_SLICE
    sl = lambda c: slice(c * MOE_FF_SLICE, (c + 1) * MOE_FF_SLICE)

    def hidden(c):
        xb = xb_scr[...]
        hg = jnp.dot(xb, wg_ref[0, 0, :, sl(c)], preferred_element_type=F32)
        hu = jnp.dot(xb, wu_ref[0, 0, :, sl(c)], preferred_element_type=F32)
        hid_scr[:, sl(c)] = (_silu(hg) * hu).astype(BF16)

    for c in range(nsl):
        hidden(c)
        row_copies(share * c // nsl, share * (c + 1) // nsl)
    yp = jnp.dot(hid_scr[...], wd_ref[0, 0], preferred_element_type=F32)
    y_old = y_scr[cur].reshape(cap, y_scr.shape[-1])
    y_new = (jnp.where(f == 0, 0.0, y_old) + yp) * jnp.where(f == nf - 1, gate, 1.0)
    y_scr[cur] = y_new.reshape(y_scr.shape[1:])

    @pl.when((e == ne - 1) & (f == nf - 1))
    def _():
        def scatter(j, carry):
            for u in range(SUBLANES):
                t = idxc_ref[0, 0, 0, j * SUBLANES + u]
                out_ref[0, pl.ds(t, 1), :] = out_ref[0, pl.ds(t, 1), :] + y_scr[cur, j, u:u + 1, :]
            return carry

        lax.fori_loop(0, cap // SUBLANES, scatter, 0)


def _moe(idx, hp, aff, wg, wu, wd, layer, tf):
    b, t, half = hp.shape
    d = 2 * half
    ne, ff = wg.shape[1], wg.shape[3]
    cap = idx.shape[-1]
    nf = ff // tf
    assert nf >= 2 and cap % nf == 0
    one = pl.Buffered(1)
    idx4 = idx.reshape(b, ne, 1, cap)
    smem = lambda shift: pl.BlockSpec((1, 1, 1, cap), lambda bb, e, f: (bb, jnp.clip(e + shift, 0, ne - 1), 0, 0),
                                      memory_space=pltpu.SMEM)
    return pl.pallas_call(
        functools.partial(_moe_body, cap=cap),
        grid=(b, ne, nf),
        in_specs=[
            smem(-1), smem(0), smem(1),
            pl.BlockSpec((1, t, half), lambda bb, e, f: (bb, 0, 0), pipeline_mode=one),
            pl.BlockSpec((1, t, LANES), lambda bb, e, f: (bb, 0, 0), pipeline_mode=one),
            pl.BlockSpec((1, 1, d, tf), lambda bb, e, f: (layer, e, 0, f)),
            pl.BlockSpec((1, 1, d, tf), lambda bb, e, f: (layer, e, 0, f)),
            pl.BlockSpec((1, 1, tf, d), lambda bb, e, f: (layer, e, f, 0)),
        ],
        out_specs=pl.BlockSpec((1, t, d), lambda bb, e, f: (bb, 0, 0), pipeline_mode=one),
        out_shape=jax.ShapeDtypeStruct((b, t, d), F32),
        scratch_shapes=[
            pltpu.VMEM((2, cap // SUBLANES, SUBLANES, half), jnp.uint32),
            pltpu.VMEM((2, cap // SUBLANES, SUBLANES, LANES), F32),
            pltpu.VMEM((2, cap // SUBLANES, SUBLANES, d), F32),
            pltpu.VMEM((cap, d), BF16),
            pltpu.VMEM((cap, tf), BF16),
        ],
        compiler_params=_cparams(VMEM_LIMIT_MOE),
        name="expert_ffn",
    )(idx4, idx4, idx4, hp, aff, wg, wu, wd)


def _ln2_body(x1_ref, moe_ref, g2_ref, g2c_ref, w_ref, b_ref, o_ref, *, n_ctx):
    g2 = g2_ref[0]
    if n_ctx:
        tl = x1_ref.shape[1]
        row = lax.broadcasted_iota(I32, (tl, 1), 0) + pl.program_id(1) * tl
        g2 = jnp.where(row < n_ctx, g2c_ref[0], g2)
    o_ref[0] = _layer_norm(DEEPNORM_ALPHA * x1_ref[0] + g2 * moe_ref[0], w_ref[...], b_ref[...])


def _ln2(x1, moe, modl, w, bias, ctx_row, n_ctx):
    b, t, d = x1.shape
    tl = t // LN2_TILES
    assert t % LN2_TILES == 0 and tl % SUBLANES == 0
    seq = pl.BlockSpec((1, tl, d), lambda bb, i: (bb, i, 0))
    full = pl.BlockSpec((1, d), lambda bb, i: (0, 0))
    return pl.pallas_call(
        functools.partial(_ln2_body, n_ctx=n_ctx),
        grid=(b, LN2_TILES),
        in_specs=[seq, seq,
                  pl.BlockSpec((1, 1, d), lambda bb, i: (bb, 0, 5)),
                  pl.BlockSpec((1, 1, d), lambda bb, i: (ctx_row, 0, 5)),
                  full, full],
        out_specs=seq,
        out_shape=jax.ShapeDtypeStruct((b, t, d), F32),
        compiler_params=_cparams(),
        name="ln2",
    )(x1, moe, modl, modl, w, bias)


def _block_diag_ones(width, block):
    r = jnp.arange(width)[:, None] // block
    c = jnp.arange(width)[None, :] // block
    return (r == c).astype(BF16)


def _block_diag(w):
    k, d, e = w.shape
    eye = jnp.eye(k, dtype=w.dtype)
    return (eye[:, None, :, None] * w[:, :, None, :]).reshape(k * d, k * e)


def _rope_tables(n_ctx, n_lat):
    pos = jnp.arange(n_lat)
    row = (pos // GRID_W).astype(F32)
    col = (pos % GRID_W).astype(F32)
    inv_freq = ROPE_THETA ** (-jnp.arange(0, AXIS_DIM, 2, dtype=F32) / AXIS_DIM)
    ang_r = row[:, None] * inv_freq
    ang_c = col[:, None] * inv_freq
    cos_h = jnp.concatenate([jnp.cos(ang_r)] * 2 + [jnp.cos(ang_c)] * 2, axis=1)
    sin_h = jnp.concatenate([-jnp.sin(ang_r), jnp.sin(ang_r), -jnp.sin(ang_c), jnp.sin(ang_c)], axis=1)
    reps = LANES // HEAD_DIM
    cos_t = jnp.concatenate([jnp.ones((n_ctx, LANES), F32), jnp.tile(cos_h, (1, reps))], axis=0)
    sin_t = jnp.concatenate([jnp.zeros((n_ctx, LANES), F32), jnp.tile(sin_h, (1, reps))], axis=0)
    return cos_t, sin_t


def _layer(xt, modl, rope, p, experts, layer, last, n_ctx, ctx_row):
    b, t, d = xt.shape
    n_lat = t - n_ctx
    cos_t, sin_t = rope
    w = p["w_in"]
    pad = jnp.zeros((d, LANES - SSD_HEADS), F32)
    wcat = jnp.concatenate(
        [w[:, 0:768], w[:, 768:1280], w[:, 1544:1800], w[:, 1280:1536], w[:, 1800:2056],
         w[:, 1536:1540], pad, w[:, 1540:1544], pad], axis=1).astype(BF16)
    qw = jnp.tile(p["q_norm"], W_Q // HEAD_DIM)[None]
    kw = jnp.tile(p["k_norm"], W_K // HEAD_DIM)[None]
    conv_w = jnp.concatenate([p["ssd_conv_w"], p["rg_conv_w"]], axis=1)
    conv_b = jnp.concatenate([p["ssd_conv_b"], p["rg_conv_b"]])[None]
    qt, k, v, xbc, u, z, g, dt = _inproj(xt, modl, wcat, cos_t, sin_t, qw, kw,
                                         _block_diag_ones(W_Q, HEAD_DIM), _block_diag_ones(W_K, HEAD_DIM),
                                         conv_w, conv_b, p["ssd_conv_w"].shape[1], ctx_row)

    qt = qt.reshape(b, W_Q // HEAD_DIM, HEAD_DIM, t)
    kh = k.reshape(b, t, KV_HEADS, HEAD_DIM).transpose(0, 2, 1, 3)
    vt = v.reshape(b, t, KV_HEADS, HEAD_DIM).transpose(0, 2, 3, 1)
    ones_pad = jnp.zeros((b, KV_HEADS, BF16_ROWS, t), BF16).at[:, :, 0, :].set(1.0)
    vt = jnp.concatenate([vt, ones_pad], axis=2)
    attn_lat = _attention(qt[..., n_ctx:], kh, vt, n_lat, 0, t, min(ATTN_Q_TILE, n_lat))
    attn_ctx = None if last else _attention(qt, kh, vt, n_ctx, 0, n_ctx, min(ATTN_Q_TILE, n_ctx))

    lane_pad = lambda a: jnp.pad(a, ((0, 0), (0, LANES - a.shape[1])))[:, None, :]
    ys = _ssd(xbc, dt, lane_pad(p["ssd_dt_bias"]), lane_pad(-jnp.exp(p["ssd_a_log"])),
              jnp.repeat(p["ssd_d"], SSD_HEAD_DIM)[None])
    wgate = jnp.stack([jnp.concatenate([_block_diag(p["rg_wa"][j]), _block_diag(p["rg_wx"][j])], axis=1)
                       for j in range(2)]).astype(BF16)
    bgate = jnp.concatenate([p["rg_ba"], p["rg_bx"]], axis=1)[:, None, :]
    hr = _rglru(u, wgate, bgate, p["rg_lambda"][:, None, :])

    off = 1 if last else 0
    wr = jnp.pad(p["w_router"], ((0, 0), (0, LANES - N_EXPERTS))).astype(BF16)
    x1, hp, aff, aff_t = _merge(attn_ctx, attn_lat, ys, z, hr, g, xt, modl,
                                p["attn_out_norm"][None], p["ssd_norm"][None], p["rg_out_norm"][None],
                                p["w_out"].astype(BF16), p["ln1_w"][None], p["ln1_b"][None], wr, ctx_row, off)

    if last:
        idx = _topk(aff_t, EC_CAPACITY * n_lat // N_EXPERTS)
    else:
        idx_lat = _topk(aff_t[:, :, n_ctx:], EC_CAPACITY * n_lat // N_EXPERTS) + n_ctx
        idx_ctx = _topk(aff_t[:, :, :n_ctx], EC_CAPACITY * n_ctx // N_EXPERTS)
        idx = jnp.concatenate([idx_lat, idx_ctx], axis=-1)
    moe = _moe(idx, hp, aff, *experts, layer, MOE_FF_TILE)
    return _ln2(x1, moe, modl, p["ln2_w"][None], p["ln2_b"][None], ctx_row, 0 if last else n_ctx)


def kernel(x, c, ctx, c_ctx, w_mod, b_mod, w_in, q_norm, k_norm, attn_out_norm, ssd_conv_w, ssd_conv_b, ssd_dt_bias, ssd_a_log, ssd_d, ssd_norm, rg_conv_w, rg_conv_b, rg_wa, rg_ba, rg_wx, rg_bx, rg_lambda, rg_out_norm, w_out, ln1_w, ln1_b, w_router, w_gate, w_up, w_down, ln2_w, ln2_b):
    b, n_lat, d = x.shape
    n_ctx = ctx.shape[1]
    assert n_ctx == SEQ_TILE and n_lat % 512 == 0 and b == SUBLANES
    params = dict(w_in=w_in, q_norm=q_norm, k_norm=k_norm, attn_out_norm=attn_out_norm, ssd_conv_w=ssd_conv_w,
                  ssd_conv_b=ssd_conv_b, ssd_dt_bias=ssd_dt_bias, ssd_a_log=ssd_a_log, ssd_d=ssd_d,
                  ssd_norm=ssd_norm, rg_conv_w=rg_conv_w, rg_conv_b=rg_conv_b, rg_wa=rg_wa, rg_ba=rg_ba,
                  rg_wx=rg_wx, rg_bx=rg_bx, rg_lambda=rg_lambda, rg_out_norm=rg_out_norm, w_out=w_out,
                  ln1_w=ln1_w, ln1_b=ln1_b, w_router=w_router, ln2_w=ln2_w, ln2_b=ln2_b)
    experts = (w_gate.astype(BF16), w_up.astype(BF16), w_down.astype(BF16))
    mod = _modulation(c, c_ctx, w_mod, b_mod)
    rope = _rope_tables(n_ctx, n_lat)
    xt = jnp.concatenate([ctx, x], axis=1)
    depth = w_mod.shape[0]
    for l in range(depth):
        p = {name: val[l] for name, val in params.items()}
        modl = mod[l].reshape(mod.shape[1], 1, mod.shape[2])
        xt = _layer(xt, modl, rope, p, experts, l, l == depth - 1, n_ctx, b)
    return xt
```

```python
import functools
import math

import jax
import jax.numpy as jnp
from jax import lax
from jax.experimental import pallas as pl
from jax.experimental.pallas import tpu as pltpu

F32 = jnp.float32
BF16 = jnp.bfloat16
I32 = jnp.int32

DEPTH = 2
GRID_W = 64
HEAD_DIM = 64
KV_HEADS = 2
Q_PER_KV = 4
AXIS_DIM = HEAD_DIM // 2
ROPE_THETA = 10000.0
SSD_HEADS = 4
SSD_HEAD_DIM = 64
SSD_STATE = 64
RG_BLOCKS = 4
RG_C = 8.0
CONV_W = 4
N_EXPERTS = 16
EC_CAPACITY = 2
EPS = 1e-6
DEEPNORM_ALPHA = (2 * DEPTH) ** 0.25

LANES = 128
SUBLANES = 8
BF16_ROWS = 16
SEQ_TILE = 256
ATTN_Q_TILE = 512
ATTN_KEY_CHUNK = 256
MERGE_ROW_SPLIT = 2
LN2_TILES = 4
SSD_BATCH = 4
MOE_FF_TILE = 1024
MOE_FF_SLICE = 256
MOE_ROW_GROUP = 4
LOG2E = math.log2(math.e)
VMEM_LIMIT = 48 * 1024 * 1024
VMEM_LIMIT_MOE = 60 * 1024 * 1024


def _cparams(limit=VMEM_LIMIT):
    return pltpu.CompilerParams(vmem_limit_bytes=limit)


def _sigmoid(x):
    return 1.0 / (1.0 + jnp.exp(-x))


def _silu(x):
    return x * _sigmoid(x)


def _softplus(x):
    return jnp.maximum(x, 0.0) + jnp.log1p(jnp.exp(-jnp.abs(x)))


def _bdot(a, b):
    return jnp.dot(a.astype(BF16), b.astype(BF16), preferred_element_type=F32)


def _mod_body(c_ref, w_ref, b_ref, o_ref):
    c = c_ref[...]
    o_ref[0] = _bdot(_silu(c), w_ref[0]) + b_ref[0]


def _modulation(c, c_ctx, w_mod, b_mod):
    depth, d, n6 = w_mod.shape
    b = c.shape[0]
    rows = 2 * SUBLANES
    cc = jnp.zeros((rows, d), F32).at[:b].set(c).at[b].set(c_ctx)
    tn = 1536
    return pl.pallas_call(
        _mod_body,
        grid=(depth, n6 // tn),
        in_specs=[
            pl.BlockSpec((rows, d), lambda l, j: (0, 0)),
            pl.BlockSpec((1, d, tn), lambda l, j: (l, 0, j)),
            pl.BlockSpec((1, 1, tn), lambda l, j: (l, 0, j)),
        ],
        out_specs=pl.BlockSpec((1, rows, tn), lambda l, j: (l, 0, j)),
        out_shape=jax.ShapeDtypeStruct((depth, rows, n6), F32),
        compiler_params=_cparams(),
        name="adaln_mod",
    )(cc, w_mod, b_mod.reshape(depth, 1, n6))


def _mod_spec(j, d, ctx_row, off):
    if off == 0:
        return pl.BlockSpec((1, 1, d), lambda b, i: (jnp.where(i == 0, ctx_row, b), 0, j))
    return pl.BlockSpec((1, 1, d), lambda b, i: (b, 0, j))


W_Q, W_K, W_V, W_XU, W_Z, W_G, W_DT = 512, 128, 128, 768, 256, 256, 256
IN_PAD = W_Q + W_K + W_V + W_XU + W_Z + W_G + W_DT


def _rope(t, cos, sin_signed):
    rows = t.shape[0]
    lane = lax.broadcasted_iota(I32, (rows, LANES), 1)
    first = (lane % AXIS_DIM) < (AXIS_DIM // 2)
    outs = []
    for c in range(t.shape[1] // LANES):
        tc = t[:, c * LANES:(c + 1) * LANES]
        partner = jnp.where(first, pltpu.roll(tc, LANES - AXIS_DIM // 2, 1), pltpu.roll(tc, AXIS_DIM // 2, 1))
        outs.append(tc * cos + partner * sin_signed)
    return outs[0] if len(outs) == 1 else jnp.concatenate(outs, axis=1)


def _inproj_body(x_ref, xp_ref, xn_ref, sc_ref, sh_ref, w_ref, cos_ref, sin_ref, qw_ref, kw_ref, oq_ref, ok_ref,
                 cw_ref, cb_ref, q_ref, k_ref, v_ref, xbc_ref, u_ref, z_ref, g_ref, dt_ref, qkv_scr):
    i = pl.program_id(1)
    n = pl.num_programs(1)
    scale = 1.0 + sc_ref[0]
    shift = sh_ref[0]
    h = (x_ref[0] * scale + shift).astype(BF16)
    n_qkv = W_Q + W_K + W_V
    qkv_scr[...] = jnp.dot(h, w_ref[:, :n_qkv], preferred_element_type=F32)
    rest = jnp.dot(h, w_ref[:, n_qkv:], preferred_element_type=F32)
    halo = (jnp.concatenate([xp_ref[0], xn_ref[0]], axis=0) * scale + shift).astype(BF16)
    xu_halo = jnp.dot(halo, w_ref[:, n_qkv:n_qkv + W_XU], preferred_element_type=F32)
    xu = rest[:, :W_XU]
    z_ref[0] = rest[:, W_XU:W_XU + W_Z]
    g_ref[0] = rest[:, W_XU + W_Z:W_XU + W_Z + W_G]
    dt_ref[0] = rest[:, W_XU + W_Z + W_G:]

    tl = xu.shape[0]
    has_prev = i > 1
    has_next = (i > 0) & (i < n - 1)
    pm = jnp.where(has_prev, xu_halo[SUBLANES - 1:SUBLANES, :], 0.0)
    n0 = jnp.where(has_next, xu_halo[SUBLANES:SUBLANES + 1, :], 0.0)
    n1 = jnp.where(has_next, xu_halo[SUBLANES + 1:SUBLANES + 2, :], 0.0)
    row = lax.broadcasted_iota(I32, xu.shape, 0)
    xm1 = jnp.where(row == 0, pm, pltpu.roll(xu, 1, 0))
    xp1 = jnp.where(row == tl - 1, n0, pltpu.roll(xu, tl - 1, 0))
    xp2 = jnp.where(row == tl - 1, n1, jnp.where(row == tl - 2, n0, pltpu.roll(xu, tl - 2, 0)))
    cw = cw_ref[...]
    yc = xm1 * cw[0:1] + xu * cw[1:2] + xp1 * cw[2:3] + xp2 * cw[3:4] + cb_ref[...]
    n_xbc = xbc_ref.shape[-1]
    xbc_ref[0] = _silu(yc[:, :n_xbc])
    u_ref[0] = yc[:, n_xbc:]

    q = qkv_scr[:, :W_Q]
    k = qkv_scr[:, W_Q:W_Q + W_K]
    cos = cos_ref[...]
    sin = sin_ref[...]
    ssq = jnp.dot((q * q).astype(BF16), oq_ref[...], preferred_element_type=F32)
    qn = q * lax.rsqrt(ssq * (1.0 / HEAD_DIM) + EPS) * qw_ref[...]
    q_ref[0] = (_rope(qn, cos, sin) * (HEAD_DIM ** -0.5 * LOG2E)).T.astype(BF16)
    ssk = jnp.dot((k * k).astype(BF16), ok_ref[...], preferred_element_type=F32)
    kn = k * lax.rsqrt(ssk * (1.0 / HEAD_DIM) + EPS) * kw_ref[...]
    k_ref[0] = _rope(kn, cos, sin).astype(BF16)
    v_ref[0] = qkv_scr[:, W_Q + W_K:].astype(BF16)


def _inproj(xt, modl, wcat, cos_t, sin_t, qw, kw, ones_q, ones_k, conv_w, conv_b, n_xbc, ctx_row):
    b, t, d = xt.shape
    tl = SEQ_TILE
    grid = (b, t // tl)
    r = tl // SUBLANES
    nblk = t // SUBLANES
    full = lambda shape: pl.BlockSpec(shape, lambda bb, i: (0,) * len(shape))
    seq = lambda w: pl.BlockSpec((1, tl, w), lambda bb, i: (bb, i, 0))
    outs = [(W_K, BF16), (W_V, BF16), (n_xbc, F32), (W_XU - n_xbc, F32), (W_Z, F32), (W_G, F32), (W_DT, F32)]
    return pl.pallas_call(
        _inproj_body,
        grid=grid,
        in_specs=[
            seq(d),
            pl.BlockSpec((1, SUBLANES, d), lambda bb, i: (bb, jnp.maximum(i * r - 1, 0), 0)),
            pl.BlockSpec((1, SUBLANES, d), lambda bb, i: (bb, jnp.minimum((i + 1) * r, nblk - 1), 0)),
            _mod_spec(1, d, ctx_row, 0),
            _mod_spec(0, d, ctx_row, 0),
            full((d, IN_PAD)),
            pl.BlockSpec((tl, LANES), lambda bb, i: (i, 0)),
            pl.BlockSpec((tl, LANES), lambda bb, i: (i, 0)),
            full((1, W_Q)), full((1, W_K)), full((W_Q, W_Q)), full((W_K, W_K)),
            full((CONV_W, W_XU)), full((1, W_XU)),
        ],
        out_specs=[pl.BlockSpec((1, W_Q, tl), lambda bb, i: (bb, 0, i))] + [seq(w) for w, _ in outs],
        out_shape=[jax.ShapeDtypeStruct((b, W_Q, t), BF16)]
        + [jax.ShapeDtypeStruct((b, t, w), dt) for w, dt in outs],
        scratch_shapes=[pltpu.VMEM((tl, W_Q + W_K + W_V), F32)],
        compiler_params=_cparams(),
        name="in_proj",
    )(xt, xt, xt, modl, modl, wcat, cos_t, sin_t, qw, kw, ones_q, ones_k, conv_w, conv_b)


def _attn_body(qt_ref, k_ref, vt_ref, o_ref, s_buf, p_buf, acc_ref, *, n_keys, kc):
    nh, hd, tq = qt_ref.shape[1:]
    qt = jnp.concatenate([qt_ref[0, h] for h in range(nh)], axis=1)
    cols = nh * tq
    rem = n_keys % kc
    chunks = ([(0, rem)] if rem else []) + [(rem + kc * i, kc) for i in range(n_keys // kc)]
    n = len(chunks)

    def scores(c):
        s0, sz = chunks[c]
        s_buf[c % 2, :sz, :] = jnp.dot(k_ref[0, 0, s0:s0 + sz, :], qt, preferred_element_type=F32)

    def softmax(c, m):
        sz = chunks[c][1]
        s = s_buf[c % 2, :sz, :]
        m_new = jnp.maximum(m, jnp.max(s, axis=0, keepdims=True))
        p_buf[c % 2, :sz, :] = jnp.exp2(s - m_new).astype(BF16)
        return m_new, jnp.exp2(m - m_new)

    def weighted_values(c, alpha):
        s0, sz = chunks[c]
        acc_ref[...] = alpha * acc_ref[...] + jnp.dot(vt_ref[0, 0, :, s0:s0 + sz], p_buf[c % 2, :sz, :],
                                                      preferred_element_type=F32)

    acc_ref[...] = jnp.zeros_like(acc_ref)
    m = jnp.full((1, cols), -jnp.inf, F32)
    alpha = None
    scores(0)
    for j in range(n + 1):
        if j + 1 < n:
            scores(j + 1)
        prev_alpha = alpha
        if j < n:
            m, alpha = softmax(j, m)
        if j >= 1:
            weighted_values(j - 1, prev_alpha)
    o = acc_ref[:hd, :] * (1.0 / acc_ref[hd:hd + 1, :])
    for h in range(nh):
        o_ref[0, h * hd:(h + 1) * hd, :] = o[:, h * tq:(h + 1) * tq]


def _attention(qt, k, vt, n_q, q_off, n_keys, tq):
    b = qt.shape[0]
    assert q_off % tq == 0
    off = q_off // tq
    kc = min(ATTN_KEY_CHUNK, n_keys)
    cols = Q_PER_KV * tq
    vrows = vt.shape[2]
    return pl.pallas_call(
        functools.partial(_attn_body, n_keys=n_keys, kc=kc),
        grid=(b, KV_HEADS, n_q // tq),
        scratch_shapes=[pltpu.VMEM((2, kc, cols), F32), pltpu.VMEM((2, kc, cols), BF16),
                        pltpu.VMEM((vrows, cols), F32)],
        in_specs=[
            pl.BlockSpec((1, Q_PER_KV, HEAD_DIM, tq), lambda bb, g, i: (bb, g, 0, i + off)),
            pl.BlockSpec((1, 1, n_keys, HEAD_DIM), lambda bb, g, i: (bb, g, 0, 0)),
            pl.BlockSpec((1, 1, vrows, n_keys), lambda bb, g, i: (bb, g, 0, 0)),
        ],
        out_specs=pl.BlockSpec((1, Q_PER_KV * HEAD_DIM, tq), lambda bb, g, i: (bb, g, i)),
        out_shape=jax.ShapeDtypeStruct((b, KV_HEADS * Q_PER_KV * HEAD_DIM, n_q), F32),
        compiler_params=_cparams(),
        name="gqa_attention",
    )(qt, k, vt)


def _seq_order(d, c, nc):
    return jnp.where(c == 0, 0, jnp.where(d == 0, c, nc - c))


def _ssd_body(xf_ref, xb_ref, dtf_ref, dtb_ref, bias_ref, a_ref, dvec_ref, yf_ref, yb_ref, s_scr):
    @pl.when(pl.program_id(1) == 0)
    def _():
        s_scr[...] = jnp.zeros_like(s_scr)

    nb = xf_ref.shape[0]
    streams = range(2 * nb)
    xbc = [(xf_ref, xb_ref)[k % 2][k // 2] for k in streams]
    q = xbc[0].shape[0]
    xw = SSD_HEADS * SSD_HEAD_DIM
    gw = 2 * SSD_STATE
    x = [v[:, :xw] for v in xbc]
    bm = [v[:, xw:xw + gw] for v in xbc]
    cm = [v[:, xw + gw:xw + 2 * gw] for v in xbc]
    dt_raw = [(dtf_ref, dtb_ref)[k % 2][k // 2] for k in streams]
    dt = [_softplus(dt_raw[k] + bias_ref[k % 2]) for k in streams]
    da = [dt[k] * a_ref[k % 2] for k in streams]
    ii = lax.broadcasted_iota(I32, (q, q), 0)
    jj = lax.broadcasted_iota(I32, (q, q), 1)
    mask = [jj <= ii, jj >= ii]
    tm = [jnp.where(m_, 1.0, 0.0).astype(BF16) for m_ in mask]
    da_hi = [da[k].astype(BF16) for k in streams]
    da_lo = [(da[k] - da_hi[k].astype(F32)).astype(BF16) for k in streams]
    cs = [jnp.dot(tm[k % 2], da_hi[k], preferred_element_type=F32)
          + jnp.dot(tm[k % 2], da_lo[k], preferred_element_type=F32) for k in streams]
    tot = [cs[k][q - 1:q, :] if k % 2 == 0 else cs[k][0:1, :] for k in streams]
    cst = [cs[k].T for k in streams]
    dec = [jnp.exp(tot[k] - cs[k]) for k in streams]
    ecs = [jnp.exp(cs[k]) for k in streams]
    etot = [jnp.exp(tot[k]) for k in streams]
    dt_t = [dt[k].T for k in streams]
    w_t = [(dt[k] * dec[k]).T for k in streams]
    x_t = [x[k].T for k in streams]
    ys = [[] for _ in streams]
    nt = (((1,), (1,)), ((), ()))
    tn = (((0,), (0,)), ((), ()))
    for g in range(2):
        gs = slice(g * SSD_STATE, (g + 1) * SSD_STATE)
        bg = [bm[k][:, gs].astype(BF16) for k in streams]
        cg = [cm[k][:, gs].astype(BF16) for k in streams]
        gmat = [lax.dot_general(cg[k], bg[k], nt, preferred_element_type=F32) for k in streams]
        for hh in range(SSD_HEADS // 2):
            h = 2 * g + hh
            hs = slice(h * SSD_HEAD_DIM, (h + 1) * SSD_HEAD_DIM)
            lmat = [jnp.exp(jnp.where(mask[k % 2], cs[k][:, h:h + 1] - cst[k][h:h + 1, :], -jnp.inf))
                    for k in streams]
            y_diag = [_bdot(gmat[k] * lmat[k] * dt_t[k][h:h + 1, :], x[k][:, hs]) for k in streams]
            s_in = [s_scr[k, h] for k in streams]
            y_off = [lax.dot_general(cg[k], s_in[k].astype(BF16), nt, preferred_element_type=F32) * ecs[k][:, h:h + 1]
                     for k in streams]
            xd_t = [(x_t[k][hs, :] * w_t[k][h:h + 1, :]).astype(BF16) for k in streams]
            for k in streams:
                ys[k].append(y_diag[k] + y_off[k])
                s_scr[k, h] = etot[k][:, h:h + 1] * s_in[k] + jnp.dot(xd_t[k], bg[k], preferred_element_type=F32)
    for k in streams:
        if k % 2 == 0:
            yf_ref[k // 2] = jnp.concatenate(ys[k], axis=1) + x[k] * dvec_ref[...]
        else:
            yb_ref[k // 2] = jnp.concatenate(ys[k], axis=1)


def _ssd(xbc, dt, bias, a_neg, dvec):
    b, t, cw = xbc.shape
    q = SEQ_TILE
    nc = t // q
    xw = SSD_HEADS * SSD_HEAD_DIM
    fwd = lambda c: c
    bwd = lambda c: _seq_order(1, c, nc)
    nb = SSD_BATCH
    return pl.pallas_call(
        _ssd_body,
        grid=(b // nb, nc),
        in_specs=[
            pl.BlockSpec((nb, q, cw), lambda bb, c: (bb, fwd(c), 0)),
            pl.BlockSpec((nb, q, cw), lambda bb, c: (bb, bwd(c), 0)),
            pl.BlockSpec((nb, q, LANES), lambda bb, c: (bb, fwd(c), 0)),
            pl.BlockSpec((nb, q, LANES), lambda bb, c: (bb, bwd(c), 1)),
            pl.BlockSpec((2, 1, LANES), lambda bb, c: (0, 0, 0)),
            pl.BlockSpec((2, 1, LANES), lambda bb, c: (0, 0, 0)),
            pl.BlockSpec((1, xw), lambda bb, c: (0, 0)),
        ],
        out_specs=[pl.BlockSpec((nb, q, xw), lambda bb, c: (bb, fwd(c), 0)),
                   pl.BlockSpec((nb, q, xw), lambda bb, c: (bb, bwd(c), 0))],
        out_shape=[jax.ShapeDtypeStruct((b, t, xw), F32)] * 2,
        scratch_shapes=[pltpu.VMEM((2 * nb, SSD_HEADS, SSD_HEAD_DIM, SSD_STATE), F32)],
        compiler_params=_cparams(),
        name="ssd_scan",
    )(xbc, xbc, dt, dt, bias, a_neg, dvec)


def _rg_body(u_ref, w_ref, bias_ref, lam_ref, y_ref, a_s, v_s, o_s, h_s, *, pitch):
    d = pl.program_id(0)
    c = pl.program_id(1)
    nb, tl, width = u_ref.shape
    ng = width // LANES

    @pl.when(c == 0)
    def _():
        h_s[...] = jnp.zeros_like(h_s)

    sp = _softplus(-lam_ref[0])
    w = w_ref[0]
    bias = bias_ref[0]
    for b in range(nb):
        ub = u_ref[b]
        pre = jnp.dot(ub.astype(BF16), w, preferred_element_type=F32) + bias
        r = _sigmoid(pre[:, :width])
        ig = _sigmoid(pre[:, width:])
        a = jnp.exp((-RG_C) * r * sp)
        v = jnp.sqrt(1.0 - a * a) * ig * ub
        for j in range(ng):
            a_s[j, pl.ds(b * pitch, tl), :] = a[:, j * LANES:(j + 1) * LANES]
            v_s[j, pl.ds(b * pitch, tl), :] = v[:, j * LANES:(j + 1) * LANES]

    def step(t, hs):
        te = jnp.where(d == 0, t, tl - 1 - t)
        out = []
        for j in range(ng):
            at = a_s[j, pl.ds(te, nb, stride=pitch), :]
            vt = v_s[j, pl.ds(te, nb, stride=pitch), :]
            hj = at * hs[j] + vt
            o_s[j, pl.ds(te, nb, stride=pitch), :] = hj
            out.append(hj)
        return tuple(out)

    h0 = tuple(h_s[:, j * LANES:(j + 1) * LANES] for j in range(ng))
    hf = lax.fori_loop(0, tl, step, h0, unroll=8)
    for j in range(ng):
        h_s[:, j * LANES:(j + 1) * LANES] = hf[j]
    for b in range(nb):
        y_ref[0, b] = jnp.concatenate([o_s[j, pl.ds(b * pitch, tl), :] for j in range(ng)], axis=1)


def _rglru(u, wg, bias, lam):
    b, t, width = u.shape
    assert b == SUBLANES, "the recurrence keeps one sample per sublane"
    tl = SEQ_TILE
    nc = t // tl
    pitch = tl + SUBLANES
    ng = width // LANES
    slab = pltpu.VMEM((ng, b * pitch, LANES), F32)
    return pl.pallas_call(
        functools.partial(_rg_body, pitch=pitch),
        grid=(2, nc),
        in_specs=[
            pl.BlockSpec((b, tl, width), lambda d, c: (0, _seq_order(d, c, nc), 0)),
            pl.BlockSpec((1, width, 2 * width), lambda d, c: (d, 0, 0)),
            pl.BlockSpec((1, 1, 2 * width), lambda d, c: (d, 0, 0)),
            pl.BlockSpec((1, 1, width), lambda d, c: (d, 0, 0)),
        ],
        out_specs=pl.BlockSpec((1, b, tl, width), lambda d, c: (d, 0, _seq_order(d, c, nc), 0)),
        out_shape=jax.ShapeDtypeStruct((2, b, t, width), F32),
        scratch_shapes=[slab, slab, slab, pltpu.VMEM((b, width), F32)],
        compiler_params=_cparams(),
        name="rglru_scan",
    )(u, wg, bias, lam)


def _rms(x, w):
    return x * lax.rsqrt(jnp.mean(x * x, axis=-1, keepdims=True) + EPS) * w


def _layer_norm(t, w, b):
    mu = jnp.mean(t, axis=-1, keepdims=True)
    tc = t - mu
    var = jnp.mean(tc * tc, axis=-1, keepdims=True)
    return tc * lax.rsqrt(var + EPS) * w + b


def _gelu_tanh(x):
    return 0.5 * x * (1.0 + jnp.tanh(math.sqrt(2.0 / math.pi) * (x + 0.044715 * (x * x * x))))


def _pack_bf16_pair(lo, hi):
    lb = pltpu.bitcast(lo.astype(BF16).astype(F32), jnp.uint32)
    hb = pltpu.bitcast(hi.astype(BF16).astype(F32), jnp.uint32)
    return (lb >> 16) | (hb & jnp.uint32(0xFFFF0000))


def _merge_body(*refs, has_ctx):
    if has_ctx:
        actx_ref, refs = refs[0], refs[1:]
    (alat_ref, ysf_ref, ysb_ref, z_ref, hr_ref, g_ref, x_ref, g1_ref, sc2_ref, sh2_ref, aw_ref, sw_ref, rw_ref,
     wo_ref, lnw_ref, lnb_ref, wr_ref, x1_ref, hp_ref, aff_ref, afft_ref, cat_scr, proj_scr) = refs
    tl = x_ref.shape[1]
    nh = MERGE_ROW_SPLIT
    rows = [slice(j * tl // nh, (j + 1) * tl // nh) for j in range(nh)]
    for rs in rows:
        a = alat_ref[0, :, rs]
        if has_ctx:
            a = jnp.where(pl.program_id(1) == 0, actx_ref[0, :, rs], a)
        an = _rms(a.T, aw_ref[...])
        sn = _rms((ysf_ref[0, rs] + ysb_ref[0, rs]) * _silu(z_ref[0, rs]), sw_ref[...])
        rn = _rms((hr_ref[0, 0, rs] + hr_ref[1, 0, rs]) * _gelu_tanh(g_ref[0, rs]), rw_ref[...])
        cat_scr[rs] = jnp.concatenate([an, sn, rn], axis=1).astype(BF16)
    for rs in rows:
        proj_scr[rs] = jnp.dot(cat_scr[rs], wo_ref[...], preferred_element_type=F32)
    for rs in rows:
        x1 = _layer_norm(DEEPNORM_ALPHA * x_ref[0, rs] + g1_ref[0] * proj_scr[rs], lnw_ref[...], lnb_ref[...])
        x1_ref[0, rs] = x1
        h2 = x1 * (1.0 + sc2_ref[0]) + sh2_ref[0]
        half = h2.shape[1] // 2
        hp_ref[0, rs] = _pack_bf16_pair(h2[:, :half], h2[:, half:])
        logits = jnp.dot(h2.astype(BF16), wr_ref[...], preferred_element_type=F32)
        lane = lax.broadcasted_iota(I32, logits.shape, 1)
        logits = jnp.where(lane < N_EXPERTS, logits, -jnp.inf)
        e = jnp.exp(logits - jnp.max(logits, axis=-1, keepdims=True))
        aff = e / jnp.sum(e, axis=-1, keepdims=True)
        aff_ref[0, rs] = aff
        afft_ref[0, :, rs] = aff.T[:N_EXPERTS, :]


def _merge(attn_ctx, attn_lat, ys, z, hr, g, xt, modl, aw, sw, rw, wo, lnw, lnb, wr, ctx_row, off):
    b, t, d = xt.shape
    tl = SEQ_TILE
    nt = t // tl - off
    has_ctx = off == 0
    aw_ = attn_lat.shape[1]
    seq = lambda w: pl.BlockSpec((1, tl, w), lambda bb, i: (bb, i + off, 0))
    pair = lambda w: pl.BlockSpec((2, 1, tl, w), lambda bb, i: (0, bb, i + off, 0))
    full = lambda shape: pl.BlockSpec(shape, lambda bb, i: (0,) * len(shape))
    out = lambda w: pl.BlockSpec((1, tl, w), lambda bb, i: (bb, i, 0))
    lat_off = 1 - off
    in_specs = [
        pl.BlockSpec((1, aw_, tl), lambda bb, i: (bb, 0, jnp.maximum(i - lat_off, 0))),
        seq(ys[0].shape[-1]), seq(ys[1].shape[-1]), seq(z.shape[-1]), pair(hr.shape[-1]), seq(g.shape[-1]), seq(d),
        _mod_spec(2, d, ctx_row, off), _mod_spec(4, d, ctx_row, off), _mod_spec(3, d, ctx_row, off),
        full(aw.shape), full(sw.shape), full(rw.shape), full(wo.shape), full(lnw.shape), full(lnb.shape),
        full(wr.shape),
    ]
    args = [attn_lat, ys[0], ys[1], z, hr, g, xt, modl, modl, modl, aw, sw, rw, wo, lnw, lnb, wr]
    if has_ctx:
        in_specs = [pl.BlockSpec((1, aw_, tl), lambda bb, i: (bb, 0, 0))] + in_specs
        args = [attn_ctx] + args
    rows = nt * tl
    return pl.pallas_call(
        functools.partial(_merge_body, has_ctx=has_ctx),
        grid=(b, nt),
        in_specs=in_specs,
        out_specs=[out(d), out(d // 2), out(LANES), pl.BlockSpec((1, N_EXPERTS, tl), lambda bb, i: (bb, 0, i))],
        out_shape=[
            jax.ShapeDtypeStruct((b, rows, d), F32),
            jax.ShapeDtypeStruct((b, rows, d // 2), jnp.uint32),
            jax.ShapeDtypeStruct((b, rows, LANES), F32),
            jax.ShapeDtypeStruct((b, N_EXPERTS, rows), F32),
        ],
        scratch_shapes=[pltpu.VMEM((tl, wo.shape[0]), BF16), pltpu.VMEM((tl, d), F32)],
        compiler_params=_cparams(),
        name="merge_outproj_ln1_router",
    )(*args)


def _topk_body(aff_ref, idx_ref, cum_scr, *, cap):
    aff = aff_ref[0]
    ne, n = aff.shape
    bits = pltpu.bitcast(aff, I32)

    def search(i, thr):
        cand = thr | lax.shift_left(jnp.int32(1), 30 - i)
        cnt = jnp.sum(jnp.where(bits >= cand, 1.0, 0.0), axis=1, keepdims=True)
        return jnp.where(cnt >= cap, cand, thr)

    thr = lax.fori_loop(0, 31, search, jnp.zeros((ne, 1), I32))
    gt = jnp.where(bits > thr, 1.0, 0.0)
    eq = jnp.where(bits == thr, 1.0, 0.0)
    need = cap - jnp.sum(gt, axis=1, keepdims=True)
    nblk = n // LANES
    r_ = lax.broadcasted_iota(I32, (LANES, LANES), 0)
    c_ = lax.broadcasted_iota(I32, (LANES, LANES), 1)
    upper = jnp.where(r_ <= c_, 1.0, 0.0).astype(BF16)

    def prefix(blocks):
        outs = []
        off = jnp.zeros((ne, 1), F32)
        for mk in blocks:
            w = jnp.dot(mk.astype(BF16), upper, preferred_element_type=F32) + off
            outs.append(w)
            off = w[:, LANES - 1:LANES]
        return outs

    blk = lambda a, k: a[:, k * LANES:(k + 1) * LANES]
    tie_rank = prefix([blk(eq, k) for k in range(nblk)])
    cum = prefix([jnp.maximum(blk(gt, k), blk(eq, k) * jnp.where(tie_rank[k] <= need, 1.0, 0.0))
                  for k in range(nblk)])
    kpad = cum_scr.shape[0] // ne
    for k in range(kpad):
        cum_scr[k * ne:(k + 1) * ne, :] = cum[k] if k < nblk else jnp.zeros((ne, LANES), F32)
    width = -(-cap // LANES) * LANES
    slot = lax.broadcasted_iota(I32, (LANES, width), 1).astype(F32)
    blk_id = lax.broadcasted_iota(I32, (LANES, width), 0).astype(F32)
    blk_col = lax.broadcasted_iota(I32, (LANES, 1), 0)
    tn = (((0,), (0,)), ((), ()))

    def per_expert(e, carry):
        cum_e = cum_scr[pl.ds(e, kpad, stride=ne), :]
        if kpad < LANES:
            cum_e = jnp.concatenate([cum_e, jnp.zeros((LANES - kpad, LANES), F32)], axis=0)
        ends = jnp.where(blk_col < nblk, cum_e[:, LANES - 1:LANES], float(2 * n))
        full = jnp.sum(jnp.where(ends <= slot, 1.0, 0.0), axis=0, keepdims=True)
        pick = jnp.where(blk_id == full, 1.0, 0.0).astype(BF16)
        hi = jnp.floor(cum_e * (1.0 / 32.0))
        lo = cum_e - 32.0 * hi
        straddle = (32.0 * lax.dot_general(hi.astype(BF16), pick, tn, preferred_element_type=F32)
                    + lax.dot_general(lo.astype(BF16), pick, tn, preferred_element_type=F32))
        inside = jnp.sum(jnp.where(straddle <= slot, 1.0, 0.0), axis=0, keepdims=True)
        idx_ref[0, pl.ds(e, 1), :] = (float(LANES) * full + inside)[:, :cap].astype(I32)
        return carry

    lax.fori_loop(0, ne, per_expert, 0, unroll=4)


def _topk(aff_t, cap):
    b, e, n = aff_t.shape
    return pl.pallas_call(
        functools.partial(_topk_body, cap=cap),
        grid=(b,),
        in_specs=[pl.BlockSpec((1, e, n), lambda bb: (bb, 0, 0))],
        out_specs=pl.BlockSpec((1, e, cap), lambda bb: (bb, 0, 0)),
        out_shape=jax.ShapeDtypeStruct((b, e, cap), I32),
        scratch_shapes=[pltpu.VMEM((max(n // LANES, SUBLANES) * e, LANES), F32)],
        compiler_params=_cparams(),
        name="expert_choice_topk",
    )(aff_t)


def _moe_body(idxp_ref, idxc_ref, idxn_ref, hp_ref, aff_ref, wg_ref, wu_ref, wd_ref, out_ref,
              xs_scr, ag_scr, y_scr, xb_scr, hid_scr, *, cap):
    e = pl.program_id(1)
    f = pl.program_id(2)
    ne = pl.num_programs(1)
    nf = pl.num_programs(2)
    cur = e % 2
    oth = 1 - cur

    @pl.when((pl.program_id(0) == 0) & (e == 0) & (f == 0))
    def _():
        y_scr[...] = jnp.zeros_like(y_scr)

    @pl.when((e == 0) & (f == 0))
    def _():
        out_ref[...] = jnp.zeros_like(out_ref)

        def gather(j, carry):
            for u in range(SUBLANES):
                t = idxc_ref[0, 0, 0, j * SUBLANES + u]
                xs_scr[0, j, u:u + 1, :] = hp_ref[0, pl.ds(t, 1), :]
                ag_scr[0, j, u:u + 1, :] = aff_ref[0, pl.ds(t, 1), :]
            return carry

        lax.fori_loop(0, cap // SUBLANES, gather, 0)

    w = xs_scr[cur].reshape(cap, xs_scr.shape[-1])
    half = w.shape[1]
    xb_scr[:, :half] = pltpu.bitcast(w << 16, F32).astype(BF16)
    xb_scr[:, half:] = pltpu.bitcast(w & jnp.uint32(0xFFFF0000), F32).astype(BF16)
    lane = lax.broadcasted_iota(I32, (cap, LANES), 1)
    gate = jnp.sum(jnp.where(lane == e, ag_scr[cur].reshape(cap, LANES), 0.0), axis=1, keepdims=True)
    share = cap // nf
    base = f * share
    base_tile = f * (share // SUBLANES)
    has_prev = e > 0

    def row_copies(r0, r1):
        for g0 in range(r0, r1, MOE_ROW_GROUP):
            rows = range(g0, min(g0 + MOE_ROW_GROUP, r1))
            tps = [idxp_ref[0, 0, 0, base + r] for r in rows]
            sums = [out_ref[0, pl.ds(tp, 1), :]
                    + jnp.where(has_prev, y_scr[oth, base_tile + r // SUBLANES, r % SUBLANES:r % SUBLANES + 1, :], 0.0)
                    for tp, r in zip(tps, rows)]
            for tp, v in zip(tps, sums):
                out_ref[0, pl.ds(tp, 1), :] = v
            for r in rows:
                tn = idxn_ref[0, 0, 0, base + r]
                j, u = base_tile + r // SUBLANES, r % SUBLANES
                xs_scr[oth, j, u:u + 1, :] = hp_ref[0, pl.ds(tn, 1), :]
                arow = aff_ref[0, pl.ds(tn, 1), :]
                ag_scr[oth, j, u:u + 1, :] = arow
        zero = jnp.right_shift(tps[-1], 31)
        bits = pltpu.bitcast(arow, I32) | pltpu.bitcast(sums[-1][:, :LANES], I32)
        return (bits & zero).astype(F32)[:, :1]

    tf = wg_ref.shape[3]
    nsl = tf // MOE_FF_SLICE
    sl = lambda c: slice(c * MOE_FF_SLICE, (c + 1) * MOE_FF_SLICE)

    def hidden(c, dep):
        xb = xb_scr[...]
        hg = jnp.dot(xb, wg_ref[0, 0, :, sl(c)], preferred_element_type=F32)
        hu = jnp.dot(xb, wu_ref[0, 0, :, sl(c)], preferred_element_type=F32)
        if dep is not None:
            hu = hu + dep
        hid_scr[:, sl(c)] = (_silu(hg) * hu).astype(BF16)

    dep = None
    for c in range(nsl):
        hidden(c, dep)
        dep = row_copies(share * c // nsl, share * (c + 1) // nsl)
    yp = jnp.dot(hid_scr[...], wd_ref[0, 0], preferred_element_type=F32)
    y_old = y_scr[cur].reshape(cap, y_scr.shape[-1])
    y_new = (jnp.where(f == 0, 0.0, y_old) + yp) * jnp.where(f == nf - 1, gate, 1.0)
    y_scr[cur] = y_new.reshape(y_scr.shape[1:])

    @pl.when((e == ne - 1) & (f == nf - 1))
    def _():
        def scatter(j, carry):
            for u in range(SUBLANES):
                t = idxc_ref[0, 0, 0, j * SUBLANES + u]
                out_ref[0, pl.ds(t, 1), :] = out_ref[0, pl.ds(t, 1), :] + y_scr[cur, j, u:u + 1, :]
            return carry

        lax.fori_loop(0, cap // SUBLANES, scatter, 0)


def _moe(idx, hp, aff, wg, wu, wd, layer, tf):
    b, t, half = hp.shape
    d = 2 * half
    ne, ff = wg.shape[1], wg.shape[3]
    cap = idx.shape[-1]
    nf = ff // tf
    assert nf >= 2 and cap % nf == 0
    one = pl.Buffered(1)
    idx4 = idx.reshape(b, ne, 1, cap)
    smem = lambda shift: pl.BlockSpec((1, 1, 1, cap), lambda bb, e, f: (bb, jnp.clip(e + shift, 0, ne - 1), 0, 0),
                                      memory_space=pltpu.SMEM)
    return pl.pallas_call(
        functools.partial(_moe_body, cap=cap),
        grid=(b, ne, nf),
        in_specs=[
            smem(-1), smem(0), smem(1),
            pl.BlockSpec((1, t, half), lambda bb, e, f: (bb, 0, 0), pipeline_mode=one),
            pl.BlockSpec((1, t, LANES), lambda bb, e, f: (bb, 0, 0), pipeline_mode=one),
            pl.BlockSpec((1, 1, d, tf), lambda bb, e, f: (layer, e, 0, f)),
            pl.BlockSpec((1, 1, d, tf), lambda bb, e, f: (layer, e, 0, f)),
            pl.BlockSpec((1, 1, tf, d), lambda bb, e, f: (layer, e, f, 0)),
        ],
        out_specs=pl.BlockSpec((1, t, d), lambda bb, e, f: (bb, 0, 0), pipeline_mode=one),
        out_shape=jax.ShapeDtypeStruct((b, t, d), F32),
        scratch_shapes=[
            pltpu.VMEM((2, cap // SUBLANES, SUBLANES, half), jnp.uint32),
            pltpu.VMEM((2, cap // SUBLANES, SUBLANES, LANES), F32),
            pltpu.VMEM((2, cap // SUBLANES, SUBLANES, d), F32),
            pltpu.VMEM((cap, d), BF16),
            pltpu.VMEM((cap, tf), BF16),
        ],
        compiler_params=_cparams(VMEM_LIMIT_MOE),
        name="expert_ffn",
    )(idx4, idx4, idx4, hp, aff, wg, wu, wd)


def _ln2_body(x1_ref, moe_ref, g2_ref, g2c_ref, w_ref, b_ref, o_ref, *, n_ctx):
    g2 = g2_ref[0]
    if n_ctx:
        tl = x1_ref.shape[1]
        row = lax.broadcasted_iota(I32, (tl, 1), 0) + pl.program_id(1) * tl
        g2 = jnp.where(row < n_ctx, g2c_ref[0], g2)
    o_ref[0] = _layer_norm(DEEPNORM_ALPHA * x1_ref[0] + g2 * moe_ref[0], w_ref[...], b_ref[...])


def _ln2(x1, moe, modl, w, bias, ctx_row, n_ctx):
    b, t, d = x1.shape
    tl = t // LN2_TILES
    assert t % LN2_TILES == 0 and tl % SUBLANES == 0
    seq = pl.BlockSpec((1, tl, d), lambda bb, i: (bb, i, 0))
    full = pl.BlockSpec((1, d), lambda bb, i: (0, 0))
    return pl.pallas_call(
        functools.partial(_ln2_body, n_ctx=n_ctx),
        grid=(b, LN2_TILES),
        in_specs=[seq, seq,
                  pl.BlockSpec((1, 1, d), lambda bb, i: (bb, 0, 5)),
                  pl.BlockSpec((1, 1, d), lambda bb, i: (ctx_row, 0, 5)),
                  full, full],
        out_specs=seq,
        out_shape=jax.ShapeDtypeStruct((b, t, d), F32),
        compiler_params=_cparams(),
        name="ln2",
    )(x1, moe, modl, modl, w, bias)


def _block_diag_ones(width, block):
    r = jnp.arange(width)[:, None] // block
    c = jnp.arange(width)[None, :] // block
    return (r == c).astype(BF16)


def _block_diag(w):
    k, d, e = w.shape
    eye = jnp.eye(k, dtype=w.dtype)
    return (eye[:, None, :, None] * w[:, :, None, :]).reshape(k * d, k * e)


def _rope_tables(n_ctx, n_lat):
    pos = jnp.arange(n_lat)
    row = (pos // GRID_W).astype(F32)
    col = (pos % GRID_W).astype(F32)
    inv_freq = ROPE_THETA ** (-jnp.arange(0, AXIS_DIM, 2, dtype=F32) / AXIS_DIM)
    ang_r = row[:, None] * inv_freq
    ang_c = col[:, None] * inv_freq
    cos_h = jnp.concatenate([jnp.cos(ang_r)] * 2 + [jnp.cos(ang_c)] * 2, axis=1)
    sin_h = jnp.concatenate([-jnp.sin(ang_r), jnp.sin(ang_r), -jnp.sin(ang_c), jnp.sin(ang_c)], axis=1)
    reps = LANES // HEAD_DIM
    cos_t = jnp.concatenate([jnp.ones((n_ctx, LANES), F32), jnp.tile(cos_h, (1, reps))], axis=0)
    sin_t = jnp.concatenate([jnp.zeros((n_ctx, LANES), F32), jnp.tile(sin_h, (1, reps))], axis=0)
    return cos_t, sin_t


def _layer(xt, modl, rope, p, experts, layer, last, n_ctx, ctx_row):
    b, t, d = xt.shape
    n_lat = t - n_ctx
    cos_t, sin_t = rope
    w = p["w_in"]
    pad = jnp.zeros((d, LANES - SSD_HEADS), F32)
    wcat = jnp.concatenate(
        [w[:, 0:768], w[:, 768:1280], w[:, 1544:1800], w[:, 1280:1536], w[:, 1800:2056],
         w[:, 1536:1540], pad, w[:, 1540:1544], pad], axis=1).astype(BF16)
    qw = jnp.tile(p["q_norm"], W_Q // HEAD_DIM)[None]
    kw = jnp.tile(p["k_norm"], W_K // HEAD_DIM)[None]
    conv_w = jnp.concatenate([p["ssd_conv_w"], p["rg_conv_w"]], axis=1)
    conv_b = jnp.concatenate([p["ssd_conv_b"], p["rg_conv_b"]])[None]
    qt, k, v, xbc, u, z, g, dt = _inproj(xt, modl, wcat, cos_t, sin_t, qw, kw,
                                         _block_diag_ones(W_Q, HEAD_DIM), _block_diag_ones(W_K, HEAD_DIM),
                                         conv_w, conv_b, p["ssd_conv_w"].shape[1], ctx_row)

    qt = qt.reshape(b, W_Q // HEAD_DIM, HEAD_DIM, t)
    kh = k.reshape(b, t, KV_HEADS, HEAD_DIM).transpose(0, 2, 1, 3)
    vt = v.reshape(b, t, KV_HEADS, HEAD_DIM).transpose(0, 2, 3, 1)
    ones_pad = jnp.zeros((b, KV_HEADS, BF16_ROWS, t), BF16).at[:, :, 0, :].set(1.0)
    vt = jnp.concatenate([vt, ones_pad], axis=2)
    attn_lat = _attention(qt[..., n_ctx:], kh, vt, n_lat, 0, t, ATTN_Q_TILE)
    attn_ctx = None if last else _attention(qt, kh, vt, n_ctx, 0, n_ctx, min(ATTN_Q_TILE, n_ctx))

    lane_pad = lambda a: jnp.pad(a, ((0, 0), (0, LANES - a.shape[1])))[:, None, :]
    ys = _ssd(xbc, dt, lane_pad(p["ssd_dt_bias"]), lane_pad(-jnp.exp(p["ssd_a_log"])),
              jnp.repeat(p["ssd_d"], SSD_HEAD_DIM)[None])
    wgate = jnp.stack([jnp.concatenate([_block_diag(p["rg_wa"][j]), _block_diag(p["rg_wx"][j])], axis=1)
                       for j in range(2)]).astype(BF16)
    bgate = jnp.concatenate([p["rg_ba"], p["rg_bx"]], axis=1)[:, None, :]
    hr = _rglru(u, wgate, bgate, p["rg_lambda"][:, None, :])

    off = 1 if last else 0
    wr = jnp.pad(p["w_router"], ((0, 0), (0, LANES - N_EXPERTS))).astype(BF16)
    x1, hp, aff, aff_t = _merge(attn_ctx, attn_lat, ys, z, hr, g, xt, modl,
                                p["attn_out_norm"][None], p["ssd_norm"][None], p["rg_out_norm"][None],
                                p["w_out"].astype(BF16), p["ln1_w"][None], p["ln1_b"][None], wr, ctx_row, off)

    if last:
        idx = _topk(aff_t, EC_CAPACITY * n_lat // N_EXPERTS)
    else:
        idx_lat = _topk(aff_t[:, :, n_ctx:], EC_CAPACITY * n_lat // N_EXPERTS) + n_ctx
        idx_ctx = _topk(aff_t[:, :, :n_ctx], EC_CAPACITY * n_ctx // N_EXPERTS)
        idx = jnp.concatenate([idx_lat, idx_ctx], axis=-1)
    moe = _moe(idx, hp, aff, *experts, layer, MOE_FF_TILE)
    return _ln2(x1, moe, modl, p["ln2_w"][None], p["ln2_b"][None], ctx_row, 0 if last else n_ctx)


def kernel(x, c, ctx, c_ctx, w_mod, b_mod, w_in, q_norm, k_norm, attn_out_norm, ssd_conv_w, ssd_conv_b, ssd_dt_bias, ssd_a_log, ssd_d, ssd_norm, rg_conv_w, rg_conv_b, rg_wa, rg_ba, rg_wx, rg_bx, rg_lambda, rg_out_norm, w_out, ln1_w, ln1_b, w_router, w_gate, w_up, w_down, ln2_w, ln2_b):
    b, n_lat, d = x.shape
    n_ctx = ctx.shape[1]
    assert n_ctx == SEQ_TILE and n_lat % 512 == 0 and b == SUBLANES
    params = dict(w_in=w_in, q_norm=q_norm, k_norm=k_norm, attn_out_norm=attn_out_norm, ssd_conv_w=ssd_conv_w,
                  ssd_conv_b=ssd_conv_b, ssd_dt_bias=ssd_dt_bias, ssd_a_log=ssd_a_log, ssd_d=ssd_d,
                  ssd_norm=ssd_norm, rg_conv_w=rg_conv_w, rg_conv_b=rg_conv_b, rg_wa=rg_wa, rg_ba=rg_ba,
                  rg_wx=rg_wx, rg_bx=rg_bx, rg_lambda=rg_lambda, rg_out_norm=rg_out_norm, w_out=w_out,
                  ln1_w=ln1_w, ln1_b=ln1_b, w_router=w_router, ln2_w=ln2_w, ln2_b=ln2_b)
    experts = (w_gate.astype(BF16), w_up.astype(BF16), w_down.astype(BF16))
    mod = _modulation(c, c_ctx, w_mod, b_mod)
    rope = _rope_tables(n_ctx, n_lat)
    xt = jnp.concatenate([ctx, x], axis=1)
    depth = w_mod.shape[0]
    for l in range(depth):
        p = {name: val[l] for name, val in params.items()}
        modl = mod[l].reshape(mod.shape[1], 1, mod.shape[2])
        xt = _layer(xt, modl, rope, p, experts, l, l == depth - 1, n_ctx, b)
    return xt
```
